```python
import math
import jax, jax.numpy as jnp
from jax import lax
import numpy as np

D_MODEL = 1024
BATCH = 2
SEQ = 8192
DEPTH = 1

D_PLE = 256
M_HEADS = 4
M_QK = 128
M_V = 256
M_WIDTH = M_HEADS * M_V
M_CHUNK = 64
CONV_W = 4
A_HEADS = 8
A_DH = 64
A_WIDTH = A_HEADS * A_DH
IDX_HEADS = 8
IDX_DH = 64
TOPK_MAX = 256
Q_BLOCK = 128
REL_BUCKETS = 32
REL_MAX_DIST = 128
D_MIX = M_WIDTH + A_WIDTH
EPS = 1e-6

PROJ_SIZES = [
    M_HEADS * M_QK,
    M_HEADS * M_QK,
    M_WIDTH,
    M_WIDTH,
    M_WIDTH,
    2 * M_HEADS,
    A_WIDTH,
    A_WIDTH,
    A_WIDTH,
    A_WIDTH,
    IDX_HEADS * IDX_DH,
    IDX_DH,
    IDX_HEADS,
]
PROJ_TOTAL = int(sum(PROJ_SIZES))
SPLIT_POINTS = [int(v) for v in np.cumsum(PROJ_SIZES)[:-1]]

kernel_name = "hybrid_mlstm_dsa_parallel_heads"


def rmsnorm(x, g):
    xf = x.astype(jnp.float32)
    y = xf * lax.rsqrt(jnp.mean(xf * xf, axis=-1, keepdims=True) + EPS)
    return (y * g.astype(jnp.float32)).astype(x.dtype)


def causal_dwconv(x, w):
    S = x.shape[1]
    xp = jnp.pad(x, ((0, 0), (CONV_W - 1, 0), (0, 0)))
    y = xp[:, 0:S] * w[0]
    for j in range(1, CONV_W):
        y = y + xp[:, j:j + S] * w[j]
    return y


def mlstm_chunkwise(q, k, v, log_i, log_f):
    f32 = jnp.float32
    B, S, H, Dk = q.shape
    Dv = v.shape[-1]
    L = M_CHUNK
    NC = S // L

    def to_chunks(a):
        return a.reshape((B, NC, L) + a.shape[2:]).swapaxes(0, 1)

    qc = to_chunks(q.astype(f32))
    kc = to_chunks(k.astype(f32) * (Dk ** -0.5))
    vc = to_chunks(v.astype(f32))
    ic = to_chunks(log_i.astype(f32))
    fc = to_chunks(log_f.astype(f32))
    causal = jnp.tril(jnp.ones((L, L), dtype=bool))

    def step(carry, inp):
        C, n, m = carry
        q_, k_, v_, li, lf = inp
        b = jnp.cumsum(lf, axis=1).transpose(0, 2, 1)
        li_h = li.transpose(0, 2, 1)
        D = b[:, :, :, None] - b[:, :, None, :] + li_h[:, :, None, :]
        D = jnp.where(causal, D, -jnp.inf)
        inter = b + m[:, :, None]
        m_t = jnp.maximum(inter, D.max(-1))
        s = jnp.einsum('blhd,bshd->bhls', q_, k_) * jnp.exp(D - m_t[..., None])
        w_inter = jnp.exp(inter - m_t)
        num = (jnp.einsum('bhls,bshv->blhv', s, v_)
               + w_inter.transpose(0, 2, 1)[..., None] * jnp.einsum('blhd,bhdv->blhv', q_, C))
        den = s.sum(-1) + w_inter * jnp.einsum('blhd,bhd->bhl', q_, n)
        den = jnp.maximum(jnp.abs(den), jnp.exp(-m_t))
        h = num / den.transpose(0, 2, 1)[..., None]
        b_L = b[:, :, -1]
        g = b_L[:, :, None] - b + li_h
        m_new = jnp.maximum(b_L + m, g.max(-1))
        wk = jnp.exp(g - m_new[..., None])
        decay = jnp.exp(b_L + m - m_new)
        C_new = decay[..., None, None] * C + jnp.einsum('bhs,bshd,bshv->bhdv', wk, k_, v_)
        n_new = decay[..., None] * n + jnp.einsum('bhs,bshd->bhd', wk, k_)
        return (C_new, n_new, m_new), h

    init = (jnp.zeros((B, H, Dk, Dv), f32), jnp.zeros((B, H, Dk), f32), jnp.zeros((B, H), f32))
    _, hs = lax.scan(step, init, (qc, kc, vc, ic, fc))
    return hs.swapaxes(0, 1).reshape(B, S, H, Dv)


def t5_bucket(dist):
    max_exact = REL_BUCKETS // 2
    large = max_exact + (jnp.log(jnp.maximum(dist, 1).astype(jnp.float32) / max_exact)
                         / math.log(REL_MAX_DIST / max_exact) * (REL_BUCKETS - max_exact)).astype(jnp.int32)
    large = jnp.minimum(large, REL_BUCKETS - 1)
    return jnp.where(dist < max_exact, dist, large)


def dsa_attention(q, k, v, iq, ik, iw, rel_bias):
    B, S, H, dh = q.shape
    topk = min(TOPK_MAX, S // 4)
    NB = S // Q_BLOCK
    key_pos = jnp.arange(S)
    iq = iq * (IDX_DH ** -0.5)
    iw = iw * (IDX_HEADS ** -0.5)

    def blockify(a):
        return a.reshape((B, NB, Q_BLOCK) + a.shape[2:]).swapaxes(0, 1)

    def one_block(args):
        qb, iqb, iwb, bi = args
        t = bi * Q_BLOCK + jnp.arange(Q_BLOCK)
        sc = jax.nn.relu(jnp.einsum('bthd,bsd->bths', iqb, ik))
        score = jnp.einsum('bths,bth->bts', sc, iwb).astype(jnp.float32)
        causal = key_pos[None, :] <= t[:, None]
        score = jnp.where(causal[None], score, -jnp.inf)
        _, idx = lax.top_k(score, topk)
        k_sel = jax.vmap(lambda kb, ib: kb[ib])(k, idx)
        v_sel = jax.vmap(lambda vb, ib: vb[ib])(v, idx)
        valid = idx <= t[None, :, None]
        bucket = t5_bucket(jnp.maximum(t[None, :, None] - idx, 0))
        bias = rel_bias[bucket].astype(jnp.float32).transpose(0, 1, 3, 2)
        logits = jnp.einsum('bthd,btkhd->bthk', qb, k_sel).astype(jnp.float32) * (dh ** -0.5) + bias
        logits = jnp.where(valid[:, :, None, :], logits, -jnp.inf)
        probs = jax.nn.softmax(logits, axis=-1).astype(v.dtype)
        return jnp.einsum('bthk,btkhd->bthd', probs, v_sel)

    out = lax.map(one_block, (blockify(q), blockify(iq), blockify(iw), jnp.arange(NB)))
    return out.swapaxes(0, 1).reshape(B, S, H, dh)


def setup_inputs(seed: int = 0) -> dict:
    key = jax.random.key(seed)
    ks = jax.random.split(key, 16)
    f32 = jnp.float32
    x = jax.random.normal(ks[0], (BATCH, SEQ, D_MODEL), f32)
    p = jax.random.normal(ks[1], (DEPTH, BATCH, SEQ, D_PLE), f32)
    w_in = jax.random.normal(ks[2], (DEPTH, D_MODEL, PROJ_TOTAL), f32) * D_MODEL ** -0.5
    i_bias = 0.1 * jax.random.normal(ks[3], (DEPTH, M_HEADS), f32)
    f_bias = jnp.linspace(3.0, 6.0, M_HEADS, dtype=f32)[None] + 0.1 * jax.random.normal(ks[4], (DEPTH, M_HEADS), f32)
    b_gate = jnp.concatenate([i_bias, f_bias], axis=-1)
    conv_w = jax.random.normal(ks[5], (DEPTH, CONV_W, 2 * M_HEADS * M_QK), f32) * CONV_W ** -0.5
    w_out = jax.random.normal(ks[6], (DEPTH, D_MIX, D_MODEL), f32) * D_MIX ** -0.5
    norm_in = 1.0 + 0.1 * jax.random.normal(ks[7], (DEPTH, D_MODEL), f32)
    m_norm = 1.0 + 0.1 * jax.random.normal(ks[8], (DEPTH, M_WIDTH), f32)
    rel_bias = 0.5 * jax.random.normal(ks[9], (REL_BUCKETS, A_HEADS), f32)
    w_ple = jax.random.normal(ks[10], (DEPTH, D_PLE, D_MODEL), f32) * D_PLE ** -0.5
    w_ple_gate = jax.random.normal(ks[11], (DEPTH, D_MODEL, D_MODEL), f32) * D_MODEL ** -0.5
    norm_final = 1.0 + 0.1 * jax.random.normal(ks[12], (D_MODEL,), f32)
    return {"x": x, "p": p, "w_in": w_in, "b_gate": b_gate, "conv_w": conv_w, "w_out": w_out,
            "norm_in": norm_in, "m_norm": m_norm, "rel_bias": rel_bias, "w_ple": w_ple,
            "w_ple_gate": w_ple_gate, "norm_final": norm_final}


def reference(x, p, w_in, b_gate, conv_w, w_out, norm_in, m_norm, rel_bias, w_ple, w_ple_gate, norm_final):
    B, S, _ = x.shape
    h = x
    for i in range(DEPTH):
        u = rmsnorm(h, norm_in[i])
        proj = u @ w_in[i]
        (m_q, m_k, m_v, m_o, m_z, m_if, a_q, a_k, a_v, a_z,
         ix_q, ix_k, ix_w) = jnp.split(proj, SPLIT_POINTS, axis=-1)

        qk = jax.nn.silu(causal_dwconv(jnp.concatenate([m_q, m_k], axis=-1), conv_w[i]))
        mq, mk = jnp.split(qk, 2, axis=-1)
        gates = (m_if + b_gate[i]).astype(jnp.float32)
        log_i = gates[..., :M_HEADS]
        log_f = jax.nn.log_sigmoid(gates[..., M_HEADS:])
        hm = mlstm_chunkwise(mq.reshape(B, S, M_HEADS, M_QK), mk.reshape(B, S, M_HEADS, M_QK),
                             m_v.reshape(B, S, M_HEADS, M_V), log_i, log_f)
        hm = hm * lax.rsqrt(jnp.mean(hm * hm, axis=-1, keepdims=True) + EPS)
        hm = hm.reshape(B, S, M_WIDTH) * m_norm[i].astype(jnp.float32)
        hm = jax.nn.sigmoid(m_o) * hm.astype(x.dtype) * jax.nn.silu(m_z)

        ha = dsa_attention(a_q.reshape(B, S, A_HEADS, A_DH), a_k.reshape(B, S, A_HEADS, A_DH),
                           a_v.reshape(B, S, A_HEADS, A_DH), ix_q.reshape(B, S, IDX_HEADS, IDX_DH),
                           ix_k, ix_w, rel_bias)
        ha = ha.reshape(B, S, A_WIDTH) * jax.nn.silu(a_z)

        h = h + jnp.concatenate([hm, ha], axis=-1) @ w_out[i]

        h = h + (p[i] @ w_ple[i]) * jax.nn.sigmoid(h @ w_ple_gate[i])
    return rmsnorm(h, norm_final)
```

```python
import functools
import math

import numpy as np
import jax
import jax.numpy as jnp
from jax import lax
from jax.experimental import pallas as pl
from jax.experimental.pallas import tpu as pltpu

F32 = jnp.float32
BF16 = jnp.bfloat16
I32 = jnp.int32

M_HEADS = 4
M_QK = 128
M_V = 256
CONV_W = 4
A_HEADS = 8
A_DH = 64
IDX_HEADS = 8
IDX_DH = 64
TOPK_MAX = 256
REL_BUCKETS = 32
REL_MAX_DIST = 128
EPS = 1e-6

LOG2E = 1.4426950408889634
NEG_BIG = -3.0e38
LOGIT_NEG = -1.0e30

ROW_TILE = 512
M_CHUNK = 256
Q_TILE = 256
K_TILE = 256
BISECT_MAX_ITERS = 256
VMEM_LIMIT = 56 * 1024 * 1024


def _silu(v):
    return v * jax.nn.sigmoid(v)


def _dot(a, b):
    return jnp.dot(a, b, preferred_element_type=F32)


def _dot_nt(a, b):
    return lax.dot_general(a, b, (((1,), (1,)), ((), ())), preferred_element_type=F32)


def _dot_tn(a, b):
    return lax.dot_general(a, b, (((0,), (0,)), ((), ())), preferred_element_type=F32)


def _resident(shape):
    nd = len(shape)
    return pl.BlockSpec(shape, lambda *_: (0,) * nd, pipeline_mode=pl.Buffered(1))


def _inproj_body(x_ref, g_ref, w_mqk, w_mv, w_mo, w_mz, w_sm, w_ik, w_aq, w_ak, w_av, w_az, w_iq,
                 o_mqk, o_mv, o_gate, o_sm, o_ik, o_aq, o_ak, o_av, o_az, o_iq):
    x = x_ref[...]
    ms = jnp.mean(x * x, axis=-1, keepdims=True)
    xn = (x * lax.rsqrt(ms + EPS) * g_ref[...]).astype(BF16)
    o_mqk[...] = _dot(xn, w_mqk[...])
    o_mv[...] = _dot(xn, w_mv[...]).astype(BF16)
    o_gate[...] = jax.nn.sigmoid(_dot(xn, w_mo[...])) * _silu(_dot(xn, w_mz[...]))
    o_sm[...] = _dot(xn, w_sm[...])
    o_ik[...] = _dot(xn, w_ik[...]).astype(BF16)
    o_aq[...] = (_dot(xn, w_aq[...]) * (A_DH ** -0.5 * LOG2E)).astype(BF16)
    o_ak[...] = _dot(xn, w_ak[...]).astype(BF16)
    o_av[...] = _dot(xn, w_av[...]).astype(BF16)
    o_az[...] = _silu(_dot(xn, w_az[...]))
    o_iq[...] = _dot(xn, w_iq[...]).astype(BF16)


def _inproj(x2, g, ws):
    rows, d = x2.shape
    tm = ROW_TILE
    out_dtypes = [F32, BF16, F32, F32, BF16, BF16, BF16, BF16, F32, BF16]
    widths = [w.shape[1] for w in ws]
    out_widths = [widths[0], widths[1], widths[2]] + widths[4:]
    in_specs = [pl.BlockSpec((tm, d), lambda i: (i, 0)), _resident((1, d))]
    in_specs += [_resident(w.shape) for w in ws]
    out_specs = [pl.BlockSpec((tm, n), lambda i: (i, 0)) for n in out_widths]
    out_shape = [jax.ShapeDtypeStruct((rows, n), dt) for n, dt in zip(out_widths, out_dtypes)]
    return pl.pallas_call(
        _inproj_body,
        grid=(rows // tm,),
        in_specs=in_specs,
        out_specs=out_specs,
        out_shape=out_shape,
        compiler_params=pltpu.CompilerParams(
            dimension_semantics=("arbitrary",), vmem_limit_bytes=VMEM_LIMIT),
        name="inproj",
    )(x2, g, *ws)


def _lane_scan(v, op, fill):
    n = v.shape[1]
    lane = lax.broadcasted_iota(I32, v.shape, 1)
    sh = 1
    while sh < n:
        v = op(v, jnp.where(lane >= sh, pltpu.roll(v, sh, 1), fill))
        sh *= 2
    return v


def _mlstm_body(mqk_ref, mv_ref, gate_ref, sm_ref, convw_ref, bpad_ref, mnorm_ref, out_ref,
                ext_ref, c_ref, n_ref, m_ref):
    L = M_CHUNK
    H, DK, DV = M_HEADS, M_QK, M_V
    HALO = 8

    @pl.when(pl.program_id(1) == 0)
    def _():
        ext_ref[0:HALO, :] = jnp.zeros((HALO, 2 * H * DK), F32)
        c_ref[...] = jnp.zeros(c_ref.shape, F32)
        n_ref[...] = jnp.zeros(n_ref.shape, F32)
        m_ref[...] = jnp.zeros(m_ref.shape, F32)

    raw = mqk_ref[...]
    ext_ref[HALO:HALO + L, :] = raw
    base = HALO - (CONV_W - 1)
    y = ext_ref[base:base + L, :] * convw_ref[0:1, :]
    for j in range(1, CONV_W):
        y = y + ext_ref[base + j:base + j + L, :] * convw_ref[j:j + 1, :]
    ext_ref[0:HALO, :] = raw[L - HALO:L, :]
    qk = _silu(y)

    gt = (sm_ref[...] + bpad_ref[...]).T
    li = gt[0:8, :]
    lf = jax.nn.log_sigmoid(gt[8:16, :])
    b = _lane_scan(lf, jnp.add, 0.0)
    a = li - b
    m_prev = m_ref[:, 0:1]
    mrow = jnp.maximum(m_prev, _lane_scan(a, jnp.maximum, -jnp.inf))
    w_inter = jnp.exp(m_prev - mrow)
    e_neg_mt = jnp.exp(-(b + mrow))
    m_last = mrow[:, L - 1:L]
    wk = jnp.exp(a - m_last)
    decay = jnp.exp(m_prev - m_last)
    m_new = b[:, L - 1:L] + m_last
    stack = jnp.concatenate([mrow, w_inter, e_neg_mt, wk, jnp.zeros((128 - 32, L), F32)], axis=0)
    cols = stack.T

    ri = lax.broadcasted_iota(I32, (L, L), 0)
    ci = lax.broadcasted_iota(I32, (L, L), 1)
    causal = ci <= ri

    for h in range(H):
        q = qk[:, h * DK:(h + 1) * DK]
        k = qk[:, (H + h) * DK:(H + h + 1) * DK] * (DK ** -0.5)
        qb = q.astype(BF16)
        v = mv_ref[:, h * DV:(h + 1) * DV]
        m_col = cols[:, h:h + 1]
        wi_col = cols[:, 8 + h:9 + h]
        emt_col = cols[:, 16 + h:17 + h]
        wk_col = cols[:, 24 + h:25 + h]
        e = jnp.where(causal, jnp.exp(a[h:h + 1, :] - m_col), 0.0)
        s = _dot_nt(qb, k.astype(BF16)) * e
        c_old = c_ref[h]
        num = _dot(s.astype(BF16), v) + wi_col * _dot(qb, c_old.astype(BF16))
        den = (jnp.sum(s, axis=1, keepdims=True)
               + wi_col * jnp.sum(q * n_ref[h:h + 1, :], axis=1, keepdims=True))
        den = jnp.maximum(jnp.abs(den), emt_col)
        hh = num / den
        hn = hh * lax.rsqrt(jnp.mean(hh * hh, axis=-1, keepdims=True) + EPS)
        hn = hn * mnorm_ref[:, h * DV:(h + 1) * DV]
        out_ref[:, h * DV:(h + 1) * DV] = (gate_ref[:, h * DV:(h + 1) * DV] * hn).astype(BF16)
        kw = k * wk_col
        dec = decay[h:h + 1, :]
        c_ref[h] = dec * c_old + _dot_tn(kw.astype(BF16), v)
        n_ref[h:h + 1, :] = dec * n_ref[h:h + 1, :] + jnp.sum(kw, axis=0, keepdims=True)
    m_ref[...] = jnp.broadcast_to(m_new, m_ref.shape)


def _mlstm(mqk, mv, gate, sm, conv_w, bpad, mnorm, batch, seq):
    L = M_CHUNK
    nc = seq // L
    wqk = mqk.shape[1]
    wv = mv.shape[1]
    row = lambda b, c: (b * nc + c, 0)
    return pl.pallas_call(
        _mlstm_body,
        grid=(batch, nc),
        in_specs=[
            pl.BlockSpec((L, wqk), row),
            pl.BlockSpec((L, wv), row),
            pl.BlockSpec((L, wv), row),
            pl.BlockSpec((L, 128), row),
            _resident(conv_w.shape),
            _resident(bpad.shape),
            _resident(mnorm.shape),
        ],
        out_specs=pl.BlockSpec((L, wv), row),
        out_shape=jax.ShapeDtypeStruct((batch * seq, wv), BF16),
        scratch_shapes=[
            pltpu.VMEM((L + 8, wqk), F32),
            pltpu.VMEM((M_HEADS, M_QK, M_V), F32),
            pltpu.VMEM((8, M_QK), F32),
            pltpu.VMEM((8, 128), F32),
        ],
        compiler_params=pltpu.CompilerParams(
            dimension_semantics=("arbitrary", "arbitrary"), vmem_limit_bytes=VMEM_LIMIT),
        name="mlstm",
    )(mqk, mv, gate, sm, conv_w, bpad, mnorm)


def _t5_bucket_table(max_dist):
    d = np.arange(max_dist)
    max_exact = REL_BUCKETS // 2
    tabs = []
    for dt in (np.float32, np.float64):
        ratio = np.maximum(d, 1).astype(dt) / dt(max_exact)
        large = max_exact + (np.log(ratio) / dt(math.log(REL_MAX_DIST / max_exact))
                             * dt(REL_BUCKETS - max_exact)).astype(np.int32)
        large = np.minimum(large, REL_BUCKETS - 1)
        tabs.append(np.where(d < max_exact, d, large).astype(np.int32))
    assert np.array_equal(tabs[0], tabs[1])
    return tabs[0]


def _bucket_maps():
    tab = _t5_bucket_table(Q_TILE + K_TILE)
    s = np.arange(K_TILE)[:, None]
    t = np.arange(Q_TILE)[None, :]
    diag = tab[np.maximum(t - s, 0)]
    prev = tab[K_TILE + t - s]
    assert np.all(tab[K_TILE:] == REL_BUCKETS - 1)
    return np.stack([prev, diag]).astype(np.int32)


def _dsa_body(rb_ref, iq_ref, sm_ref, aq_ref, az_ref, ik_ref, k_ref, vt_ref, map_ref, out_ref,
              sc_ref, iqm_ref, md_ref, qm_ref, bias_ref, acc_ref, m_ref, l_ref):
    TQ, TK, H, DH = Q_TILE, K_TILE, A_HEADS, A_DH
    j = pl.program_id(1)
    n_tiles = j + 1

    @pl.when((pl.program_id(0) == 0) & (j == 0))
    def _():
        bias_ref[...] = jnp.zeros(bias_ref.shape, F32)

        def fill(bkt, carry):
            for which in range(2):
                hit = map_ref[which] == bkt
                for h in range(H):
                    val = (rb_ref[bkt, h] - rb_ref[REL_BUCKETS - 1, h]) * LOG2E
                    bias_ref[which, h] = jnp.where(hit, val, bias_ref[which, h])
            return carry

        lax.fori_loop(0, REL_BUCKETS - 1, fill, 0)

    lane = lax.broadcasted_iota(I32, (TQ, 2 * DH), 1)
    eye = (lax.broadcasted_iota(I32, (TQ, TQ), 0) == lax.broadcasted_iota(I32, (TQ, TQ), 1))
    w_idx = sm_ref[:, 16:16 + IDX_HEADS] * (IDX_HEADS ** -0.5 * IDX_DH ** -0.5)
    for h in range(H):
        p0 = (h // 2) * 2 * DH
        keep = (lane < DH) if h % 2 == 0 else (lane >= DH)
        iqm_ref[h] = jnp.where(keep, iq_ref[:, p0:p0 + 2 * DH], jnp.zeros((), BF16))
        qm_ref[h] = jnp.where(keep, aq_ref[:, p0:p0 + 2 * DH], jnp.zeros((), BF16))
        md_ref[h] = jnp.where(eye, w_idx[:, h:h + 1], 0.0).astype(BF16)

    s_loc = lax.broadcasted_iota(I32, (TK, TQ), 0)
    t_loc = lax.broadcasted_iota(I32, (TK, TQ), 1)

    def index_tile(kb, diag):
        r0 = pl.multiple_of(kb * TK, TK)
        lhs = ik_ref[pl.ds(r0, TK), :]
        sc = jnp.zeros((TK, TQ), F32)
        for h in range(H):
            xh = _dot_nt(lhs, iqm_ref[h])
            sc = sc + _dot(jnp.maximum(xh, 0.0).astype(BF16), md_ref[h])
        if diag:
            sc = jnp.where(s_loc <= t_loc, sc, NEG_BIG)
        sc_ref[pl.ds(r0, TK), :] = sc

    def index_loop(kb, carry):
        index_tile(kb, False)
        return carry

    lax.fori_loop(0, j, index_loop, 0)
    index_tile(j, True)

    def count_gt(thr):
        def body(kb, cnt):
            r0 = pl.multiple_of(kb * TK, TK)
            hit = jnp.where(sc_ref[pl.ds(r0, TK), :] > thr, 1, 0).astype(I32)
            return cnt + jnp.sum(hit.reshape(TK // 8, 8, TQ), axis=0)
        cnt8 = lax.fori_loop(0, n_tiles, body, jnp.zeros((8, TQ), I32))
        return jnp.sum(cnt8, axis=0, keepdims=True)

    def stats():
        def body(kb, carry):
            c_pos_, c_nn_, mx_, mn_ = carry
            r0 = pl.multiple_of(kb * TK, TK)
            sc = sc_ref[pl.ds(r0, TK), :]
            fold = lambda v: v.reshape(TK // 8, 8, TQ)
            c_pos_ = c_pos_ + jnp.sum(fold(jnp.where(sc > 0.0, 1, 0).astype(I32)), axis=0)
            c_nn_ = c_nn_ + jnp.sum(fold(jnp.where(sc >= 0.0, 1, 0).astype(I32)), axis=0)
            mx_ = jnp.maximum(mx_, jnp.max(fold(sc), axis=0))
            mn_ = jnp.minimum(mn_, jnp.min(fold(jnp.where(sc > NEG_BIG, sc, -NEG_BIG)), axis=0))
            return c_pos_, c_nn_, mx_, mn_
        init = (jnp.zeros((8, TQ), I32), jnp.zeros((8, TQ), I32),
                jnp.full((8, TQ), NEG_BIG, F32), jnp.full((8, TQ), -NEG_BIG, F32))
        c_pos_, c_nn_, mx_, mn_ = lax.fori_loop(0, n_tiles, body, init)
        return (jnp.sum(c_pos_, axis=0, keepdims=True), jnp.sum(c_nn_, axis=0, keepdims=True),
                jnp.max(mx_, axis=0, keepdims=True), jnp.min(mn_, axis=0, keepdims=True))

    t_glob = j * TQ + lax.broadcasted_iota(I32, (1, TQ), 1)
    k_eff = jnp.minimum(TOPK_MAX, t_glob + 1)

    zero = jnp.zeros((1, TQ), F32)
    c_pos, c_nn, hi0, mn = stats()
    all_in = (t_glob + 1) <= TOPK_MAX
    pos_ok = c_pos >= k_eff
    tie0 = (~all_in) & (~pos_ok) & (c_nn >= k_eff)
    lo_neg = jnp.maximum(mn + mn - 1.0, NEG_BIG)
    lo = jnp.where(all_in, NEG_BIG, jnp.where(pos_ok, zero, lo_neg))
    hi = jnp.where((~all_in) & (~pos_ok), zero, hi0)
    fin0 = all_in | (pos_ok & (c_pos == k_eff)) | tie0

    def bis_cond(st):
        it, _, _, fin_, _ = st
        return (it < BISECT_MAX_ITERS) & (jnp.min(fin_) == 0)

    def bis_body(st):
        it, lo_, hi_, fin_, tie_ = st
        mid = 0.5 * lo_ + 0.5 * hi_
        collapsed = (mid <= lo_) | (mid >= hi_)
        cm = count_gt(mid)
        act = fin_ == 0
        up = act & (~collapsed) & (cm >= k_eff)
        dn = act & (~collapsed) & (cm < k_eff)
        lo_n = jnp.where(up, mid, lo_)
        hi_n = jnp.where(dn, mid, hi_)
        tie_n = jnp.where(act & collapsed, 1, tie_)
        fin_n = jnp.where((act & collapsed) | (up & (cm == k_eff)), 1, fin_)
        return it + 1, lo_n, hi_n, fin_n, tie_n

    _, lo, hi, _, tie_i = lax.while_loop(
        bis_cond, bis_body,
        (jnp.int32(0), lo, hi, jnp.where(fin0, 1, 0).astype(I32), jnp.where(tie0, 1, 0).astype(I32)))
    tie = tie_i > 0
    theta_ref = m_ref
    theta_ref[0:1, :] = lo

    @pl.when(jnp.max(tie_i) > 0)
    def _():
        v = hi

        def pass_a(kb, carry):
            n_gt, below = carry
            r0 = pl.multiple_of(kb * TK, TK)
            sc = sc_ref[pl.ds(r0, TK), :]
            n_gt = n_gt + jnp.sum(jnp.where(sc > v, 1.0, 0.0), axis=0, keepdims=True)
            below = jnp.maximum(below, jnp.max(jnp.where(sc < v, sc, NEG_BIG), axis=0, keepdims=True))
            return n_gt, below

        n_gt, below = lax.fori_loop(
            0, n_tiles, pass_a, (jnp.zeros((1, TQ), F32), jnp.full((1, TQ), NEG_BIG, F32)))
        need = k_eff.astype(F32) - n_gt
        tri = jnp.where(lax.broadcasted_iota(I32, (TK, TK), 1) < lax.broadcasted_iota(I32, (TK, TK), 0),
                        1.0, 0.0).astype(BF16)

        def pass_b(kb, seen):
            r0 = pl.multiple_of(kb * TK, TK)
            sc = sc_ref[pl.ds(r0, TK), :]
            eq = (sc == v) & tie
            eqf = jnp.where(eq, 1.0, 0.0)
            rank = seen + _dot(tri, eqf.astype(BF16))
            sc_ref[pl.ds(r0, TK), :] = jnp.where(eq & (rank >= need), NEG_BIG, sc)
            return seen + jnp.sum(eqf, axis=0, keepdims=True)

        lax.fori_loop(0, n_tiles, pass_b, jnp.zeros((1, TQ), F32))
        theta_ref[0:1, :] = jnp.where(tie, below, lo)

    theta = theta_ref[0:1, :]

    m_ref[...] = jnp.full(m_ref.shape, LOGIT_NEG, F32)
    l_ref[...] = jnp.zeros(l_ref.shape, F32)
    acc_ref[...] = jnp.zeros(acc_ref.shape, F32)

    def attend_tile(kb, near):
        r0 = pl.multiple_of(kb * TK, TK)
        mb = jnp.where(sc_ref[pl.ds(r0, TK), :] > theta, 0.0, LOGIT_NEG)
        for h in range(H):
            p0 = (h // 2) * 2 * DH
            lg = _dot_nt(k_ref[pl.ds(r0, TK), p0:p0 + 2 * DH], qm_ref[h]) + mb
            if near is not None:
                lg = lg + bias_ref[near, h]
            m_old = m_ref[h:h + 1, :]
            m_new = jnp.maximum(m_old, jnp.max(lg, axis=0, keepdims=True))
            alpha = jnp.exp2(m_old - m_new)
            p = jnp.exp2(lg - m_new)
            l_ref[h:h + 1, :] = alpha * l_ref[h:h + 1, :] + jnp.sum(p, axis=0, keepdims=True)
            m_ref[h:h + 1, :] = m_new
            pv = _dot(vt_ref[h * DH:(h + 1) * DH, pl.ds(r0, TK)], p.astype(BF16))
            acc_ref[h * DH:(h + 1) * DH, :] = alpha * acc_ref[h * DH:(h + 1) * DH, :] + pv

    def far_loop(kb, carry):
        attend_tile(kb, None)
        return carry

    lax.fori_loop(0, j - 1, far_loop, 0)

    @pl.when(j >= 1)
    def _():
        attend_tile(j - 1, 0)

    attend_tile(j, 1)

    for h in range(H):
        acc_ref[h * DH:(h + 1) * DH, :] = acc_ref[h * DH:(h + 1) * DH, :] / l_ref[h:h + 1, :]
    out_ref[...] = (acc_ref[...].T * az_ref[...]).astype(BF16)


def _dsa(rel_bias, iq, sm, aq, az, ik2, ak, avt, batch, seq):
    TQ = Q_TILE
    nq = seq // TQ
    width = A_HEADS * A_DH
    maps = jnp.asarray(_bucket_maps())
    qrow = lambda b, j: (b * nq + j, 0)
    return pl.pallas_call(
        _dsa_body,
        grid=(batch, nq),
        in_specs=[
            pl.BlockSpec(memory_space=pltpu.SMEM),
            pl.BlockSpec((TQ, width), qrow),
            pl.BlockSpec((TQ, 128), qrow),
            pl.BlockSpec((TQ, width), qrow),
            pl.BlockSpec((TQ, width), qrow),
            pl.BlockSpec((seq, 128), lambda b, j: (b, 0), pipeline_mode=pl.Buffered(1)),
            pl.BlockSpec((seq, width), lambda b, j: (b, 0), pipeline_mode=pl.Buffered(1)),
            pl.BlockSpec((None, width, seq), lambda b, j: (b, 0, 0), pipeline_mode=pl.Buffered(1)),
            _resident(maps.shape),
        ],
        out_specs=pl.BlockSpec((TQ, width), qrow),
        out_shape=jax.ShapeDtypeStruct((batch * seq, width), BF16),
        scratch_shapes=[
            pltpu.VMEM((seq, TQ), F32),
            pltpu.VMEM((A_HEADS, TQ, 2 * A_DH), BF16),
            pltpu.VMEM((A_HEADS, TQ, TQ), BF16),
            pltpu.VMEM((A_HEADS, TQ, 2 * A_DH), BF16),
            pltpu.VMEM((2, A_HEADS, K_TILE, TQ), F32),
            pltpu.VMEM((A_HEADS * A_DH, TQ), F32),
            pltpu.VMEM((8, TQ), F32),
            pltpu.VMEM((8, TQ), F32),
        ],
        compiler_params=pltpu.CompilerParams(
            dimension_semantics=("arbitrary", "arbitrary"), vmem_limit_bytes=VMEM_LIMIT),
        name="dsa",
    )(rel_bias, iq, sm, aq, az, ik2, ak, avt, maps)


def _outproj_body(x_ref, hm_ref, ha_ref, p_ref, wom_ref, woa_ref, wple_ref, wg_ref, g_ref, o_ref):
    h1 = x_ref[...] + _dot(hm_ref[...], wom_ref[...]) + _dot(ha_ref[...], woa_ref[...])
    gate = jax.nn.sigmoid(_dot(h1.astype(BF16), wg_ref[...]))
    h2 = h1 + _dot(p_ref[...].astype(BF16), wple_ref[...]) * gate
    ms = jnp.mean(h2 * h2, axis=-1, keepdims=True)
    o_ref[...] = h2 * lax.rsqrt(ms + EPS) * g_ref[...]


def _outproj(x2, hm, ha, p2, wom, woa, wple, wg, g):
    rows, d = x2.shape
    tm = ROW_TILE
    tile = lambda n: pl.BlockSpec((tm, n), lambda i: (i, 0))
    return pl.pallas_call(
        _outproj_body,
        grid=(rows // tm,),
        in_specs=[tile(d), tile(hm.shape[1]), tile(ha.shape[1]), tile(p2.shape[1]),
                  _resident(wom.shape), _resident(woa.shape), _resident(wple.shape),
                  _resident(wg.shape), _resident(g.shape)],
        out_specs=tile(d),
        out_shape=jax.ShapeDtypeStruct((rows, d), F32),
        compiler_params=pltpu.CompilerParams(
            dimension_semantics=("arbitrary",), vmem_limit_bytes=VMEM_LIMIT),
        name="outproj",
    )(x2, hm, ha, p2, wom, woa, wple, wg, g)


def kernel(x, p, w_in, b_gate, conv_w, w_out, norm_in, m_norm, rel_bias, w_ple, w_ple_gate, norm_final):
    batch, seq, d = x.shape
    assert w_in.shape[0] == 1 and p.shape[0] == 1, "single-layer block"
    assert seq % Q_TILE == 0 and seq % M_CHUNK == 0 and (batch * seq) % ROW_TILE == 0
    rows = batch * seq
    x2 = x.reshape(rows, d)

    sizes = [M_HEADS * M_QK, M_HEADS * M_QK, M_HEADS * M_V, M_HEADS * M_V, M_HEADS * M_V, 2 * M_HEADS,
             A_HEADS * A_DH, A_HEADS * A_DH, A_HEADS * A_DH, A_HEADS * A_DH,
             IDX_HEADS * IDX_DH, IDX_DH, IDX_HEADS]
    offs = np.concatenate([[0], np.cumsum(sizes)])
    wi = w_in[0]
    col = lambda i: wi[:, int(offs[i]):int(offs[i + 1])]
    w_if, w_ixk, w_ixw = col(5), col(11), col(12)
    zeros = lambda n: jnp.zeros((d, n), wi.dtype)
    w_sm = jnp.concatenate([w_if[:, :M_HEADS], zeros(4), w_if[:, M_HEADS:], zeros(4), w_ixw, zeros(104)], axis=1)
    w_ik2 = jnp.concatenate([w_ixk, w_ixk], axis=1)
    ws = [jnp.concatenate([col(0), col(1)], axis=1), col(2), col(3), col(4), w_sm, w_ik2,
          col(6), col(7), col(8), col(9), col(10)]
    ws = [w.astype(BF16) for w in ws]
    bg = b_gate[0]
    bpad = jnp.concatenate([bg[:M_HEADS], jnp.zeros((4,), F32), bg[M_HEADS:], jnp.zeros((116,), F32)])[None, :]

    mqk, mv, gate, sm, ik2, aq, ak, av, az, iq = _inproj(x2, norm_in[0][None, :], ws)

    hm = _mlstm(mqk, mv, gate, sm, conv_w[0], bpad, m_norm[0][None, :], batch, seq)

    avt = av.reshape(batch, seq, -1).transpose(0, 2, 1)
    ha = _dsa(rel_bias, iq, sm, aq, az, ik2, ak, avt, batch, seq)

    wo = w_out[0].astype(BF16)
    out = _outproj(x2, hm, ha, p[0].reshape(rows, -1), wo[:M_HEADS * M_V], wo[M_HEADS * M_V:],
                   w_ple[0].astype(BF16), w_ple_gate[0].astype(BF16), norm_final[None, :])
    return out.reshape(batch, seq, d)
```

```python
import functools
import math

import numpy as np
import jax
import jax.numpy as jnp
from jax import lax
from jax.experimental import pallas as pl
from jax.experimental.pallas import tpu as pltpu

F32 = jnp.float32
BF16 = jnp.bfloat16
I32 = jnp.int32

M_HEADS = 4
M_QK = 128
M_V = 256
CONV_W = 4
A_HEADS = 8
A_DH = 64
IDX_HEADS = 8
IDX_DH = 64
TOPK_MAX = 256
REL_BUCKETS = 32
REL_MAX_DIST = 128
EPS = 1e-6

LOG2E = 1.4426950408889634
NEG_BIG = -3.0e38
LOGIT_NEG = -1.0e30

ROW_TILE = 512
M_CHUNK = 256
Q_TILE = 256
K_TILE = 256
BISECT_MAX_ITERS = 256
VMEM_LIMIT = 56 * 1024 * 1024


def _silu(v):
    return v * jax.nn.sigmoid(v)


def _dot(a, b):
    return jnp.dot(a, b, preferred_element_type=F32)


def _dot_nt(a, b):
    return lax.dot_general(a, b, (((1,), (1,)), ((), ())), preferred_element_type=F32)


def _dot_tn(a, b):
    return lax.dot_general(a, b, (((0,), (0,)), ((), ())), preferred_element_type=F32)


def _resident(shape):
    nd = len(shape)
    return pl.BlockSpec(shape, lambda *_: (0,) * nd, pipeline_mode=pl.Buffered(1))


def _inproj_body(x_ref, g_ref, w_mqk, w_mv, w_mo, w_mz, w_sm, w_ik, w_aq, w_ak, w_av, w_az, w_iq,
                 o_mqk, o_mv, o_gate, o_sm, o_ik, o_aq, o_ak, o_av, o_az, o_iq):
    x = x_ref[...]
    ms = jnp.mean(x * x, axis=-1, keepdims=True)
    xn = (x * lax.rsqrt(ms + EPS) * g_ref[...]).astype(BF16)
    o_mqk[...] = _dot(xn, w_mqk[...])
    o_mv[...] = _dot(xn, w_mv[...]).astype(BF16)
    o_gate[...] = jax.nn.sigmoid(_dot(xn, w_mo[...])) * _silu(_dot(xn, w_mz[...]))
    o_sm[...] = _dot(xn, w_sm[...])
    o_ik[...] = _dot(xn, w_ik[...]).astype(BF16)
    o_aq[...] = (_dot(xn, w_aq[...]) * (A_DH ** -0.5 * LOG2E)).astype(BF16)
    o_ak[...] = _dot(xn, w_ak[...]).astype(BF16)
    o_av[...] = _dot(xn, w_av[...]).astype(BF16)
    o_az[...] = _silu(_dot(xn, w_az[...]))
    o_iq[...] = _dot(xn, w_iq[...]).astype(BF16)


def _inproj(x2, g, ws):
    rows, d = x2.shape
    tm = ROW_TILE
    out_dtypes = [F32, BF16, F32, F32, BF16, BF16, BF16, BF16, F32, BF16]
    widths = [w.shape[1] for w in ws]
    out_widths = [widths[0], widths[1], widths[2]] + widths[4:]
    in_specs = [pl.BlockSpec((tm, d), lambda i: (i, 0)), _resident((1, d))]
    in_specs += [_resident(w.shape) for w in ws]
    out_specs = [pl.BlockSpec((tm, n), lambda i: (i, 0)) for n in out_widths]
    out_shape = [jax.ShapeDtypeStruct((rows, n), dt) for n, dt in zip(out_widths, out_dtypes)]
    return pl.pallas_call(
        _inproj_body,
        grid=(rows // tm,),
        in_specs=in_specs,
        out_specs=out_specs,
        out_shape=out_shape,
        compiler_params=pltpu.CompilerParams(
            dimension_semantics=("arbitrary",), vmem_limit_bytes=VMEM_LIMIT),
        name="inproj",
    )(x2, g, *ws)


def _lane_scan(v, op, fill):
    n = v.shape[1]
    lane = lax.broadcasted_iota(I32, v.shape, 1)
    sh = 1
    while sh < n:
        v = op(v, jnp.where(lane >= sh, pltpu.roll(v, sh, 1), fill))
        sh *= 2
    return v


def _mlstm_body(mqk_ref, mv_ref, gate_ref, sm_ref, convw_ref, bpad_ref, mnorm_ref, out_ref,
                ext_ref, c_ref, n_ref, m_ref):
    L = M_CHUNK
    H, DK, DV = M_HEADS, M_QK, M_V
    HALO = 8

    @pl.when(pl.program_id(1) == 0)
    def _():
        ext_ref[0:HALO, :] = jnp.zeros((HALO, 2 * H * DK), F32)
        c_ref[...] = jnp.zeros(c_ref.shape, F32)
        n_ref[...] = jnp.zeros(n_ref.shape, F32)
        m_ref[...] = jnp.zeros(m_ref.shape, F32)

    raw = mqk_ref[...]
    ext_ref[HALO:HALO + L, :] = raw
    base = HALO - (CONV_W - 1)
    y = ext_ref[base:base + L, :] * convw_ref[0:1, :]
    for j in range(1, CONV_W):
        y = y + ext_ref[base + j:base + j + L, :] * convw_ref[j:j + 1, :]
    ext_ref[0:HALO, :] = raw[L - HALO:L, :]
    qk = _silu(y)

    gt = (sm_ref[...] + bpad_ref[...]).T
    li = gt[0:8, :]
    lf = jax.nn.log_sigmoid(gt[8:16, :])
    b = _lane_scan(lf, jnp.add, 0.0)
    a = li - b
    m_prev = m_ref[:, 0:1]
    mrow = jnp.maximum(m_prev, _lane_scan(a, jnp.maximum, -jnp.inf))
    w_inter = jnp.exp(m_prev - mrow)
    e_neg_mt = jnp.exp(-(b + mrow))
    m_last = mrow[:, L - 1:L]
    wk = jnp.exp(a - m_last)
    decay = jnp.exp(m_prev - m_last)
    m_new = b[:, L - 1:L] + m_last
    stack = jnp.concatenate([mrow, w_inter, e_neg_mt, wk, jnp.zeros((128 - 32, L), F32)], axis=0)
    cols = stack.T

    ri = lax.broadcasted_iota(I32, (L, L), 0)
    ci = lax.broadcasted_iota(I32, (L, L), 1)
    causal = ci <= ri

    for h in range(H):
        q = qk[:, h * DK:(h + 1) * DK]
        k = qk[:, (H + h) * DK:(H + h + 1) * DK] * (DK ** -0.5)
        qb = q.astype(BF16)
        v = mv_ref[:, h * DV:(h + 1) * DV]
        m_col = cols[:, h:h + 1]
        wi_col = cols[:, 8 + h:9 + h]
        emt_col = cols[:, 16 + h:17 + h]
        wk_col = cols[:, 24 + h:25 + h]
        e = jnp.where(causal, jnp.exp(a[h:h + 1, :] - m_col), 0.0)
        s = _dot_nt(qb, k.astype(BF16)) * e
        c_old = c_ref[h]
        num = _dot(s.astype(BF16), v) + wi_col * _dot(qb, c_old.astype(BF16))
        den = (jnp.sum(s, axis=1, keepdims=True)
               + wi_col * jnp.sum(q * n_ref[h:h + 1, :], axis=1, keepdims=True))
        den = jnp.maximum(jnp.abs(den), emt_col)
        hh = num / den
        hn = hh * lax.rsqrt(jnp.mean(hh * hh, axis=-1, keepdims=True) + EPS)
        hn = hn * mnorm_ref[:, h * DV:(h + 1) * DV]
        out_ref[:, h * DV:(h + 1) * DV] = (gate_ref[:, h * DV:(h + 1) * DV] * hn).astype(BF16)
        kw = k * wk_col
        dec = decay[h:h + 1, :]
        c_ref[h] = dec * c_old + _dot_tn(kw.astype(BF16), v)
        n_ref[h:h + 1, :] = dec * n_ref[h:h + 1, :] + jnp.sum(kw, axis=0, keepdims=True)
    m_ref[...] = jnp.broadcast_to(m_new, m_ref.shape)


def _mlstm(mqk, mv, gate, sm, conv_w, bpad, mnorm, batch, seq):
    L = M_CHUNK
    nc = seq // L
    wqk = mqk.shape[1]
    wv = mv.shape[1]
    row = lambda b, c: (b * nc + c, 0)
    return pl.pallas_call(
        _mlstm_body,
        grid=(batch, nc),
        in_specs=[
            pl.BlockSpec((L, wqk), row),
            pl.BlockSpec((L, wv), row),
            pl.BlockSpec((L, wv), row),
            pl.BlockSpec((L, 128), row),
            _resident(conv_w.shape),
            _resident(bpad.shape),
            _resident(mnorm.shape),
        ],
        out_specs=pl.BlockSpec((L, wv), row),
        out_shape=jax.ShapeDtypeStruct((batch * seq, wv), BF16),
        scratch_shapes=[
            pltpu.VMEM((L + 8, wqk), F32),
            pltpu.VMEM((M_HEADS, M_QK, M_V), F32),
            pltpu.VMEM((8, M_QK), F32),
            pltpu.VMEM((8, 128), F32),
        ],
        compiler_params=pltpu.CompilerParams(
            dimension_semantics=("arbitrary", "arbitrary"), vmem_limit_bytes=VMEM_LIMIT),
        name="mlstm",
    )(mqk, mv, gate, sm, conv_w, bpad, mnorm)


def _t5_bucket_table(max_dist):
    d = np.arange(max_dist)
    max_exact = REL_BUCKETS // 2
    tabs = []
    for dt in (np.float32, np.float64):
        ratio = np.maximum(d, 1).astype(dt) / dt(max_exact)
        large = max_exact + (np.log(ratio) / dt(math.log(REL_MAX_DIST / max_exact))
                             * dt(REL_BUCKETS - max_exact)).astype(np.int32)
        large = np.minimum(large, REL_BUCKETS - 1)
        tabs.append(np.where(d < max_exact, d, large).astype(np.int32))
    assert np.array_equal(tabs[0], tabs[1])
    return tabs[0]


def _bucket_maps():
    tab = _t5_bucket_table(Q_TILE + K_TILE)
    s = np.arange(K_TILE)[:, None]
    t = np.arange(Q_TILE)[None, :]
    diag = tab[np.maximum(t - s, 0)]
    prev = tab[K_TILE + t - s]
    assert np.all(tab[K_TILE:] == REL_BUCKETS - 1)
    return np.stack([prev, diag]).astype(np.int32)


def _dsa_body(rb_ref, iq_ref, sm_ref, aq_ref, az_ref, ik_ref, k_ref, vt_ref, map_ref, out_ref,
              sc_ref, iqm_ref, md_ref, qm_ref, bias_ref, acc_ref, m_ref, l_ref, lg_ref, p_ref):
    TQ, TK, H, DH = Q_TILE, K_TILE, A_HEADS, A_DH
    j = pl.program_id(1)
    n_tiles = j + 1

    @pl.when((pl.program_id(0) == 0) & (j == 0))
    def _():
        bias_ref[...] = jnp.zeros(bias_ref.shape, F32)

        def fill(bkt, carry):
            for which in range(2):
                hit = map_ref[which] == bkt
                for h in range(H):
                    val = (rb_ref[bkt, h] - rb_ref[REL_BUCKETS - 1, h]) * LOG2E
                    bias_ref[which, h] = jnp.where(hit, val, bias_ref[which, h])
            return carry

        lax.fori_loop(0, REL_BUCKETS - 1, fill, 0)

    lane = lax.broadcasted_iota(I32, (TQ, 2 * DH), 1)
    eye = (lax.broadcasted_iota(I32, (TQ, TQ), 0) == lax.broadcasted_iota(I32, (TQ, TQ), 1))
    w_idx = sm_ref[:, 16:16 + IDX_HEADS] * (IDX_HEADS ** -0.5 * IDX_DH ** -0.5)
    for h in range(H):
        p0 = (h // 2) * 2 * DH
        keep = (lane < DH) if h % 2 == 0 else (lane >= DH)
        iqm_ref[h] = jnp.where(keep, iq_ref[:, p0:p0 + 2 * DH], jnp.zeros((), BF16))
        qm_ref[h] = jnp.where(keep, aq_ref[:, p0:p0 + 2 * DH], jnp.zeros((), BF16))
        md_ref[h] = jnp.where(eye, w_idx[:, h:h + 1], 0.0).astype(BF16)

    s_loc = lax.broadcasted_iota(I32, (TK, TQ), 0)
    t_loc = lax.broadcasted_iota(I32, (TK, TQ), 1)

    def index_tile(kb, diag):
        r0 = pl.multiple_of(kb * TK, TK)
        lhs = ik_ref[pl.ds(r0, TK), :]
        sc = jnp.zeros((TK, TQ), F32)
        for h in range(H):
            xh = _dot_nt(lhs, iqm_ref[h])
            sc = sc + _dot(jnp.maximum(xh, 0.0).astype(BF16), md_ref[h])
        if diag:
            sc = jnp.where(s_loc <= t_loc, sc, NEG_BIG)
        sc_ref[pl.ds(r0, TK), :] = sc

    def index_loop(kb, carry):
        index_tile(kb, False)
        return carry

    lax.fori_loop(0, j, index_loop, 0)
    index_tile(j, True)

    def count_gt(thr):
        def body(kb, cnt):
            r0 = pl.multiple_of(kb * TK, TK)
            hit = jnp.where(sc_ref[pl.ds(r0, TK), :] > thr, 1, 0).astype(I32)
            return cnt + jnp.sum(hit.reshape(TK // 8, 8, TQ), axis=0)
        cnt8 = lax.fori_loop(0, n_tiles, body, jnp.zeros((8, TQ), I32))
        return jnp.sum(cnt8, axis=0, keepdims=True)

    def stats():
        def body(kb, carry):
            c_pos_, c_nn_, mx_, mn_ = carry
            r0 = pl.multiple_of(kb * TK, TK)
            sc = sc_ref[pl.ds(r0, TK), :]
            fold = lambda v: v.reshape(TK // 8, 8, TQ)
            c_pos_ = c_pos_ + jnp.sum(fold(jnp.where(sc > 0.0, 1, 0).astype(I32)), axis=0)
            c_nn_ = c_nn_ + jnp.sum(fold(jnp.where(sc >= 0.0, 1, 0).astype(I32)), axis=0)
            mx_ = jnp.maximum(mx_, jnp.max(fold(sc), axis=0))
            mn_ = jnp.minimum(mn_, jnp.min(fold(jnp.where(sc > NEG_BIG, sc, -NEG_BIG)), axis=0))
            return c_pos_, c_nn_, mx_, mn_
        init = (jnp.zeros((8, TQ), I32), jnp.zeros((8, TQ), I32),
                jnp.full((8, TQ), NEG_BIG, F32), jnp.full((8, TQ), -NEG_BIG, F32))
        c_pos_, c_nn_, mx_, mn_ = lax.fori_loop(0, n_tiles, body, init)
        return (jnp.sum(c_pos_, axis=0, keepdims=True), jnp.sum(c_nn_, axis=0, keepdims=True),
                jnp.max(mx_, axis=0, keepdims=True), jnp.min(mn_, axis=0, keepdims=True))

    t_glob = j * TQ + lax.broadcasted_iota(I32, (1, TQ), 1)
    k_eff = jnp.minimum(TOPK_MAX, t_glob + 1)

    zero = jnp.zeros((1, TQ), F32)
    c_pos, c_nn, hi0, mn = stats()
    all_in = (t_glob + 1) <= TOPK_MAX
    pos_ok = c_pos >= k_eff
    tie0 = (~all_in) & (~pos_ok) & (c_nn >= k_eff)
    lo_neg = jnp.maximum(mn + mn - 1.0, NEG_BIG)
    lo = jnp.where(all_in, NEG_BIG, jnp.where(pos_ok, zero, lo_neg))
    hi = jnp.where((~all_in) & (~pos_ok), zero, hi0)
    fin0 = all_in | (pos_ok & (c_pos == k_eff)) | tie0

    def bis_cond(st):
        it, _, _, fin_, _ = st
        return (it < BISECT_MAX_ITERS) & (jnp.min(fin_) == 0)

    def bis_body(st):
        it, lo_, hi_, fin_, tie_ = st
        mid = 0.5 * lo_ + 0.5 * hi_
        collapsed = (mid <= lo_) | (mid >= hi_)
        cm = count_gt(mid)
        act = fin_ == 0
        up = act & (~collapsed) & (cm >= k_eff)
        dn = act & (~collapsed) & (cm < k_eff)
        lo_n = jnp.where(up, mid, lo_)
        hi_n = jnp.where(dn, mid, hi_)
        tie_n = jnp.where(act & collapsed, 1, tie_)
        fin_n = jnp.where((act & collapsed) | (up & (cm == k_eff)), 1, fin_)
        return it + 1, lo_n, hi_n, fin_n, tie_n

    _, lo, hi, _, tie_i = lax.while_loop(
        bis_cond, bis_body,
        (jnp.int32(0), lo, hi, jnp.where(fin0, 1, 0).astype(I32), jnp.where(tie0, 1, 0).astype(I32)))
    tie = tie_i > 0
    theta_ref = m_ref
    theta_ref[0:1, :] = lo

    @pl.when(jnp.max(tie_i) > 0)
    def _():
        v = hi

        def pass_a(kb, carry):
            n_gt, below = carry
            r0 = pl.multiple_of(kb * TK, TK)
            sc = sc_ref[pl.ds(r0, TK), :]
            n_gt = n_gt + jnp.sum(jnp.where(sc > v, 1.0, 0.0), axis=0, keepdims=True)
            below = jnp.maximum(below, jnp.max(jnp.where(sc < v, sc, NEG_BIG), axis=0, keepdims=True))
            return n_gt, below

        n_gt, below = lax.fori_loop(
            0, n_tiles, pass_a, (jnp.zeros((1, TQ), F32), jnp.full((1, TQ), NEG_BIG, F32)))
        need = k_eff.astype(F32) - n_gt
        tri = jnp.where(lax.broadcasted_iota(I32, (TK, TK), 1) < lax.broadcasted_iota(I32, (TK, TK), 0),
                        1.0, 0.0).astype(BF16)

        def pass_b(kb, seen):
            r0 = pl.multiple_of(kb * TK, TK)
            sc = sc_ref[pl.ds(r0, TK), :]
            eq = (sc == v) & tie
            eqf = jnp.where(eq, 1.0, 0.0)
            rank = seen + _dot(tri, eqf.astype(BF16))
            sc_ref[pl.ds(r0, TK), :] = jnp.where(eq & (rank >= need), NEG_BIG, sc)
            return seen + jnp.sum(eqf, axis=0, keepdims=True)

        lax.fori_loop(0, n_tiles, pass_b, jnp.zeros((1, TQ), F32))
        theta_ref[0:1, :] = jnp.where(tie, below, lo)

    theta = theta_ref[0:1, :]

    m_ref[...] = jnp.full(m_ref.shape, LOGIT_NEG, F32)
    l_ref[...] = jnp.zeros(l_ref.shape, F32)
    acc_ref[...] = jnp.zeros(acc_ref.shape, F32)

    def attend_tile(kb, near):
        r0 = pl.multiple_of(kb * TK, TK)
        mb = jnp.where(sc_ref[pl.ds(r0, TK), :] > theta, 0.0, LOGIT_NEG)
        m_new, alpha = [], []
        for h in range(H):
            p0 = (h // 2) * 2 * DH
            lg = _dot_nt(k_ref[pl.ds(r0, TK), p0:p0 + 2 * DH], qm_ref[h]) + mb
            if near is not None:
                lg = lg + bias_ref[near, h]
            lg_ref[h] = lg
            m_old = m_ref[h:h + 1, :]
            m_new.append(jnp.maximum(m_old, jnp.max(lg, axis=0, keepdims=True)))
            alpha.append(jnp.exp2(m_old - m_new[h]))
            m_ref[h:h + 1, :] = m_new[h]
        for h in range(H):
            p = jnp.exp2(lg_ref[h] - m_new[h])
            l_ref[h:h + 1, :] = alpha[h] * l_ref[h:h + 1, :] + jnp.sum(p, axis=0, keepdims=True)
            p_ref[h] = p.astype(BF16)
        for h in range(H):
            pv = _dot(vt_ref[h * DH:(h + 1) * DH, pl.ds(r0, TK)], p_ref[h])
            acc_ref[h * DH:(h + 1) * DH, :] = alpha[h] * acc_ref[h * DH:(h + 1) * DH, :] + pv

    def far_loop(kb, carry):
        attend_tile(kb, None)
        return carry

    lax.fori_loop(0, j - 1, far_loop, 0)

    @pl.when(j >= 1)
    def _():
        attend_tile(j - 1, 0)

    attend_tile(j, 1)

    for h in range(H):
        acc_ref[h * DH:(h + 1) * DH, :] = acc_ref[h * DH:(h + 1) * DH, :] / l_ref[h:h + 1, :]
    out_ref[...] = (acc_ref[...].T * az_ref[...]).astype(BF16)


def _dsa(rel_bias, iq, sm, aq, az, ik2, ak, avt, batch, seq):
    TQ = Q_TILE
    nq = seq // TQ
    width = A_HEADS * A_DH
    maps = jnp.asarray(_bucket_maps())
    qrow = lambda b, j: (b * nq + j, 0)
    return pl.pallas_call(
        _dsa_body,
        grid=(batch, nq),
        in_specs=[
            pl.BlockSpec(memory_space=pltpu.SMEM),
            pl.BlockSpec((TQ, width), qrow),
            pl.BlockSpec((TQ, 128), qrow),
            pl.BlockSpec((TQ, width), qrow),
            pl.BlockSpec((TQ, width), qrow),
            pl.BlockSpec((seq, 128), lambda b, j: (b, 0), pipeline_mode=pl.Buffered(1)),
            pl.BlockSpec((seq, width), lambda b, j: (b, 0), pipeline_mode=pl.Buffered(1)),
            pl.BlockSpec((None, width, seq), lambda b, j: (b, 0, 0), pipeline_mode=pl.Buffered(1)),
            _resident(maps.shape),
        ],
        out_specs=pl.BlockSpec((TQ, width), qrow),
        out_shape=jax.ShapeDtypeStruct((batch * seq, width), BF16),
        scratch_shapes=[
            pltpu.VMEM((seq, TQ), F32),
            pltpu.VMEM((A_HEADS, TQ, 2 * A_DH), BF16),
            pltpu.VMEM((A_HEADS, TQ, TQ), BF16),
            pltpu.VMEM((A_HEADS, TQ, 2 * A_DH), BF16),
            pltpu.VMEM((2, A_HEADS, K_TILE, TQ), F32),
            pltpu.VMEM((A_HEADS * A_DH, TQ), F32),
            pltpu.VMEM((8, TQ), F32),
            pltpu.VMEM((8, TQ), F32),
            pltpu.VMEM((A_HEADS, K_TILE, TQ), F32),
            pltpu.VMEM((A_HEADS, K_TILE, TQ), BF16),
        ],
        compiler_params=pltpu.CompilerParams(
            dimension_semantics=("arbitrary", "arbitrary"), vmem_limit_bytes=VMEM_LIMIT),
        name="dsa",
    )(rel_bias, iq, sm, aq, az, ik2, ak, avt, maps)


def _outproj_body(x_ref, hm_ref, ha_ref, p_ref, wom_ref, woa_ref, wple_ref, wg_ref, g_ref, o_ref):
    h1 = x_ref[...] + _dot(hm_ref[...], wom_ref[...]) + _dot(ha_ref[...], woa_ref[...])
    gate = jax.nn.sigmoid(_dot(h1.astype(BF16), wg_ref[...]))
    h2 = h1 + _dot(p_ref[...].astype(BF16), wple_ref[...]) * gate
    ms = jnp.mean(h2 * h2, axis=-1, keepdims=True)
    o_ref[...] = h2 * lax.rsqrt(ms + EPS) * g_ref[...]


def _outproj(x2, hm, ha, p2, wom, woa, wple, wg, g):
    rows, d = x2.shape
    tm = ROW_TILE
    tile = lambda n: pl.BlockSpec((tm, n), lambda i: (i, 0))
    return pl.pallas_call(
        _outproj_body,
        grid=(rows // tm,),
        in_specs=[tile(d), tile(hm.shape[1]), tile(ha.shape[1]), tile(p2.shape[1]),
                  _resident(wom.shape), _resident(woa.shape), _resident(wple.shape),
                  _resident(wg.shape), _resident(g.shape)],
        out_specs=tile(d),
        out_shape=jax.ShapeDtypeStruct((rows, d), F32),
        compiler_params=pltpu.CompilerParams(
            dimension_semantics=("arbitrary",), vmem_limit_bytes=VMEM_LIMIT),
        name="outproj",
    )(x2, hm, ha, p2, wom, woa, wple, wg, g)


def kernel(x, p, w_in, b_gate, conv_w, w_out, norm_in, m_norm, rel_bias, w_ple, w_ple_gate, norm_final):
    batch, seq, d = x.shape
    assert w_in.shape[0] == 1 and p.shape[0] == 1, "single-layer block"
    assert seq % Q_TILE == 0 and seq % M_CHUNK == 0 and (batch * seq) % ROW_TILE == 0
    rows = batch * seq
    x2 = x.reshape(rows, d)

    sizes = [M_HEADS * M_QK, M_HEADS * M_QK, M_HEADS * M_V, M_HEADS * M_V, M_HEADS * M_V, 2 * M_HEADS,
             A_HEADS * A_DH, A_HEADS * A_DH, A_HEADS * A_DH, A_HEADS * A_DH,
             IDX_HEADS * IDX_DH, IDX_DH, IDX_HEADS]
    offs = np.concatenate([[0], np.cumsum(sizes)])
    wi = w_in[0]
    col = lambda i: wi[:, int(offs[i]):int(offs[i + 1])]
    w_if, w_ixk, w_ixw = col(5), col(11), col(12)
    zeros = lambda n: jnp.zeros((d, n), wi.dtype)
    w_sm = jnp.concatenate([w_if[:, :M_HEADS], zeros(4), w_if[:, M_HEADS:], zeros(4), w_ixw, zeros(104)], axis=1)
    w_ik2 = jnp.concatenate([w_ixk, w_ixk], axis=1)
    ws = [jnp.concatenate([col(0), col(1)], axis=1), col(2), col(3), col(4), w_sm, w_ik2,
          col(6), col(7), col(8), col(9), col(10)]
    ws = [w.astype(BF16) for w in ws]
    bg = b_gate[0]
    bpad = jnp.concatenate([bg[:M_HEADS], jnp.zeros((4,), F32), bg[M_HEADS:], jnp.zeros((116,), F32)])[None, :]

    mqk, mv, gate, sm, ik2, aq, ak, av, az, iq = _inproj(x2, norm_in[0][None, :], ws)

    hm = _mlstm(mqk, mv, gate, sm, conv_w[0], bpad, m_norm[0][None, :], batch, seq)

    avt = av.reshape(batch, seq, -1).transpose(0, 2, 1)
    ha = _dsa(rel_bias, iq, sm, aq, az, ik2, ak, avt, batch, seq)

    wo = w_out[0].astype(BF16)
    out = _outproj(x2, hm, ha, p[0].reshape(rows, -1), wo[:M_HEADS * M_V], wo[M_HEADS * M_V:],
                   w_ple[0].astype(BF16), w_ple_gate[0].astype(BF16), norm_final[None, :])
    return out.reshape(batch, seq, d)
```

```python
import functools
import math

import numpy as np
import jax
import jax.numpy as jnp
from jax import lax
from jax.experimental import pallas as pl
from jax.experimental.pallas import tpu as pltpu

F32 = jnp.float32
BF16 = jnp.bfloat16
I32 = jnp.int32

M_HEADS = 4
M_QK = 128
M_V = 256
CONV_W = 4
A_HEADS = 8
A_DH = 64
IDX_HEADS = 8
IDX_DH = 64
TOPK_MAX = 256
REL_BUCKETS = 32
REL_MAX_DIST = 128
EPS = 1e-6

LOG2E = 1.4426950408889634
NEG_BIG = -3.0e38
LOGIT_NEG = -1.0e30

ROW_TILE = 512
M_CHUNK = 256
Q_TILE = 256
K_TILE = 256
ONES_ROWS = 16
BISECT_MAX_ITERS = 256
VMEM_LIMIT = 56 * 1024 * 1024


def _silu(v):
    return v * jax.nn.sigmoid(v)


def _dot(a, b):
    return jnp.dot(a, b, preferred_element_type=F32)


def _dot_nt(a, b):
    return lax.dot_general(a, b, (((1,), (1,)), ((), ())), preferred_element_type=F32)


def _dot_tn(a, b):
    return lax.dot_general(a, b, (((0,), (0,)), ((), ())), preferred_element_type=F32)


def _resident(shape):
    nd = len(shape)
    return pl.BlockSpec(shape, lambda *_: (0,) * nd, pipeline_mode=pl.Buffered(1))


def _inproj_body(x_ref, g_ref, w_mqk, w_mv, w_mo, w_mz, w_sm, w_ik, w_aq, w_ak, w_av, w_az, w_iq,
                 o_mqk, o_mv, o_gate, o_sm, o_ik, o_aq, o_ak, o_avt, o_az, o_iq):
    x = x_ref[...]
    ms = jnp.mean(x * x, axis=-1, keepdims=True)
    xn = (x * lax.rsqrt(ms + EPS) * g_ref[...]).astype(BF16)
    o_mqk[...] = _dot(xn, w_mqk[...])
    o_mv[...] = _dot(xn, w_mv[...]).astype(BF16)
    o_gate[...] = jax.nn.sigmoid(_dot(xn, w_mo[...])) * _silu(_dot(xn, w_mz[...]))
    sm = _dot(xn, w_sm[...])
    o_sm[...] = sm
    o_ik[...] = _dot(xn, w_ik[...]).astype(BF16)
    o_aq[...] = (_dot(xn, w_aq[...]) * (A_DH ** -0.5 * LOG2E)).astype(BF16)
    o_ak[...] = _dot(xn, w_ak[...]).astype(BF16)
    o_avt[...] = _dot(xn, w_av[...]).T.astype(BF16)
    o_az[...] = _silu(_dot(xn, w_az[...]))
    iq = _dot(xn, w_iq[...])
    wsc = sm[:, 16:16 + IDX_HEADS] * (IDX_HEADS ** -0.5 * IDX_DH ** -0.5)
    lane = lax.broadcasted_iota(I32, (x.shape[0], 2 * IDX_DH), 1)
    for pair in range(IDX_HEADS // 2):
        wpair = jnp.where(lane < IDX_DH, wsc[:, 2 * pair:2 * pair + 1], wsc[:, 2 * pair + 1:2 * pair + 2])
        sl = slice(pair * 2 * IDX_DH, (pair + 1) * 2 * IDX_DH)
        o_iq[:, sl] = (iq[:, sl] * wpair).astype(BF16)


def _inproj(x2, g, ws, batch, seq):
    rows, d = x2.shape
    tm = ROW_TILE
    per_seq = seq // tm
    out_dtypes = [F32, BF16, F32, F32, BF16, BF16, BF16, BF16, F32, BF16]
    widths = [w.shape[1] for w in ws]
    out_widths = [widths[0], widths[1], widths[2]] + widths[4:]
    in_specs = [pl.BlockSpec((tm, d), lambda i: (i, 0)), _resident((1, d))]
    in_specs += [_resident(w.shape) for w in ws]
    out_specs = [pl.BlockSpec((tm, n), lambda i: (i, 0)) for n in out_widths]
    out_shape = [jax.ShapeDtypeStruct((rows, n), dt) for n, dt in zip(out_widths, out_dtypes)]
    av_pos = 7
    out_specs[av_pos] = pl.BlockSpec((None, out_widths[av_pos], tm), lambda i: (i // per_seq, 0, i % per_seq))
    out_shape[av_pos] = jax.ShapeDtypeStruct((batch, out_widths[av_pos], seq), BF16)
    return pl.pallas_call(
        _inproj_body,
        grid=(rows // tm,),
        in_specs=in_specs,
        out_specs=out_specs,
        out_shape=out_shape,
        compiler_params=pltpu.CompilerParams(
            dimension_semantics=("arbitrary",), vmem_limit_bytes=VMEM_LIMIT),
        name="inproj",
    )(x2, g, *ws)


def _lane_scan(v, op, fill):
    n = v.shape[1]
    lane = lax.broadcasted_iota(I32, v.shape, 1)
    sh = 1
    while sh < n:
        v = op(v, jnp.where(lane >= sh, pltpu.roll(v, sh, 1), fill))
        sh *= 2
    return v


def _mlstm_body(mqk_ref, mv_ref, gate_ref, sm_ref, convw_ref, bpad_ref, mnorm_ref, out_ref,
                ext_ref, c_ref, n_ref, m_ref):
    L = M_CHUNK
    H, DK, DV = M_HEADS, M_QK, M_V
    HALO = 8

    @pl.when(pl.program_id(1) == 0)
    def _():
        ext_ref[0:HALO, :] = jnp.zeros((HALO, 2 * H * DK), F32)
        c_ref[...] = jnp.zeros(c_ref.shape, F32)
        n_ref[...] = jnp.zeros(n_ref.shape, F32)
        m_ref[...] = jnp.zeros(m_ref.shape, F32)

    raw = mqk_ref[...]
    ext_ref[HALO:HALO + L, :] = raw
    base = HALO - (CONV_W - 1)
    y = ext_ref[base:base + L, :] * convw_ref[0:1, :]
    for j in range(1, CONV_W):
        y = y + ext_ref[base + j:base + j + L, :] * convw_ref[j:j + 1, :]
    ext_ref[0:HALO, :] = raw[L - HALO:L, :]
    qk = _silu(y)

    gt = (sm_ref[...] + bpad_ref[...]).T
    li = gt[0:8, :]
    lf = jax.nn.log_sigmoid(gt[8:16, :])
    b = _lane_scan(lf, jnp.add, 0.0)
    a = li - b
    m_prev = m_ref[:, 0:1]
    mrow = jnp.maximum(m_prev, _lane_scan(a, jnp.maximum, -jnp.inf))
    w_inter = jnp.exp(m_prev - mrow)
    e_neg_mt = jnp.exp(-(b + mrow))
    m_last = mrow[:, L - 1:L]
    wk = jnp.exp(a - m_last)
    decay = jnp.exp(m_prev - m_last)
    m_new = b[:, L - 1:L] + m_last
    stack = jnp.concatenate([mrow, w_inter, e_neg_mt, wk, jnp.zeros((128 - 32, L), F32)], axis=0)
    cols = stack.T

    ri = lax.broadcasted_iota(I32, (L, L), 0)
    ci = lax.broadcasted_iota(I32, (L, L), 1)
    causal = ci <= ri

    for h in range(H):
        q = qk[:, h * DK:(h + 1) * DK]
        k = qk[:, (H + h) * DK:(H + h + 1) * DK] * (DK ** -0.5)
        qb = q.astype(BF16)
        v = mv_ref[:, h * DV:(h + 1) * DV]
        m_col = cols[:, h:h + 1]
        wi_col = cols[:, 8 + h:9 + h]
        emt_col = cols[:, 16 + h:17 + h]
        wk_col = cols[:, 24 + h:25 + h]
        e = jnp.where(causal, jnp.exp(a[h:h + 1, :] - m_col), 0.0)
        s = _dot_nt(qb, k.astype(BF16)) * e
        c_old = c_ref[h]
        num = _dot(s.astype(BF16), v) + wi_col * _dot(qb, c_old.astype(BF16))
        den = (jnp.sum(s, axis=1, keepdims=True)
               + wi_col * jnp.sum(q * n_ref[h:h + 1, :], axis=1, keepdims=True))
        den = jnp.maximum(jnp.abs(den), emt_col)
        hh = num / den
        hn = hh * lax.rsqrt(jnp.mean(hh * hh, axis=-1, keepdims=True) + EPS)
        hn = hn * mnorm_ref[:, h * DV:(h + 1) * DV]
        out_ref[:, h * DV:(h + 1) * DV] = (gate_ref[:, h * DV:(h + 1) * DV] * hn).astype(BF16)
        kw = k * wk_col
        dec = decay[h:h + 1, :]
        c_ref[h] = dec * c_old + _dot_tn(kw.astype(BF16), v)
        n_ref[h:h + 1, :] = dec * n_ref[h:h + 1, :] + jnp.sum(kw, axis=0, keepdims=True)
    m_ref[...] = jnp.broadcast_to(m_new, m_ref.shape)


def _mlstm(mqk, mv, gate, sm, conv_w, bpad, mnorm, batch, seq):
    L = M_CHUNK
    nc = seq // L
    wqk = mqk.shape[1]
    wv = mv.shape[1]
    row = lambda b, c: (b * nc + c, 0)
    return pl.pallas_call(
        _mlstm_body,
        grid=(batch, nc),
        in_specs=[
            pl.BlockSpec((L, wqk), row),
            pl.BlockSpec((L, wv), row),
            pl.BlockSpec((L, wv), row),
            pl.BlockSpec((L, 128), row),
            _resident(conv_w.shape),
            _resident(bpad.shape),
            _resident(mnorm.shape),
        ],
        out_specs=pl.BlockSpec((L, wv), row),
        out_shape=jax.ShapeDtypeStruct((batch * seq, wv), BF16),
        scratch_shapes=[
            pltpu.VMEM((L + 8, wqk), F32),
            pltpu.VMEM((M_HEADS, M_QK, M_V), F32),
            pltpu.VMEM((8, M_QK), F32),
            pltpu.VMEM((8, 128), F32),
        ],
        compiler_params=pltpu.CompilerParams(
            dimension_semantics=("arbitrary", "arbitrary"), vmem_limit_bytes=VMEM_LIMIT),
        name="mlstm",
    )(mqk, mv, gate, sm, conv_w, bpad, mnorm)


def _t5_bucket_table(max_dist):
    d = np.arange(max_dist)
    max_exact = REL_BUCKETS // 2
    tabs = []
    for dt in (np.float32, np.float64):
        ratio = np.maximum(d, 1).astype(dt) / dt(max_exact)
        large = max_exact + (np.log(ratio) / dt(math.log(REL_MAX_DIST / max_exact))
                             * dt(REL_BUCKETS - max_exact)).astype(np.int32)
        large = np.minimum(large, REL_BUCKETS - 1)
        tabs.append(np.where(d < max_exact, d, large).astype(np.int32))
    assert np.array_equal(tabs[0], tabs[1])
    return tabs[0]


def _bucket_maps():
    tab = _t5_bucket_table(Q_TILE + K_TILE)
    s = np.arange(K_TILE)[:, None]
    t = np.arange(Q_TILE)[None, :]
    diag = tab[np.maximum(t - s, 0)]
    prev = tab[K_TILE + t - s]
    assert np.all(tab[K_TILE:] == REL_BUCKETS - 1)
    return np.stack([prev, diag]).astype(np.int32)


def _dsa_body(rb_ref, iq_ref, sm_ref, aq_ref, az_ref, ik_ref, k_ref, vt_ref, map_ref, out_ref,
              sc_ref, iqm_ref, qm_ref, bias_ref, acc_ref, m_ref, l_ref, lg_ref, p_ref):
    TQ, TK, H, DH = Q_TILE, K_TILE, A_HEADS, A_DH
    j = pl.program_id(1)
    n_tiles = j + 1

    @pl.when((pl.program_id(0) == 0) & (j == 0))
    def _():
        bias_ref[...] = jnp.zeros(bias_ref.shape, F32)

        def fill(bkt, carry):
            for which in range(2):
                hit = map_ref[which] == bkt
                for h in range(H):
                    val = (rb_ref[bkt, h] - rb_ref[REL_BUCKETS - 1, h]) * LOG2E
                    bias_ref[which, h] = jnp.where(hit, val, bias_ref[which, h])
            return carry

        lax.fori_loop(0, REL_BUCKETS - 1, fill, 0)

    lane = lax.broadcasted_iota(I32, (TQ, 2 * DH), 1)
    for h in range(H):
        p0 = (h // 2) * 2 * DH
        keep = (lane < DH) if h % 2 == 0 else (lane >= DH)
        iqm_ref[h] = jnp.where(keep, iq_ref[:, p0:p0 + 2 * DH], jnp.zeros((), BF16))
        qm_ref[h] = jnp.where(keep, aq_ref[:, p0:p0 + 2 * DH], jnp.zeros((), BF16))
    w_rows = sm_ref[...].T[16:16 + IDX_HEADS, :]
    clamp_lo = jnp.where(w_rows >= 0.0, 0.0, -jnp.inf)
    clamp_hi = jnp.where(w_rows >= 0.0, jnp.inf, 0.0)

    s_loc = lax.broadcasted_iota(I32, (TK, TQ), 0)
    t_loc = lax.broadcasted_iota(I32, (TK, TQ), 1)

    def index_tile(kb, diag):
        r0 = pl.multiple_of(kb * TK, TK)
        lhs = ik_ref[pl.ds(r0, TK), :]
        sc = None
        for h in range(H):
            y = _dot_nt(lhs, iqm_ref[h])
            term = jnp.minimum(jnp.maximum(y, clamp_lo[h:h + 1, :]), clamp_hi[h:h + 1, :])
            sc = term if sc is None else sc + term
        if diag:
            sc = jnp.where(s_loc <= t_loc, sc, NEG_BIG)
        sc_ref[pl.ds(r0, TK), :] = sc

    def index_loop(kb, carry):
        index_tile(kb, False)
        return carry

    lax.fori_loop(0, j, index_loop, 0)
    index_tile(j, True)

    def count_gt(thr):
        def body(kb, cnt):
            r0 = pl.multiple_of(kb * TK, TK)
            hit = jnp.where(sc_ref[pl.ds(r0, TK), :] > thr, 1, 0).astype(I32)
            return cnt + jnp.sum(hit.reshape(TK // 8, 8, TQ), axis=0)
        cnt8 = lax.fori_loop(0, n_tiles, body, jnp.zeros((8, TQ), I32))
        return jnp.sum(cnt8, axis=0, keepdims=True)

    def stats():
        def body(kb, carry):
            c_pos_, c_nn_, mx_, mn_ = carry
            r0 = pl.multiple_of(kb * TK, TK)
            sc = sc_ref[pl.ds(r0, TK), :]
            fold = lambda v: v.reshape(TK // 8, 8, TQ)
            c_pos_ = c_pos_ + jnp.sum(fold(jnp.where(sc > 0.0, 1, 0).astype(I32)), axis=0)
            c_nn_ = c_nn_ + jnp.sum(fold(jnp.where(sc >= 0.0, 1, 0).astype(I32)), axis=0)
            mx_ = jnp.maximum(mx_, jnp.max(fold(sc), axis=0))
            mn_ = jnp.minimum(mn_, jnp.min(fold(jnp.where(sc > NEG_BIG, sc, -NEG_BIG)), axis=0))
            return c_pos_, c_nn_, mx_, mn_
        init = (jnp.zeros((8, TQ), I32), jnp.zeros((8, TQ), I32),
                jnp.full((8, TQ), NEG_BIG, F32), jnp.full((8, TQ), -NEG_BIG, F32))
        c_pos_, c_nn_, mx_, mn_ = lax.fori_loop(0, n_tiles, body, init)
        return (jnp.sum(c_pos_, axis=0, keepdims=True), jnp.sum(c_nn_, axis=0, keepdims=True),
                jnp.max(mx_, axis=0, keepdims=True), jnp.min(mn_, axis=0, keepdims=True))

    t_glob = j * TQ + lax.broadcasted_iota(I32, (1, TQ), 1)
    k_eff = jnp.minimum(TOPK_MAX, t_glob + 1)

    zero = jnp.zeros((1, TQ), F32)
    c_pos, c_nn, hi0, mn = stats()
    all_in = (t_glob + 1) <= TOPK_MAX
    pos_ok = c_pos >= k_eff
    tie0 = (~all_in) & (~pos_ok) & (c_nn >= k_eff)
    lo_neg = jnp.maximum(mn + mn - 1.0, NEG_BIG)
    lo = jnp.where(all_in, NEG_BIG, jnp.where(pos_ok, zero, lo_neg))
    hi = jnp.where((~all_in) & (~pos_ok), zero, hi0)
    fin0 = all_in | (pos_ok & (c_pos == k_eff)) | tie0

    def bis_cond(st):
        it, _, _, fin_, _ = st
        return (it < BISECT_MAX_ITERS) & (jnp.min(fin_) == 0)

    def bis_body(st):
        it, lo_, hi_, fin_, tie_ = st
        mid = 0.5 * lo_ + 0.5 * hi_
        collapsed = (mid <= lo_) | (mid >= hi_)
        cm = count_gt(mid)
        act = fin_ == 0
        up = act & (~collapsed) & (cm >= k_eff)
        dn = act & (~collapsed) & (cm < k_eff)
        lo_n = jnp.where(up, mid, lo_)
        hi_n = jnp.where(dn, mid, hi_)
        tie_n = jnp.where(act & collapsed, 1, tie_)
        fin_n = jnp.where((act & collapsed) | (up & (cm == k_eff)), 1, fin_)
        return it + 1, lo_n, hi_n, fin_n, tie_n

    _, lo, hi, _, tie_i = lax.while_loop(
        bis_cond, bis_body,
        (jnp.int32(0), lo, hi, jnp.where(fin0, 1, 0).astype(I32), jnp.where(tie0, 1, 0).astype(I32)))
    tie = tie_i > 0
    theta_ref = m_ref
    theta_ref[0:1, :] = lo

    @pl.when(jnp.max(tie_i) > 0)
    def _():
        v = hi

        def pass_a(kb, carry):
            n_gt, below = carry
            r0 = pl.multiple_of(kb * TK, TK)
            sc = sc_ref[pl.ds(r0, TK), :]
            n_gt = n_gt + jnp.sum(jnp.where(sc > v, 1.0, 0.0), axis=0, keepdims=True)
            below = jnp.maximum(below, jnp.max(jnp.where(sc < v, sc, NEG_BIG), axis=0, keepdims=True))
            return n_gt, below

        n_gt, below = lax.fori_loop(
            0, n_tiles, pass_a, (jnp.zeros((1, TQ), F32), jnp.full((1, TQ), NEG_BIG, F32)))
        need = k_eff.astype(F32) - n_gt
        tri = jnp.where(lax.broadcasted_iota(I32, (TK, TK), 1) < lax.broadcasted_iota(I32, (TK, TK), 0),
                        1.0, 0.0).astype(BF16)

        def pass_b(kb, seen):
            r0 = pl.multiple_of(kb * TK, TK)
            sc = sc_ref[pl.ds(r0, TK), :]
            eq = (sc == v) & tie
            eqf = jnp.where(eq, 1.0, 0.0)
            rank = seen + _dot(tri, eqf.astype(BF16))
            sc_ref[pl.ds(r0, TK), :] = jnp.where(eq & (rank >= need), NEG_BIG, sc)
            return seen + jnp.sum(eqf, axis=0, keepdims=True)

        lax.fori_loop(0, n_tiles, pass_b, jnp.zeros((1, TQ), F32))
        theta_ref[0:1, :] = jnp.where(tie, below, lo)

    theta = theta_ref[0:1, :]

    m_ref[...] = jnp.full(m_ref.shape, LOGIT_NEG, F32)
    l_ref[...] = jnp.zeros(l_ref.shape, F32)
    acc_ref[...] = jnp.zeros(acc_ref.shape, F32)

    def attend_tile(kb, near):
        r0 = pl.multiple_of(kb * TK, TK)
        mb = jnp.where(sc_ref[pl.ds(r0, TK), :] > theta, 0.0, LOGIT_NEG)
        m_new, alpha = [], []
        for h in range(H):
            p0 = (h // 2) * 2 * DH
            lg = _dot_nt(k_ref[pl.ds(r0, TK), p0:p0 + 2 * DH], qm_ref[h]) + mb
            if near is not None:
                lg = lg + bias_ref[near, h]
            lg_ref[h] = lg
            m_old = m_ref[h:h + 1, :]
            m_new.append(jnp.maximum(m_old, jnp.max(lg, axis=0, keepdims=True)))
            alpha.append(jnp.exp2(m_old - m_new[h]))
            m_ref[h:h + 1, :] = m_new[h]
        for h in range(H):
            p_ref[h] = jnp.exp2(lg_ref[h] - m_new[h]).astype(BF16)
        ones = jnp.ones((ONES_ROWS, TK), BF16)
        for h in range(H):
            lhs = jnp.concatenate([vt_ref[h * DH:(h + 1) * DH, pl.ds(r0, TK)], ones], axis=0)
            pv = _dot(lhs, p_ref[h])
            acc_ref[h * DH:(h + 1) * DH, :] = alpha[h] * acc_ref[h * DH:(h + 1) * DH, :] + pv[0:DH, :]
            l_ref[h:h + 1, :] = alpha[h] * l_ref[h:h + 1, :] + pv[DH:DH + 1, :]

    def far_loop(kb, carry):
        attend_tile(kb, None)
        return carry

    lax.fori_loop(0, j - 1, far_loop, 0)

    @pl.when(j >= 1)
    def _():
        attend_tile(j - 1, 0)

    attend_tile(j, 1)

    for h in range(H):
        acc_ref[h * DH:(h + 1) * DH, :] = acc_ref[h * DH:(h + 1) * DH, :] / l_ref[h:h + 1, :]
    out_ref[...] = (acc_ref[...].T * az_ref[...]).astype(BF16)


def _dsa(rel_bias, iq, sm, aq, az, ik2, ak, avt, batch, seq):
    TQ = Q_TILE
    nq = seq // TQ
    width = A_HEADS * A_DH
    maps = jnp.asarray(_bucket_maps())
    qrow = lambda b, j: (b * nq + j, 0)
    return pl.pallas_call(
        _dsa_body,
        grid=(batch, nq),
        in_specs=[
            pl.BlockSpec(memory_space=pltpu.SMEM),
            pl.BlockSpec((TQ, width), qrow),
            pl.BlockSpec((TQ, 128), qrow),
            pl.BlockSpec((TQ, width), qrow),
            pl.BlockSpec((TQ, width), qrow),
            pl.BlockSpec((seq, 128), lambda b, j: (b, 0), pipeline_mode=pl.Buffered(1)),
            pl.BlockSpec((seq, width), lambda b, j: (b, 0), pipeline_mode=pl.Buffered(1)),
            pl.BlockSpec((None, width, seq), lambda b, j: (b, 0, 0), pipeline_mode=pl.Buffered(1)),
            _resident(maps.shape),
        ],
        out_specs=pl.BlockSpec((TQ, width), qrow),
        out_shape=jax.ShapeDtypeStruct((batch * seq, width), BF16),
        scratch_shapes=[
            pltpu.VMEM((seq, TQ), F32),
            pltpu.VMEM((A_HEADS, TQ, 2 * A_DH), BF16),
            pltpu.VMEM((A_HEADS, TQ, 2 * A_DH), BF16),
            pltpu.VMEM((2, A_HEADS, K_TILE, TQ), F32),
            pltpu.VMEM((A_HEADS * A_DH, TQ), F32),
            pltpu.VMEM((8, TQ), F32),
            pltpu.VMEM((8, TQ), F32),
            pltpu.VMEM((A_HEADS, K_TILE, TQ), F32),
            pltpu.VMEM((A_HEADS, K_TILE, TQ), BF16),
        ],
        compiler_params=pltpu.CompilerParams(
            dimension_semantics=("arbitrary", "arbitrary"), vmem_limit_bytes=VMEM_LIMIT),
        name="dsa",
    )(rel_bias, iq, sm, aq, az, ik2, ak, avt, maps)


def _outproj_body(x_ref, hm_ref, ha_ref, p_ref, wom_ref, woa_ref, wple_ref, wg_ref, g_ref, o_ref):
    h1 = x_ref[...] + _dot(hm_ref[...], wom_ref[...]) + _dot(ha_ref[...], woa_ref[...])
    gate = jax.nn.sigmoid(_dot(h1.astype(BF16), wg_ref[...]))
    h2 = h1 + _dot(p_ref[...].astype(BF16), wple_ref[...]) * gate
    ms = jnp.mean(h2 * h2, axis=-1, keepdims=True)
    o_ref[...] = h2 * lax.rsqrt(ms + EPS) * g_ref[...]


def _outproj(x2, hm, ha, p2, wom, woa, wple, wg, g):
    rows, d = x2.shape
    tm = ROW_TILE
    tile = lambda n: pl.BlockSpec((tm, n), lambda i: (i, 0))
    return pl.pallas_call(
        _outproj_body,
        grid=(rows // tm,),
        in_specs=[tile(d), tile(hm.shape[1]), tile(ha.shape[1]), tile(p2.shape[1]),
                  _resident(wom.shape), _resident(woa.shape), _resident(wple.shape),
                  _resident(wg.shape), _resident(g.shape)],
        out_specs=tile(d),
        out_shape=jax.ShapeDtypeStruct((rows, d), F32),
        compiler_params=pltpu.CompilerParams(
            dimension_semantics=("arbitrary",), vmem_limit_bytes=VMEM_LIMIT),
        name="outproj",
    )(x2, hm, ha, p2, wom, woa, wple, wg, g)


def kernel(x, p, w_in, b_gate, conv_w, w_out, norm_in, m_norm, rel_bias, w_ple, w_ple_gate, norm_final):
    batch, seq, d = x.shape
    assert w_in.shape[0] == 1 and p.shape[0] == 1, "single-layer block"
    assert seq % Q_TILE == 0 and seq % M_CHUNK == 0 and (batch * seq) % ROW_TILE == 0
    rows = batch * seq
    x2 = x.reshape(rows, d)

    sizes = [M_HEADS * M_QK, M_HEADS * M_QK, M_HEADS * M_V, M_HEADS * M_V, M_HEADS * M_V, 2 * M_HEADS,
             A_HEADS * A_DH, A_HEADS * A_DH, A_HEADS * A_DH, A_HEADS * A_DH,
             IDX_HEADS * IDX_DH, IDX_DH, IDX_HEADS]
    offs = np.concatenate([[0], np.cumsum(sizes)])
    wi = w_in[0]
    col = lambda i: wi[:, int(offs[i]):int(offs[i + 1])]
    w_if, w_ixk, w_ixw = col(5), col(11), col(12)
    zeros = lambda n: jnp.zeros((d, n), wi.dtype)
    w_sm = jnp.concatenate([w_if[:, :M_HEADS], zeros(4), w_if[:, M_HEADS:], zeros(4), w_ixw, zeros(104)], axis=1)
    w_ik2 = jnp.concatenate([w_ixk, w_ixk], axis=1)
    ws = [jnp.concatenate([col(0), col(1)], axis=1), col(2), col(3), col(4), w_sm, w_ik2,
          col(6), col(7), col(8), col(9), col(10)]
    ws = [w.astype(BF16) for w in ws]
    bg = b_gate[0]
    bpad = jnp.concatenate([bg[:M_HEADS], jnp.zeros((4,), F32), bg[M_HEADS:], jnp.zeros((116,), F32)])[None, :]

    mqk, mv, gate, sm, ik2, aq, ak, avt, az, iq = _inproj(x2, norm_in[0][None, :], ws, batch, seq)

    hm = _mlstm(mqk, mv, gate, sm, conv_w[0], bpad, m_norm[0][None, :], batch, seq)

    ha = _dsa(rel_bias, iq, sm, aq, az, ik2, ak, avt, batch, seq)

    wo = w_out[0].astype(BF16)
    out = _outproj(x2, hm, ha, p[0].reshape(rows, -1), wo[:M_HEADS * M_V], wo[M_HEADS * M_V:],
                   w_ple[0].astype(BF16), w_ple_gate[0].astype(BF16), norm_final[None, :])
    return out.reshape(batch, seq, d)
```

```python
import functools
import math

import numpy as np
import jax
import jax.numpy as jnp
from jax import lax
from jax.experimental import pallas as pl
from jax.experimental.pallas import tpu as pltpu

F32 = jnp.float32
BF16 = jnp.bfloat16
I32 = jnp.int32

M_HEADS = 4
M_QK = 128
M_V = 256
CONV_W = 4
A_HEADS = 8
A_DH = 64
IDX_HEADS = 8
IDX_DH = 64
TOPK_MAX = 256
REL_BUCKETS = 32
REL_MAX_DIST = 128
EPS = 1e-6

LOG2E = 1.4426950408889634
NEG_BIG = -3.0e38
LOGIT_NEG = -1.0e30

ROW_TILE = 512
M_CHUNK = 256
Q_TILE = 256
K_TILE = 256
ONES_ROWS = 16
BISECT_MAX_ITERS = 256
VMEM_LIMIT = 56 * 1024 * 1024


def _silu(v):
    return v * jax.nn.sigmoid(v)


def _dot(a, b):
    return jnp.dot(a, b, preferred_element_type=F32)


def _dot_nt(a, b):
    return lax.dot_general(a, b, (((1,), (1,)), ((), ())), preferred_element_type=F32)


def _dot_tn(a, b):
    return lax.dot_general(a, b, (((0,), (0,)), ((), ())), preferred_element_type=F32)


def _resident(shape):
    nd = len(shape)
    return pl.BlockSpec(shape, lambda *_: (0,) * nd, pipeline_mode=pl.Buffered(1))


def _inproj_body(x_ref, g_ref, w_mqk, w_mv, w_mo, w_mz, w_sm, w_ik, w_aq, w_ak, w_av, w_az, w_iq,
                 o_mqk, o_mv, o_gate, o_sm, o_ik, o_aq, o_ak, o_avt, o_az, o_iq):
    x = x_ref[...]
    ms = jnp.mean(x * x, axis=-1, keepdims=True)
    xn = (x * lax.rsqrt(ms + EPS) * g_ref[...]).astype(BF16)
    o_mqk[...] = _dot(xn, w_mqk[...])
    o_mv[...] = _dot(xn, w_mv[...]).astype(BF16)
    o_gate[...] = jax.nn.sigmoid(_dot(xn, w_mo[...])) * _silu(_dot(xn, w_mz[...]))
    sm = _dot(xn, w_sm[...])
    o_sm[...] = sm
    o_ik[...] = _dot(xn, w_ik[...]).astype(BF16)
    o_aq[...] = (_dot(xn, w_aq[...]) * (A_DH ** -0.5 * LOG2E)).astype(BF16)
    o_ak[...] = _dot(xn, w_ak[...]).astype(BF16)
    o_avt[...] = _dot(xn, w_av[...]).T.astype(BF16)
    o_az[...] = _silu(_dot(xn, w_az[...]))
    iq = _dot(xn, w_iq[...])
    wsc = sm[:, 16:16 + IDX_HEADS] * (IDX_HEADS ** -0.5 * IDX_DH ** -0.5)
    lane = lax.broadcasted_iota(I32, (x.shape[0], 2 * IDX_DH), 1)
    for pair in range(IDX_HEADS // 2):
        wpair = jnp.where(lane < IDX_DH, wsc[:, 2 * pair:2 * pair + 1], wsc[:, 2 * pair + 1:2 * pair + 2])
        sl = slice(pair * 2 * IDX_DH, (pair + 1) * 2 * IDX_DH)
        o_iq[:, sl] = (iq[:, sl] * wpair).astype(BF16)


def _inproj(x2, g, ws, batch, seq):
    rows, d = x2.shape
    tm = ROW_TILE
    per_seq = seq // tm
    out_dtypes = [F32, BF16, F32, F32, BF16, BF16, BF16, BF16, F32, BF16]
    widths = [w.shape[1] for w in ws]
    out_widths = [widths[0], widths[1], widths[2]] + widths[4:]
    in_specs = [pl.BlockSpec((tm, d), lambda i: (i, 0)), _resident((1, d))]
    in_specs += [_resident(w.shape) for w in ws]
    out_specs = [pl.BlockSpec((tm, n), lambda i: (i, 0)) for n in out_widths]
    out_shape = [jax.ShapeDtypeStruct((rows, n), dt) for n, dt in zip(out_widths, out_dtypes)]
    av_pos = 7
    out_specs[av_pos] = pl.BlockSpec((None, out_widths[av_pos], tm), lambda i: (i // per_seq, 0, i % per_seq))
    out_shape[av_pos] = jax.ShapeDtypeStruct((batch, out_widths[av_pos], seq), BF16)
    return pl.pallas_call(
        _inproj_body,
        grid=(rows // tm,),
        in_specs=in_specs,
        out_specs=out_specs,
        out_shape=out_shape,
        compiler_params=pltpu.CompilerParams(
            dimension_semantics=("arbitrary",), vmem_limit_bytes=VMEM_LIMIT),
        name="inproj",
    )(x2, g, *ws)


def _lane_scan(v, op, fill):
    n = v.shape[1]
    lane = lax.broadcasted_iota(I32, v.shape, 1)
    sh = 1
    while sh < n:
        v = op(v, jnp.where(lane >= sh, pltpu.roll(v, sh, 1), fill))
        sh *= 2
    return v


def _mlstm_body(mqk_ref, mv_ref, gate_ref, sm_ref, convw_ref, bpad_ref, mnorm_ref, out_ref,
                ext_ref, c_ref, n_ref, m_ref):
    L = M_CHUNK
    H, DK, DV = M_HEADS, M_QK, M_V
    HALO = 8

    @pl.when(pl.program_id(1) == 0)
    def _():
        ext_ref[0:HALO, :] = jnp.zeros((HALO, 2 * H * DK), F32)
        c_ref[...] = jnp.zeros(c_ref.shape, F32)
        n_ref[...] = jnp.zeros(n_ref.shape, F32)
        m_ref[...] = jnp.zeros(m_ref.shape, F32)

    raw = mqk_ref[...]
    ext_ref[HALO:HALO + L, :] = raw
    base = HALO - (CONV_W - 1)
    y = ext_ref[base:base + L, :] * convw_ref[0:1, :]
    for j in range(1, CONV_W):
        y = y + ext_ref[base + j:base + j + L, :] * convw_ref[j:j + 1, :]
    ext_ref[0:HALO, :] = raw[L - HALO:L, :]
    qk = _silu(y)

    gt = (sm_ref[...] + bpad_ref[...]).T
    li = gt[0:8, :]
    lf = jax.nn.log_sigmoid(gt[8:16, :])
    b = _lane_scan(lf, jnp.add, 0.0)
    a = li - b
    m_prev = m_ref[:, 0:1]
    mrow = jnp.maximum(m_prev, _lane_scan(a, jnp.maximum, -jnp.inf))
    w_inter = jnp.exp(m_prev - mrow)
    e_neg_mt = jnp.exp(-(b + mrow))
    m_last = mrow[:, L - 1:L]
    wk = jnp.exp(a - m_last)
    decay = jnp.exp(m_prev - m_last)
    m_new = b[:, L - 1:L] + m_last
    stack = jnp.concatenate([mrow, w_inter, e_neg_mt, wk, jnp.zeros((128 - 32, L), F32)], axis=0)
    cols = stack.T

    ri = lax.broadcasted_iota(I32, (L, L), 0)
    ci = lax.broadcasted_iota(I32, (L, L), 1)
    causal = ci <= ri

    for h in range(H):
        q = qk[:, h * DK:(h + 1) * DK]
        k = qk[:, (H + h) * DK:(H + h + 1) * DK] * (DK ** -0.5)
        qb = q.astype(BF16)
        v = mv_ref[:, h * DV:(h + 1) * DV]
        m_col = cols[:, h:h + 1]
        wi_col = cols[:, 8 + h:9 + h]
        emt_col = cols[:, 16 + h:17 + h]
        wk_col = cols[:, 24 + h:25 + h]
        e = jnp.where(causal, jnp.exp(a[h:h + 1, :] - m_col), 0.0)
        s = _dot_nt(qb, k.astype(BF16)) * e
        c_old = c_ref[h]
        num = _dot(s.astype(BF16), v) + wi_col * _dot(qb, c_old.astype(BF16))
        den = (jnp.sum(s, axis=1, keepdims=True)
               + wi_col * jnp.sum(q * n_ref[h:h + 1, :], axis=1, keepdims=True))
        den = jnp.maximum(jnp.abs(den), emt_col)
        hh = num / den
        hn = hh * lax.rsqrt(jnp.mean(hh * hh, axis=-1, keepdims=True) + EPS)
        hn = hn * mnorm_ref[:, h * DV:(h + 1) * DV]
        out_ref[:, h * DV:(h + 1) * DV] = (gate_ref[:, h * DV:(h + 1) * DV] * hn).astype(BF16)
        kw = k * wk_col
        dec = decay[h:h + 1, :]
        c_ref[h] = dec * c_old + _dot_tn(kw.astype(BF16), v)
        n_ref[h:h + 1, :] = dec * n_ref[h:h + 1, :] + jnp.sum(kw, axis=0, keepdims=True)
    m_ref[...] = jnp.broadcast_to(m_new, m_ref.shape)


def _mlstm(mqk, mv, gate, sm, conv_w, bpad, mnorm, batch, seq):
    L = M_CHUNK
    nc = seq // L
    wqk = mqk.shape[1]
    wv = mv.shape[1]
    row = lambda b, c: (b * nc + c, 0)
    return pl.pallas_call(
        _mlstm_body,
        grid=(batch, nc),
        in_specs=[
            pl.BlockSpec((L, wqk), row),
            pl.BlockSpec((L, wv), row),
            pl.BlockSpec((L, wv), row),
            pl.BlockSpec((L, 128), row),
            _resident(conv_w.shape),
            _resident(bpad.shape),
            _resident(mnorm.shape),
        ],
        out_specs=pl.BlockSpec((L, wv), row),
        out_shape=jax.ShapeDtypeStruct((batch * seq, wv), BF16),
        scratch_shapes=[
            pltpu.VMEM((L + 8, wqk), F32),
            pltpu.VMEM((M_HEADS, M_QK, M_V), F32),
            pltpu.VMEM((8, M_QK), F32),
            pltpu.VMEM((8, 128), F32),
        ],
        compiler_params=pltpu.CompilerParams(
            dimension_semantics=("arbitrary", "arbitrary"), vmem_limit_bytes=VMEM_LIMIT),
        name="mlstm",
    )(mqk, mv, gate, sm, conv_w, bpad, mnorm)


def _t5_bucket_table(max_dist):
    d = np.arange(max_dist)
    max_exact = REL_BUCKETS // 2
    tabs = []
    for dt in (np.float32, np.float64):
        ratio = np.maximum(d, 1).astype(dt) / dt(max_exact)
        large = max_exact + (np.log(ratio) / dt(math.log(REL_MAX_DIST / max_exact))
                             * dt(REL_BUCKETS - max_exact)).astype(np.int32)
        large = np.minimum(large, REL_BUCKETS - 1)
        tabs.append(np.where(d < max_exact, d, large).astype(np.int32))
    assert np.array_equal(tabs[0], tabs[1])
    return tabs[0]


def _bucket_maps():
    tab = _t5_bucket_table(Q_TILE + K_TILE)
    s = np.arange(K_TILE)[:, None]
    t = np.arange(Q_TILE)[None, :]
    diag = tab[np.maximum(t - s, 0)]
    prev = tab[K_TILE + t - s]
    assert np.all(tab[K_TILE:] == REL_BUCKETS - 1)
    return np.stack([prev, diag]).astype(np.int32)


def _dsa_body(rb_ref, iq_ref, sm_ref, aq_ref, az_ref, ik_ref, k_ref, vt_ref, map_ref, out_ref,
              sc_ref, iqm_ref, qm_ref, bias_ref, acc_ref, m_ref, l_ref, alpha_ref, mtile_ref, lg_ref, p_ref):
    TQ, TK, H, DH = Q_TILE, K_TILE, A_HEADS, A_DH
    j = pl.program_id(1)
    n_tiles = j + 1

    @pl.when((pl.program_id(0) == 0) & (j == 0))
    def _():
        bias_ref[...] = jnp.zeros(bias_ref.shape, F32)

        def fill(bkt, carry):
            for which in range(2):
                hit = map_ref[which] == bkt
                for h in range(H):
                    val = (rb_ref[bkt, h] - rb_ref[REL_BUCKETS - 1, h]) * LOG2E
                    bias_ref[which, h] = jnp.where(hit, val, bias_ref[which, h])
            return carry

        lax.fori_loop(0, REL_BUCKETS - 1, fill, 0)

    lane = lax.broadcasted_iota(I32, (TQ, 2 * DH), 1)
    for h in range(H):
        p0 = (h // 2) * 2 * DH
        keep = (lane < DH) if h % 2 == 0 else (lane >= DH)
        iqm_ref[h] = jnp.where(keep, iq_ref[:, p0:p0 + 2 * DH], jnp.zeros((), BF16))
        qm_ref[h] = jnp.where(keep, aq_ref[:, p0:p0 + 2 * DH], jnp.zeros((), BF16))
    w_rows = sm_ref[...].T[16:16 + IDX_HEADS, :]
    clamp_lo = jnp.where(w_rows >= 0.0, 0.0, -jnp.inf)
    clamp_hi = jnp.where(w_rows >= 0.0, jnp.inf, 0.0)

    s_loc = lax.broadcasted_iota(I32, (TK, TQ), 0)
    t_loc = lax.broadcasted_iota(I32, (TK, TQ), 1)

    fold = lambda v: v.reshape(v.shape[0] // 8, 8, TQ)

    def index_tile(kb, diag, stats):
        r0 = pl.multiple_of(kb * TK, TK)
        lhs = ik_ref[pl.ds(r0, TK), :]
        sc = None
        for h in range(H):
            y = _dot_nt(lhs, iqm_ref[h])
            term = jnp.minimum(jnp.maximum(y, clamp_lo[h:h + 1, :]), clamp_hi[h:h + 1, :])
            sc = term if sc is None else sc + term
        sc_min = sc
        if diag:
            admissible = s_loc <= t_loc
            sc_min = jnp.where(admissible, sc, -NEG_BIG)
            sc = jnp.where(admissible, sc, NEG_BIG)
        sc_ref[pl.ds(r0, TK), :] = sc
        c_pos_, c_nn_, mx_, mn_ = stats
        c_pos_ = c_pos_ + jnp.sum(fold(jnp.where(sc > 0.0, 1, 0).astype(I32)), axis=0)
        c_nn_ = c_nn_ + jnp.sum(fold(jnp.where(sc >= 0.0, 1, 0).astype(I32)), axis=0)
        mx_ = jnp.maximum(mx_, jnp.max(fold(sc), axis=0))
        mn_ = jnp.minimum(mn_, jnp.min(fold(sc_min), axis=0))
        return c_pos_, c_nn_, mx_, mn_

    stats0 = (jnp.zeros((8, TQ), I32), jnp.zeros((8, TQ), I32),
              jnp.full((8, TQ), NEG_BIG, F32), jnp.full((8, TQ), -NEG_BIG, F32))
    stats1 = lax.fori_loop(0, j, lambda kb, st: index_tile(kb, False, st), stats0)
    c_pos8, c_nn8, mx8, mn8 = index_tile(j, True, stats1)
    c_pos = jnp.sum(c_pos8, axis=0, keepdims=True)
    c_nn = jnp.sum(c_nn8, axis=0, keepdims=True)
    hi0 = jnp.max(mx8, axis=0, keepdims=True)
    mn = jnp.min(mn8, axis=0, keepdims=True)

    @pl.when((n_tiles & 1) == 1)
    def _():
        sc_ref[pl.ds(pl.multiple_of(n_tiles * TK, TK), TK), :] = jnp.full((TK, TQ), NEG_BIG, F32)

    n_pairs = (n_tiles + 1) >> 1

    def count_gt(thr):
        def body(kp, cnt):
            r0 = pl.multiple_of(kp * (2 * TK), 2 * TK)
            hit = jnp.where(sc_ref[pl.ds(r0, 2 * TK), :] > thr, 1, 0).astype(I32)
            return cnt + jnp.sum(hit.reshape(2 * TK // 32, 32, TQ), axis=0)
        cnt32 = lax.fori_loop(1, n_pairs, body, body(0, jnp.zeros((32, TQ), I32)))
        return jnp.sum(cnt32, axis=0, keepdims=True)

    t_glob = j * TQ + lax.broadcasted_iota(I32, (1, TQ), 1)
    k_eff = jnp.minimum(TOPK_MAX, t_glob + 1)

    zero = jnp.zeros((1, TQ), F32)
    all_in = (t_glob + 1) <= TOPK_MAX
    pos_ok = c_pos >= k_eff
    tie0 = (~all_in) & (~pos_ok) & (c_nn >= k_eff)
    lo_neg = jnp.maximum(mn + mn - 1.0, NEG_BIG)
    lo = jnp.where(all_in, NEG_BIG, jnp.where(pos_ok, zero, lo_neg))
    hi = jnp.where((~all_in) & (~pos_ok), zero, hi0)
    fin0 = all_in | (pos_ok & (c_pos == k_eff)) | tie0

    def bis_cond(st):
        it, go, _, _, _, _ = st
        return (it < BISECT_MAX_ITERS) & (go > 0)

    def bis_body(st):
        it, _, lo_, hi_, fin_, tie_ = st
        go_n = (jnp.min(fin_) == 0.0).astype(I32)
        mid = 0.5 * lo_ + 0.5 * hi_
        collapsed = (mid <= lo_) | (mid >= hi_)
        cm = count_gt(mid)
        act = fin_ == 0.0
        up = act & (~collapsed) & (cm >= k_eff)
        dn = act & (~collapsed) & (cm < k_eff)
        lo_n = jnp.where(up, mid, lo_)
        hi_n = jnp.where(dn, mid, hi_)
        tie_n = jnp.where(act & collapsed, 1.0, tie_)
        fin_n = jnp.where((act & collapsed) | (up & (cm == k_eff)), 1.0, fin_)
        return it + 1, go_n, lo_n, hi_n, fin_n, tie_n

    _, _, lo, hi, _, tie_f = lax.while_loop(
        bis_cond, bis_body,
        (jnp.int32(0), jnp.int32(1), lo, hi, jnp.where(fin0, 1.0, 0.0), jnp.where(tie0, 1.0, 0.0)))
    tie = tie_f > 0.0
    theta_ref = m_ref
    theta_ref[0:1, :] = lo

    @pl.when(jnp.max(tie_f) > 0.0)
    def _():
        v = hi

        def pass_a(kb, carry):
            n_gt, below = carry
            r0 = pl.multiple_of(kb * TK, TK)
            sc = sc_ref[pl.ds(r0, TK), :]
            n_gt = n_gt + jnp.sum(jnp.where(sc > v, 1.0, 0.0), axis=0, keepdims=True)
            below = jnp.maximum(below, jnp.max(jnp.where(sc < v, sc, NEG_BIG), axis=0, keepdims=True))
            return n_gt, below

        n_gt, below = lax.fori_loop(
            0, n_tiles, pass_a, (jnp.zeros((1, TQ), F32), jnp.full((1, TQ), NEG_BIG, F32)))
        need = k_eff.astype(F32) - n_gt
        tri = jnp.where(lax.broadcasted_iota(I32, (TK, TK), 1) < lax.broadcasted_iota(I32, (TK, TK), 0),
                        1.0, 0.0).astype(BF16)

        def pass_b(kb, seen):
            r0 = pl.multiple_of(kb * TK, TK)
            sc = sc_ref[pl.ds(r0, TK), :]
            eq = (sc == v) & tie
            eqf = jnp.where(eq, 1.0, 0.0)
            rank = seen + _dot(tri, eqf.astype(BF16))
            sc_ref[pl.ds(r0, TK), :] = jnp.where(eq & (rank >= need), NEG_BIG, sc)
            return seen + jnp.sum(eqf, axis=0, keepdims=True)

        lax.fori_loop(0, n_tiles, pass_b, jnp.zeros((1, TQ), F32))
        theta_ref[0:1, :] = jnp.where(tie, below, lo)

    theta = theta_ref[0:1, :]

    m_ref[...] = jnp.full(m_ref.shape, LOGIT_NEG, F32)
    l_ref[...] = jnp.zeros(l_ref.shape, F32)
    acc_ref[...] = jnp.zeros(acc_ref.shape, F32)

    def logits_stage(kb, slot, near):
        r0 = pl.multiple_of(kb * TK, TK)
        mb = jnp.where(sc_ref[pl.ds(r0, TK), :] > theta, 0.0, LOGIT_NEG)
        for h in range(H):
            p0 = (h // 2) * 2 * DH
            lg = _dot_nt(k_ref[pl.ds(r0, TK), p0:p0 + 2 * DH], qm_ref[h]) + mb
            if near is not None:
                lg = lg + bias_ref[near, h]
            lg_ref[slot, h] = lg
            m_old = m_ref[h:h + 1, :]
            m_new = jnp.maximum(m_old, jnp.max(lg, axis=0, keepdims=True))
            alpha_ref[slot, h:h + 1, :] = jnp.exp2(m_old - m_new)
            mtile_ref[slot, h:h + 1, :] = m_new
            m_ref[h:h + 1, :] = m_new

    def exp_stage(slot):
        for h in range(H):
            p_ref[slot, h] = jnp.exp2(lg_ref[slot, h] - mtile_ref[slot, h:h + 1, :]).astype(BF16)

    def pv_stage(kb, slot):
        r0 = pl.multiple_of(kb * TK, TK)
        ones = jnp.ones((ONES_ROWS, TK), BF16)
        for h in range(H):
            lhs = jnp.concatenate([vt_ref[h * DH:(h + 1) * DH, pl.ds(r0, TK)], ones], axis=0)
            pv = _dot(lhs, p_ref[slot, h])
            alpha = alpha_ref[slot, h:h + 1, :]
            acc_ref[h * DH:(h + 1) * DH, :] = alpha * acc_ref[h * DH:(h + 1) * DH, :] + pv[0:DH, :]
            l_ref[h:h + 1, :] = alpha * l_ref[h:h + 1, :] + pv[DH:DH + 1, :]

    @pl.when(j >= 2)
    def _():
        logits_stage(0, 0, None)

        def far_loop(i, carry):
            exp_stage((i - 1) & 1)
            pv_stage(i - 1, (i - 1) & 1)
            logits_stage(i, i & 1, None)
            return carry

        lax.fori_loop(1, j - 1, far_loop, 0)
        exp_stage((j - 2) & 1)
        pv_stage(j - 2, (j - 2) & 1)

    @pl.when(j >= 1)
    def _():
        logits_stage(j - 1, 0, 0)
        exp_stage(0)
        pv_stage(j - 1, 0)

    logits_stage(j, 0, 1)
    exp_stage(0)
    pv_stage(j, 0)

    for h in range(H):
        acc_ref[h * DH:(h + 1) * DH, :] = acc_ref[h * DH:(h + 1) * DH, :] / l_ref[h:h + 1, :]
    out_ref[...] = (acc_ref[...].T * az_ref[...]).astype(BF16)


def _dsa(rel_bias, iq, sm, aq, az, ik2, ak, avt, batch, seq):
    TQ = Q_TILE
    nq = seq // TQ
    width = A_HEADS * A_DH
    maps = jnp.asarray(_bucket_maps())
    qrow = lambda b, j: (b * nq + j, 0)
    return pl.pallas_call(
        _dsa_body,
        grid=(batch, nq),
        in_specs=[
            pl.BlockSpec(memory_space=pltpu.SMEM),
            pl.BlockSpec((TQ, width), qrow),
            pl.BlockSpec((TQ, 128), qrow),
            pl.BlockSpec((TQ, width), qrow),
            pl.BlockSpec((TQ, width), qrow),
            pl.BlockSpec((seq, 128), lambda b, j: (b, 0), pipeline_mode=pl.Buffered(1)),
            pl.BlockSpec((seq, width), lambda b, j: (b, 0), pipeline_mode=pl.Buffered(1)),
            pl.BlockSpec((None, width, seq), lambda b, j: (b, 0, 0), pipeline_mode=pl.Buffered(1)),
            _resident(maps.shape),
        ],
        out_specs=pl.BlockSpec((TQ, width), qrow),
        out_shape=jax.ShapeDtypeStruct((batch * seq, width), BF16),
        scratch_shapes=[
            pltpu.VMEM((seq, TQ), F32),
            pltpu.VMEM((A_HEADS, TQ, 2 * A_DH), BF16),
            pltpu.VMEM((A_HEADS, TQ, 2 * A_DH), BF16),
            pltpu.VMEM((2, A_HEADS, K_TILE, TQ), F32),
            pltpu.VMEM((A_HEADS * A_DH, TQ), F32),
            pltpu.VMEM((8, TQ), F32),
            pltpu.VMEM((8, TQ), F32),
            pltpu.VMEM((2, 8, TQ), F32),
            pltpu.VMEM((2, 8, TQ), F32),
            pltpu.VMEM((2, A_HEADS, K_TILE, TQ), F32),
            pltpu.VMEM((2, A_HEADS, K_TILE, TQ), BF16),
        ],
        compiler_params=pltpu.CompilerParams(
            dimension_semantics=("arbitrary", "arbitrary"), vmem_limit_bytes=VMEM_LIMIT),
        name="dsa",
    )(rel_bias, iq, sm, aq, az, ik2, ak, avt, maps)


def _outproj_body(x_ref, hm_ref, ha_ref, p_ref, wom_ref, woa_ref, wple_ref, wg_ref, g_ref, o_ref):
    h1 = x_ref[...] + _dot(hm_ref[...], wom_ref[...]) + _dot(ha_ref[...], woa_ref[...])
    gate = jax.nn.sigmoid(_dot(h1.astype(BF16), wg_ref[...]))
    h2 = h1 + _dot(p_ref[...].astype(BF16), wple_ref[...]) * gate
    ms = jnp.mean(h2 * h2, axis=-1, keepdims=True)
    o_ref[...] = h2 * lax.rsqrt(ms + EPS) * g_ref[...]


def _outproj(x2, hm, ha, p2, wom, woa, wple, wg, g):
    rows, d = x2.shape
    tm = ROW_TILE
    tile = lambda n: pl.BlockSpec((tm, n), lambda i: (i, 0))
    return pl.pallas_call(
        _outproj_body,
        grid=(rows // tm,),
        in_specs=[tile(d), tile(hm.shape[1]), tile(ha.shape[1]), tile(p2.shape[1]),
                  _resident(wom.shape), _resident(woa.shape), _resident(wple.shape),
                  _resident(wg.shape), _resident(g.shape)],
        out_specs=tile(d),
        out_shape=jax.ShapeDtypeStruct((rows, d), F32),
        compiler_params=pltpu.CompilerParams(
            dimension_semantics=("arbitrary",), vmem_limit_bytes=VMEM_LIMIT),
        name="outproj",
    )(x2, hm, ha, p2, wom, woa, wple, wg, g)


def kernel(x, p, w_in, b_gate, conv_w, w_out, norm_in, m_norm, rel_bias, w_ple, w_ple_gate, norm_final):
    batch, seq, d = x.shape
    assert w_in.shape[0] == 1 and p.shape[0] == 1, "single-layer block"
    assert seq % Q_TILE == 0 and seq % M_CHUNK == 0 and (batch * seq) % ROW_TILE == 0
    rows = batch * seq
    x2 = x.reshape(rows, d)

    sizes = [M_HEADS * M_QK, M_HEADS * M_QK, M_HEADS * M_V, M_HEADS * M_V, M_HEADS * M_V, 2 * M_HEADS,
             A_HEADS * A_DH, A_HEADS * A_DH, A_HEADS * A_DH, A_HEADS * A_DH,
             IDX_HEADS * IDX_DH, IDX_DH, IDX_HEADS]
    offs = np.concatenate([[0], np.cumsum(sizes)])
    wi = w_in[0]
    col = lambda i: wi[:, int(offs[i]):int(offs[i + 1])]
    w_if, w_ixk, w_ixw = col(5), col(11), col(12)
    zeros = lambda n: jnp.zeros((d, n), wi.dtype)
    w_sm = jnp.concatenate([w_if[:, :M_HEADS], zeros(4), w_if[:, M_HEADS:], zeros(4), w_ixw, zeros(104)], axis=1)
    w_ik2 = jnp.concatenate([w_ixk, w_ixk], axis=1)
    ws = [jnp.concatenate([col(0), col(1)], axis=1), col(2), col(3), col(4), w_sm, w_ik2,
          col(6), col(7), col(8), col(9), col(10)]
    ws = [w.astype(BF16) for w in ws]
    bg = b_gate[0]
    bpad = jnp.concatenate([bg[:M_HEADS], jnp.zeros((4,), F32), bg[M_HEADS:], jnp.zeros((116,), F32)])[None, :]

    mqk, mv, gate, sm, ik2, aq, ak, avt, az, iq = _inproj(x2, norm_in[0][None, :], ws, batch, seq)

    hm = _mlstm(mqk, mv, gate, sm, conv_w[0], bpad, m_norm[0][None, :], batch, seq)

    ha = _dsa(rel_bias, iq, sm, aq, az, ik2, ak, avt, batch, seq)

    wo = w_out[0].astype(BF16)
    out = _outproj(x2, hm, ha, p[0].reshape(rows, -1), wo[:M_HEADS * M_V], wo[M_HEADS * M_V:],
                   w_ple[0].astype(BF16), w_ple_gate[0].astype(BF16), norm_final[None, :])
    return out.reshape(batch, seq, d)
```

```python
import functools
import math

import numpy as np
import jax
import jax.numpy as jnp
from jax import lax
from jax.experimental import pallas as pl
from jax.experimental.pallas import tpu as pltpu

F32 = jnp.float32
BF16 = jnp.bfloat16
I32 = jnp.int32

M_HEADS = 4
M_QK = 128
M_V = 256
CONV_W = 4
A_HEADS = 8
A_DH = 64
IDX_HEADS = 8
IDX_DH = 64
TOPK_MAX = 256
REL_BUCKETS = 32
REL_MAX_DIST = 128
EPS = 1e-6

LOG2E = 1.4426950408889634
NEG_BIG = -3.0e38
LOGIT_NEG = -1.0e30

ROW_TILE = 512
M_CHUNK = 256
Q_TILE = 256
K_TILE = 256
ONES_ROWS = 16
BISECT_MAX_ITERS = 256
COARSE_ITERS = 11
VMEM_LIMIT = 56 * 1024 * 1024


def _silu(v):
    return v * jax.nn.sigmoid(v)


def _dot(a, b):
    return jnp.dot(a, b, preferred_element_type=F32)


def _dot_nt(a, b):
    return lax.dot_general(a, b, (((1,), (1,)), ((), ())), preferred_element_type=F32)


def _dot_tn(a, b):
    return lax.dot_general(a, b, (((0,), (0,)), ((), ())), preferred_element_type=F32)


def _resident(shape):
    nd = len(shape)
    return pl.BlockSpec(shape, lambda *_: (0,) * nd, pipeline_mode=pl.Buffered(1))


def _inproj_body(x_ref, g_ref, w_mqk, w_mv, w_mo, w_mz, w_sm, w_ik, w_aq, w_ak, w_av, w_az, w_iq,
                 o_mqk, o_mv, o_gate, o_sm, o_ik, o_aq, o_ak, o_avt, o_az, o_iq):
    x = x_ref[...]
    ms = jnp.mean(x * x, axis=-1, keepdims=True)
    xn = (x * lax.rsqrt(ms + EPS) * g_ref[...]).astype(BF16)
    o_mqk[...] = _dot(xn, w_mqk[...])
    o_mv[...] = _dot(xn, w_mv[...]).astype(BF16)
    o_gate[...] = jax.nn.sigmoid(_dot(xn, w_mo[...])) * _silu(_dot(xn, w_mz[...]))
    sm = _dot(xn, w_sm[...])
    o_sm[...] = sm
    o_ik[...] = _dot(xn, w_ik[...]).astype(BF16)
    o_aq[...] = (_dot(xn, w_aq[...]) * (A_DH ** -0.5 * LOG2E)).astype(BF16)
    o_ak[...] = _dot(xn, w_ak[...]).astype(BF16)
    o_avt[...] = _dot(xn, w_av[...]).T.astype(BF16)
    o_az[...] = _silu(_dot(xn, w_az[...]))
    iq = _dot(xn, w_iq[...])
    wsc = sm[:, 16:16 + IDX_HEADS] * (IDX_HEADS ** -0.5 * IDX_DH ** -0.5)
    lane = lax.broadcasted_iota(I32, (x.shape[0], 2 * IDX_DH), 1)
    for pair in range(IDX_HEADS // 2):
        wpair = jnp.where(lane < IDX_DH, wsc[:, 2 * pair:2 * pair + 1], wsc[:, 2 * pair + 1:2 * pair + 2])
        sl = slice(pair * 2 * IDX_DH, (pair + 1) * 2 * IDX_DH)
        o_iq[:, sl] = (iq[:, sl] * wpair).astype(BF16)


def _inproj(x2, g, ws, batch, seq):
    rows, d = x2.shape
    tm = ROW_TILE
    per_seq = seq // tm
    out_dtypes = [F32, BF16, F32, F32, BF16, BF16, BF16, BF16, F32, BF16]
    widths = [w.shape[1] for w in ws]
    out_widths = [widths[0], widths[1], widths[2]] + widths[4:]
    in_specs = [pl.BlockSpec((tm, d), lambda i: (i, 0)), _resident((1, d))]
    in_specs += [_resident(w.shape) for w in ws]
    out_specs = [pl.BlockSpec((tm, n), lambda i: (i, 0)) for n in out_widths]
    out_shape = [jax.ShapeDtypeStruct((rows, n), dt) for n, dt in zip(out_widths, out_dtypes)]
    av_pos = 7
    out_specs[av_pos] = pl.BlockSpec((None, out_widths[av_pos], tm), lambda i: (i // per_seq, 0, i % per_seq))
    out_shape[av_pos] = jax.ShapeDtypeStruct((batch, out_widths[av_pos], seq), BF16)
    return pl.pallas_call(
        _inproj_body,
        grid=(rows // tm,),
        in_specs=in_specs,
        out_specs=out_specs,
        out_shape=out_shape,
        compiler_params=pltpu.CompilerParams(
            dimension_semantics=("arbitrary",), vmem_limit_bytes=VMEM_LIMIT),
        name="inproj",
    )(x2, g, *ws)


def _lane_scan(v, op, fill):
    n = v.shape[1]
    lane = lax.broadcasted_iota(I32, v.shape, 1)
    sh = 1
    while sh < n:
        v = op(v, jnp.where(lane >= sh, pltpu.roll(v, sh, 1), fill))
        sh *= 2
    return v


def _mlstm_body(mqk_ref, mv_ref, gate_ref, sm_ref, convw_ref, bpad_ref, mnorm_ref, out_ref,
                ext_ref, c_ref, n_ref, m_ref):
    L = M_CHUNK
    H, DK, DV = M_HEADS, M_QK, M_V
    HALO = 8

    @pl.when(pl.program_id(1) == 0)
    def _():
        ext_ref[0:HALO, :] = jnp.zeros((HALO, 2 * H * DK), F32)
        c_ref[...] = jnp.zeros(c_ref.shape, F32)
        n_ref[...] = jnp.zeros(n_ref.shape, F32)
        m_ref[...] = jnp.zeros(m_ref.shape, F32)

    raw = mqk_ref[...]
    ext_ref[HALO:HALO + L, :] = raw
    base = HALO - (CONV_W - 1)
    y = ext_ref[base:base + L, :] * convw_ref[0:1, :]
    for j in range(1, CONV_W):
        y = y + ext_ref[base + j:base + j + L, :] * convw_ref[j:j + 1, :]
    ext_ref[0:HALO, :] = raw[L - HALO:L, :]
    qk = _silu(y)

    gt = (sm_ref[...] + bpad_ref[...]).T
    li = gt[0:8, :]
    lf = jax.nn.log_sigmoid(gt[8:16, :])
    b = _lane_scan(lf, jnp.add, 0.0)
    a = li - b
    m_prev = m_ref[:, 0:1]
    mrow = jnp.maximum(m_prev, _lane_scan(a, jnp.maximum, -jnp.inf))
    w_inter = jnp.exp(m_prev - mrow)
    e_neg_mt = jnp.exp(-(b + mrow))
    m_last = mrow[:, L - 1:L]
    wk = jnp.exp(a - m_last)
    decay = jnp.exp(m_prev - m_last)
    m_new = b[:, L - 1:L] + m_last
    stack = jnp.concatenate([mrow, w_inter, e_neg_mt, wk, jnp.zeros((128 - 32, L), F32)], axis=0)
    cols = stack.T

    ri = lax.broadcasted_iota(I32, (L, L), 0)
    ci = lax.broadcasted_iota(I32, (L, L), 1)
    causal = ci <= ri

    for h in range(H):
        q = qk[:, h * DK:(h + 1) * DK]
        k = qk[:, (H + h) * DK:(H + h + 1) * DK] * (DK ** -0.5)
        qb = q.astype(BF16)
        v = mv_ref[:, h * DV:(h + 1) * DV]
        m_col = cols[:, h:h + 1]
        wi_col = cols[:, 8 + h:9 + h]
        emt_col = cols[:, 16 + h:17 + h]
        wk_col = cols[:, 24 + h:25 + h]
        e = jnp.where(causal, jnp.exp(a[h:h + 1, :] - m_col), 0.0)
        s = _dot_nt(qb, k.astype(BF16)) * e
        c_old = c_ref[h]
        num = _dot(s.astype(BF16), v) + wi_col * _dot(qb, c_old.astype(BF16))
        den = (jnp.sum(s, axis=1, keepdims=True)
               + wi_col * jnp.sum(q * n_ref[h:h + 1, :], axis=1, keepdims=True))
        den = jnp.maximum(jnp.abs(den), emt_col)
        hh = num / den
        hn = hh * lax.rsqrt(jnp.mean(hh * hh, axis=-1, keepdims=True) + EPS)
        hn = hn * mnorm_ref[:, h * DV:(h + 1) * DV]
        out_ref[:, h * DV:(h + 1) * DV] = (gate_ref[:, h * DV:(h + 1) * DV] * hn).astype(BF16)
        kw = k * wk_col
        dec = decay[h:h + 1, :]
        c_ref[h] = dec * c_old + _dot_tn(kw.astype(BF16), v)
        n_ref[h:h + 1, :] = dec * n_ref[h:h + 1, :] + jnp.sum(kw, axis=0, keepdims=True)
    m_ref[...] = jnp.broadcast_to(m_new, m_ref.shape)


def _mlstm(mqk, mv, gate, sm, conv_w, bpad, mnorm, batch, seq):
    L = M_CHUNK
    nc = seq // L
    wqk = mqk.shape[1]
    wv = mv.shape[1]
    row = lambda b, c: (b * nc + c, 0)
    return pl.pallas_call(
        _mlstm_body,
        grid=(batch, nc),
        in_specs=[
            pl.BlockSpec((L, wqk), row),
            pl.BlockSpec((L, wv), row),
            pl.BlockSpec((L, wv), row),
            pl.BlockSpec((L, 128), row),
            _resident(conv_w.shape),
            _resident(bpad.shape),
            _resident(mnorm.shape),
        ],
        out_specs=pl.BlockSpec((L, wv), row),
        out_shape=jax.ShapeDtypeStruct((batch * seq, wv), BF16),
        scratch_shapes=[
            pltpu.VMEM((L + 8, wqk), F32),
            pltpu.VMEM((M_HEADS, M_QK, M_V), F32),
            pltpu.VMEM((8, M_QK), F32),
            pltpu.VMEM((8, 128), F32),
        ],
        compiler_params=pltpu.CompilerParams(
            dimension_semantics=("arbitrary", "arbitrary"), vmem_limit_bytes=VMEM_LIMIT),
        name="mlstm",
    )(mqk, mv, gate, sm, conv_w, bpad, mnorm)


def _t5_bucket_table(max_dist):
    d = np.arange(max_dist)
    max_exact = REL_BUCKETS // 2
    tabs = []
    for dt in (np.float32, np.float64):
        ratio = np.maximum(d, 1).astype(dt) / dt(max_exact)
        large = max_exact + (np.log(ratio) / dt(math.log(REL_MAX_DIST / max_exact))
                             * dt(REL_BUCKETS - max_exact)).astype(np.int32)
        large = np.minimum(large, REL_BUCKETS - 1)
        tabs.append(np.where(d < max_exact, d, large).astype(np.int32))
    assert np.array_equal(tabs[0], tabs[1])
    return tabs[0]


def _bucket_maps():
    tab = _t5_bucket_table(Q_TILE + K_TILE)
    s = np.arange(K_TILE)[:, None]
    t = np.arange(Q_TILE)[None, :]
    diag = tab[np.maximum(t - s, 0)]
    prev = tab[K_TILE + t - s]
    assert np.all(tab[K_TILE:] == REL_BUCKETS - 1)
    return np.stack([prev, diag]).astype(np.int32)


def _dsa_body(rb_ref, iq_ref, sm_ref, aq_ref, az_ref, ik_ref, k_ref, vt_ref, map_ref, out_ref,
              sc_ref, rs_ref, iqm_ref, qm_ref, bias_ref, acc_ref, m_ref, l_ref, alpha_ref, mtile_ref, lg_ref,
              p_ref):
    TQ, TK, H, DH = Q_TILE, K_TILE, A_HEADS, A_DH
    j = pl.program_id(1)
    n_tiles = j + 1

    @pl.when((pl.program_id(0) == 0) & (j == 0))
    def _():
        bias_ref[...] = jnp.zeros(bias_ref.shape, F32)

        def fill(bkt, carry):
            for which in range(2):
                hit = map_ref[which] == bkt
                for h in range(H):
                    val = (rb_ref[bkt, h] - rb_ref[REL_BUCKETS - 1, h]) * LOG2E
                    bias_ref[which, h] = jnp.where(hit, val, bias_ref[which, h])
            return carry

        lax.fori_loop(0, REL_BUCKETS - 1, fill, 0)

    lane = lax.broadcasted_iota(I32, (TQ, 2 * DH), 1)
    for h in range(H):
        p0 = (h // 2) * 2 * DH
        keep = (lane < DH) if h % 2 == 0 else (lane >= DH)
        iqm_ref[h] = jnp.where(keep, iq_ref[:, p0:p0 + 2 * DH], jnp.zeros((), BF16))
        qm_ref[h] = jnp.where(keep, aq_ref[:, p0:p0 + 2 * DH], jnp.zeros((), BF16))
    w_rows = sm_ref[...].T[16:16 + IDX_HEADS, :]
    clamp_lo = jnp.where(w_rows >= 0.0, 0.0, -jnp.inf)
    clamp_hi = jnp.where(w_rows >= 0.0, jnp.inf, 0.0)

    s_loc = lax.broadcasted_iota(I32, (TK, TQ), 0)
    t_loc = lax.broadcasted_iota(I32, (TK, TQ), 1)

    fold = lambda v: v.reshape(v.shape[0] // 8, 8, TQ)

    def index_tile(kb, diag, stats):
        r0 = pl.multiple_of(kb * TK, TK)
        lhs = ik_ref[pl.ds(r0, TK), :]
        sc = None
        for h in range(H):
            y = _dot_nt(lhs, iqm_ref[h])
            term = jnp.minimum(jnp.maximum(y, clamp_lo[h:h + 1, :]), clamp_hi[h:h + 1, :])
            sc = term if sc is None else sc + term
        sc_min = sc
        if diag:
            admissible = s_loc <= t_loc
            sc_min = jnp.where(admissible, sc, -NEG_BIG)
            sc = jnp.where(admissible, sc, NEG_BIG)
        sc_ref[pl.ds(r0, TK), :] = sc
        rs_ref[pl.ds(r0, TK), :] = sc.astype(BF16)
        c_pos_, c_nn_, mx_, mn_ = stats
        c_pos_ = c_pos_ + jnp.sum(fold(jnp.where(sc > 0.0, 1, 0).astype(I32)), axis=0)
        c_nn_ = c_nn_ + jnp.sum(fold(jnp.where(sc >= 0.0, 1, 0).astype(I32)), axis=0)
        mx_ = jnp.maximum(mx_, jnp.max(fold(sc), axis=0))
        mn_ = jnp.minimum(mn_, jnp.min(fold(sc_min), axis=0))
        return c_pos_, c_nn_, mx_, mn_

    stats0 = (jnp.zeros((8, TQ), I32), jnp.zeros((8, TQ), I32),
              jnp.full((8, TQ), NEG_BIG, F32), jnp.full((8, TQ), -NEG_BIG, F32))
    stats1 = lax.fori_loop(0, j, lambda kb, st: index_tile(kb, False, st), stats0)
    c_pos8, c_nn8, mx8, mn8 = index_tile(j, True, stats1)
    c_pos = jnp.sum(c_pos8, axis=0, keepdims=True)
    c_nn = jnp.sum(c_nn8, axis=0, keepdims=True)
    hi0 = jnp.max(mx8, axis=0, keepdims=True)
    mn = jnp.min(mn8, axis=0, keepdims=True)

    @pl.when((n_tiles & 1) == 1)
    def _():
        sc_ref[pl.ds(pl.multiple_of(n_tiles * TK, TK), TK), :] = jnp.full((TK, TQ), NEG_BIG, F32)
        rs_ref[pl.ds(pl.multiple_of(n_tiles * TK, TK), TK), :] = jnp.full((TK, TQ), NEG_BIG, BF16)

    n_pairs = (n_tiles + 1) >> 1

    def count_gt(thr):
        def body(kp, cnt):
            r0 = pl.multiple_of(kp * (2 * TK), 2 * TK)
            hit = jnp.where(sc_ref[pl.ds(r0, 2 * TK), :] > thr, 1, 0).astype(I32)
            return cnt + jnp.sum(hit.reshape(2 * TK // 32, 32, TQ), axis=0)
        cnt32 = lax.fori_loop(1, n_pairs, body, body(0, jnp.zeros((32, TQ), I32)))
        return jnp.sum(cnt32, axis=0, keepdims=True)

    t_glob = j * TQ + lax.broadcasted_iota(I32, (1, TQ), 1)
    k_eff = jnp.minimum(TOPK_MAX, t_glob + 1)

    zero = jnp.zeros((1, TQ), F32)
    all_in = (t_glob + 1) <= TOPK_MAX
    pos_ok = c_pos >= k_eff
    tie0 = (~all_in) & (~pos_ok) & (c_nn >= k_eff)
    lo_neg = jnp.maximum(mn + mn - 1.0, NEG_BIG)
    lo = jnp.where(all_in, NEG_BIG, jnp.where(pos_ok, zero, lo_neg))
    hi = jnp.where((~all_in) & (~pos_ok), zero, hi0)
    fin0 = all_in | (pos_ok & (c_pos == k_eff)) | tie0

    k_f = k_eff.astype(F32)
    one_b = jnp.ones((), BF16)
    zero_b = jnp.zeros((), BF16)

    def count_gt_coarse(thr_b):
        def body(kp, cnt):
            r0 = pl.multiple_of(kp * (2 * TK), 2 * TK)
            hit = jnp.where(rs_ref[pl.ds(r0, 2 * TK), :] > thr_b, one_b, zero_b)
            parts = [hit[g * 32:(g + 1) * 32, :] for g in range(2 * TK // 32)]
            while len(parts) > 1:
                parts = [parts[i] + parts[i + 1] for i in range(0, len(parts), 2)]
            return cnt + parts[0].astype(F32)
        cnt32 = lax.fori_loop(0, n_pairs, body, jnp.zeros((32, TQ), F32))
        return jnp.sum(cnt32, axis=0, keepdims=True)

    def coarse_body(_, st):
        lo_c, hi_c = st
        mid_b = (0.5 * lo_c + 0.5 * hi_c).astype(BF16)
        mid = mid_b.astype(F32)
        inside = (~fin0) & (mid > lo_c) & (mid < hi_c)
        cm = count_gt_coarse(mid_b)
        return (jnp.where(inside & (cm >= k_f), mid, lo_c), jnp.where(inside & (cm < k_f), mid, hi_c))

    lo_c, hi_c = lax.fori_loop(0, COARSE_ITERS, coarse_body, (lo, hi))

    def count_two(thr_a, thr_b):
        def body(kp, carry):
            cnt_a, cnt_b = carry
            r0 = pl.multiple_of(kp * (2 * TK), 2 * TK)
            sc = sc_ref[pl.ds(r0, 2 * TK), :]
            part = lambda thr: jnp.sum(jnp.where(sc > thr, 1, 0).astype(I32).reshape(2 * TK // 32, 32, TQ), axis=0)
            return cnt_a + part(thr_a), cnt_b + part(thr_b)
        z = jnp.zeros((32, TQ), I32)
        cnt_a, cnt_b = lax.fori_loop(0, n_pairs, body, (z, z))
        return jnp.sum(cnt_a, axis=0, keepdims=True), jnp.sum(cnt_b, axis=0, keepdims=True)

    c_a, c_b = count_two(lo_c, hi_c)
    act0 = ~fin0
    a_low = act0 & (c_a >= k_eff)
    b_low = act0 & (c_b >= k_eff)
    hi = jnp.where(act0 & (~a_low), lo_c, jnp.where(act0 & (~b_low), jnp.minimum(hi, hi_c), hi))
    lo = jnp.where(b_low, hi_c, jnp.where(a_low, lo_c, lo))
    fin0 = fin0 | (b_low & (c_b == k_eff)) | (a_low & (~b_low) & (c_a == k_eff))

    def bis_cond(st):
        it, go, _, _, _, _ = st
        return (it < BISECT_MAX_ITERS) & (go > 0)

    def bis_body(st):
        it, _, lo_, hi_, fin_, tie_ = st
        go_n = (jnp.min(fin_) == 0.0).astype(I32)
        mid = 0.5 * lo_ + 0.5 * hi_
        collapsed = (mid <= lo_) | (mid >= hi_)
        cm = count_gt(mid)
        act = fin_ == 0.0
        up = act & (~collapsed) & (cm >= k_eff)
        dn = act & (~collapsed) & (cm < k_eff)
        lo_n = jnp.where(up, mid, lo_)
        hi_n = jnp.where(dn, mid, hi_)
        tie_n = jnp.where(act & collapsed, 1.0, tie_)
        fin_n = jnp.where((act & collapsed) | (up & (cm == k_eff)), 1.0, fin_)
        return it + 1, go_n, lo_n, hi_n, fin_n, tie_n

    _, _, lo, hi, _, tie_f = lax.while_loop(
        bis_cond, bis_body,
        (jnp.int32(0), jnp.int32(1), lo, hi, jnp.where(fin0, 1.0, 0.0), jnp.where(tie0, 1.0, 0.0)))
    tie = tie_f > 0.0
    theta_ref = m_ref
    theta_ref[0:1, :] = lo

    @pl.when(jnp.max(tie_f) > 0.0)
    def _():
        v = hi

        def pass_a(kb, carry):
            n_gt, below = carry
            r0 = pl.multiple_of(kb * TK, TK)
            sc = sc_ref[pl.ds(r0, TK), :]
            n_gt = n_gt + jnp.sum(jnp.where(sc > v, 1.0, 0.0), axis=0, keepdims=True)
            below = jnp.maximum(below, jnp.max(jnp.where(sc < v, sc, NEG_BIG), axis=0, keepdims=True))
            return n_gt, below

        n_gt, below = lax.fori_loop(
            0, n_tiles, pass_a, (jnp.zeros((1, TQ), F32), jnp.full((1, TQ), NEG_BIG, F32)))
        need = k_eff.astype(F32) - n_gt
        tri = jnp.where(lax.broadcasted_iota(I32, (TK, TK), 1) < lax.broadcasted_iota(I32, (TK, TK), 0),
                        1.0, 0.0).astype(BF16)

        def pass_b(kb, seen):
            r0 = pl.multiple_of(kb * TK, TK)
            sc = sc_ref[pl.ds(r0, TK), :]
            eq = (sc == v) & tie
            eqf = jnp.where(eq, 1.0, 0.0)
            rank = seen + _dot(tri, eqf.astype(BF16))
            sc_ref[pl.ds(r0, TK), :] = jnp.where(eq & (rank >= need), NEG_BIG, sc)
            return seen + jnp.sum(eqf, axis=0, keepdims=True)

        lax.fori_loop(0, n_tiles, pass_b, jnp.zeros((1, TQ), F32))
        theta_ref[0:1, :] = jnp.where(tie, below, lo)

    theta = theta_ref[0:1, :]

    m_ref[...] = jnp.full(m_ref.shape, LOGIT_NEG, F32)
    l_ref[...] = jnp.zeros(l_ref.shape, F32)
    acc_ref[...] = jnp.zeros(acc_ref.shape, F32)

    def tile_mask(kb):
        r0 = pl.multiple_of(kb * TK, TK)
        return jnp.where(sc_ref[pl.ds(r0, TK), :] > theta, 0.0, LOGIT_NEG)

    def logits_head(kb, slot, near, mb, h):
        r0 = pl.multiple_of(kb * TK, TK)
        p0 = (h // 2) * 2 * DH
        lg = _dot_nt(k_ref[pl.ds(r0, TK), p0:p0 + 2 * DH], qm_ref[h]) + mb
        if near is not None:
            lg = lg + bias_ref[near, h]
        lg_ref[slot, h] = lg
        m_old = m_ref[h:h + 1, :]
        m_new = jnp.maximum(m_old, jnp.max(lg, axis=0, keepdims=True))
        alpha_ref[slot, h:h + 1, :] = jnp.exp2(m_old - m_new)
        mtile_ref[slot, h:h + 1, :] = m_new
        m_ref[h:h + 1, :] = m_new

    def exp_head(slot, h):
        p_ref[slot, h] = jnp.exp2(lg_ref[slot, h] - mtile_ref[slot, h:h + 1, :]).astype(BF16)

    def pv_head(kb, slot, h):
        r0 = pl.multiple_of(kb * TK, TK)
        ones = jnp.ones((ONES_ROWS, TK), BF16)
        lhs = jnp.concatenate([vt_ref[h * DH:(h + 1) * DH, pl.ds(r0, TK)], ones], axis=0)
        pv = _dot(lhs, p_ref[slot, h])
        alpha = alpha_ref[slot, h:h + 1, :]
        acc_ref[h * DH:(h + 1) * DH, :] = alpha * acc_ref[h * DH:(h + 1) * DH, :] + pv[0:DH, :]
        l_ref[h:h + 1, :] = alpha * l_ref[h:h + 1, :] + pv[DH:DH + 1, :]

    def fill(kb, slot, near):
        mb = tile_mask(kb)
        for h in range(H):
            logits_head(kb, slot, near, mb, h)

    def drain(kb, slot):
        for h in range(H):
            exp_head(slot, h)
        for h in range(H):
            pv_head(kb, slot, h)

    def step(kb_prev, slot_prev, kb_next, slot_next, near_next):
        mb = tile_mask(kb_next)
        for h in range(H):
            exp_head(slot_prev, h)
            logits_head(kb_next, slot_next, near_next, mb, h)
            pv_head(kb_prev, slot_prev, h)

    n_far = j - 1

    @pl.when(n_far >= 1)
    def _():
        fill(0, 0, None)

        def far_loop(t, carry):
            step(2 * t, 0, 2 * t + 1, 1, None)
            step(2 * t + 1, 1, 2 * t + 2, 0, None)
            return carry

        lax.fori_loop(0, (n_far - 1) >> 1, far_loop, 0)

        @pl.when(((n_far - 1) & 1) == 1)
        def _():
            step(n_far - 2, 0, n_far - 1, 1, None)
            drain(n_far - 1, 1)

        @pl.when(((n_far - 1) & 1) == 0)
        def _():
            drain(n_far - 1, 0)

    @pl.when(j >= 1)
    def _():
        fill(j - 1, 0, 0)
        step(j - 1, 0, j, 1, 1)
        drain(j, 1)

    @pl.when(j == 0)
    def _():
        fill(j, 0, 1)
        drain(j, 0)

    for h in range(H):
        acc_ref[h * DH:(h + 1) * DH, :] = acc_ref[h * DH:(h + 1) * DH, :] / l_ref[h:h + 1, :]
    out_ref[...] = (acc_ref[...].T * az_ref[...]).astype(BF16)


def _dsa(rel_bias, iq, sm, aq, az, ik2, ak, avt, batch, seq):
    TQ = Q_TILE
    nq = seq // TQ
    width = A_HEADS * A_DH
    maps = jnp.asarray(_bucket_maps())
    qrow = lambda b, j: (b * nq + j, 0)
    return pl.pallas_call(
        _dsa_body,
        grid=(batch, nq),
        in_specs=[
            pl.BlockSpec(memory_space=pltpu.SMEM),
            pl.BlockSpec((TQ, width), qrow),
            pl.BlockSpec((TQ, 128), qrow),
            pl.BlockSpec((TQ, width), qrow),
            pl.BlockSpec((TQ, width), qrow),
            pl.BlockSpec((seq, 128), lambda b, j: (b, 0), pipeline_mode=pl.Buffered(1)),
            pl.BlockSpec((seq, width), lambda b, j: (b, 0), pipeline_mode=pl.Buffered(1)),
            pl.BlockSpec((None, width, seq), lambda b, j: (b, 0, 0), pipeline_mode=pl.Buffered(1)),
            _resident(maps.shape),
        ],
        out_specs=pl.BlockSpec((TQ, width), qrow),
        out_shape=jax.ShapeDtypeStruct((batch * seq, width), BF16),
        scratch_shapes=[
            pltpu.VMEM((seq, TQ), F32),
            pltpu.VMEM((seq, TQ), BF16),
            pltpu.VMEM((A_HEADS, TQ, 2 * A_DH), BF16),
            pltpu.VMEM((A_HEADS, TQ, 2 * A_DH), BF16),
            pltpu.VMEM((2, A_HEADS, K_TILE, TQ), F32),
            pltpu.VMEM((A_HEADS * A_DH, TQ), F32),
            pltpu.VMEM((8, TQ), F32),
            pltpu.VMEM((8, TQ), F32),
            pltpu.VMEM((2, 8, TQ), F32),
            pltpu.VMEM((2, 8, TQ), F32),
            pltpu.VMEM((2, A_HEADS, K_TILE, TQ), F32),
            pltpu.VMEM((2, A_HEADS, K_TILE, TQ), BF16),
        ],
        compiler_params=pltpu.CompilerParams(
            dimension_semantics=("arbitrary", "arbitrary"), vmem_limit_bytes=VMEM_LIMIT),
        name="dsa",
    )(rel_bias, iq, sm, aq, az, ik2, ak, avt, maps)


def _outproj_body(x_ref, hm_ref, ha_ref, p_ref, wom_ref, woa_ref, wple_ref, wg_ref, g_ref, o_ref):
    h1 = x_ref[...] + _dot(hm_ref[...], wom_ref[...]) + _dot(ha_ref[...], woa_ref[...])
    gate = jax.nn.sigmoid(_dot(h1.astype(BF16), wg_ref[...]))
    h2 = h1 + _dot(p_ref[...].astype(BF16), wple_ref[...]) * gate
    ms = jnp.mean(h2 * h2, axis=-1, keepdims=True)
    o_ref[...] = h2 * lax.rsqrt(ms + EPS) * g_ref[...]


def _outproj(x2, hm, ha, p2, wom, woa, wple, wg, g):
    rows, d = x2.shape
    tm = ROW_TILE
    tile = lambda n: pl.BlockSpec((tm, n), lambda i: (i, 0))
    return pl.pallas_call(
        _outproj_body,
        grid=(rows // tm,),
        in_specs=[tile(d), tile(hm.shape[1]), tile(ha.shape[1]), tile(p2.shape[1]),
                  _resident(wom.shape), _resident(woa.shape), _resident(wple.shape),
                  _resident(wg.shape), _resident(g.shape)],
        out_specs=tile(d),
        out_shape=jax.ShapeDtypeStruct((rows, d), F32),
        compiler_params=pltpu.CompilerParams(
            dimension_semantics=("arbitrary",), vmem_limit_bytes=VMEM_LIMIT),
        name="outproj",
    )(x2, hm, ha, p2, wom, woa, wple, wg, g)


def kernel(x, p, w_in, b_gate, conv_w, w_out, norm_in, m_norm, rel_bias, w_ple, w_ple_gate, norm_final):
    batch, seq, d = x.shape
    assert w_in.shape[0] == 1 and p.shape[0] == 1, "single-layer block"
    assert seq % Q_TILE == 0 and seq % M_CHUNK == 0 and (batch * seq) % ROW_TILE == 0
    rows = batch * seq
    x2 = x.reshape(rows, d)

    sizes = [M_HEADS * M_QK, M_HEADS * M_QK, M_HEADS * M_V, M_HEADS * M_V, M_HEADS * M_V, 2 * M_HEADS,
             A_HEADS * A_DH, A_HEADS * A_DH, A_HEADS * A_DH, A_HEADS * A_DH,
             IDX_HEADS * IDX_DH, IDX_DH, IDX_HEADS]
    offs = np.concatenate([[0], np.cumsum(sizes)])
    wi = w_in[0]
    col = lambda i: wi[:, int(offs[i]):int(offs[i + 1])]
    w_if, w_ixk, w_ixw = col(5), col(11), col(12)
    zeros = lambda n: jnp.zeros((d, n), wi.dtype)
    w_sm = jnp.concatenate([w_if[:, :M_HEADS], zeros(4), w_if[:, M_HEADS:], zeros(4), w_ixw, zeros(104)], axis=1)
    w_ik2 = jnp.concatenate([w_ixk, w_ixk], axis=1)
    ws = [jnp.concatenate([col(0), col(1)], axis=1), col(2), col(3), col(4), w_sm, w_ik2,
          col(6), col(7), col(8), col(9), col(10)]
    ws = [w.astype(BF16) for w in ws]
    bg = b_gate[0]
    bpad = jnp.concatenate([bg[:M_HEADS], jnp.zeros((4,), F32), bg[M_HEADS:], jnp.zeros((116,), F32)])[None, :]

    mqk, mv, gate, sm, ik2, aq, ak, avt, az, iq = _inproj(x2, norm_in[0][None, :], ws, batch, seq)

    hm = _mlstm(mqk, mv, gate, sm, conv_w[0], bpad, m_norm[0][None, :], batch, seq)

    ha = _dsa(rel_bias, iq, sm, aq, az, ik2, ak, avt, batch, seq)

    wo = w_out[0].astype(BF16)
    out = _outproj(x2, hm, ha, p[0].reshape(rows, -1), wo[:M_HEADS * M_V], wo[M_HEADS * M_V:],
                   w_ple[0].astype(BF16), w_ple_gate[0].astype(BF16), norm_final[None, :])
    return out.reshape(batch, seq, d)
```

```python
import functools
import math

import numpy as np
import jax
import jax.numpy as jnp
from jax import lax
from jax.experimental import pallas as pl
from jax.experimental.pallas import tpu as pltpu

F32 = jnp.float32
BF16 = jnp.bfloat16
I32 = jnp.int32

M_HEADS = 4
M_QK = 128
M_V = 256
CONV_W = 4
A_HEADS = 8
A_DH = 64
IDX_HEADS = 8
IDX_DH = 64
TOPK_MAX = 256
REL_BUCKETS = 32
REL_MAX_DIST = 128
EPS = 1e-6

LOG2E = 1.4426950408889634
NEG_BIG = -3.0e38
LOGIT_NEG = -1.0e30

ROW_TILE = 512
M_CHUNK = 256
Q_TILE = 256
K_TILE = 256
ONES_ROWS = 16
CONV_HALO = 8
BISECT_MAX_ITERS = 256
VMEM_LIMIT = 56 * 1024 * 1024


def _silu(v):
    return v * jax.nn.sigmoid(v)


def _dot(a, b):
    return jnp.dot(a, b, preferred_element_type=F32)


def _dot_nt(a, b):
    return lax.dot_general(a, b, (((1,), (1,)), ((), ())), preferred_element_type=F32)


def _dot_tn(a, b):
    return lax.dot_general(a, b, (((0,), (0,)), ((), ())), preferred_element_type=F32)


def _resident(shape):
    nd = len(shape)
    return pl.BlockSpec(shape, lambda *_: (0,) * nd, pipeline_mode=pl.Buffered(1))


def _inproj_body(x_ref, g_ref, w_mqk, w_mv, w_mo, w_mz, w_sm, w_ik, w_aq, w_ak, w_av, w_az, w_iq,
                 o_mqk, o_mv, o_gate, o_sm, o_ik, o_aq, o_ak, o_avt, o_az, o_iq):
    x = x_ref[...]
    ms = jnp.mean(x * x, axis=-1, keepdims=True)
    xn = (x * lax.rsqrt(ms + EPS) * g_ref[...]).astype(BF16)
    o_mqk[...] = _dot(xn, w_mqk[...])
    o_mv[...] = _dot(xn, w_mv[...]).astype(BF16)
    o_gate[...] = jax.nn.sigmoid(_dot(xn, w_mo[...])) * _silu(_dot(xn, w_mz[...]))
    sm = _dot(xn, w_sm[...])
    o_sm[...] = sm
    o_ik[...] = _dot(xn, w_ik[...]).astype(BF16)
    o_aq[...] = (_dot(xn, w_aq[...]) * (A_DH ** -0.5 * LOG2E)).astype(BF16)
    o_ak[...] = _dot(xn, w_ak[...]).astype(BF16)
    o_avt[...] = _dot(xn, w_av[...]).T.astype(BF16)
    o_az[...] = _silu(_dot(xn, w_az[...]))
    iq = _dot(xn, w_iq[...])
    wsc = sm[:, 16:16 + IDX_HEADS] * (IDX_HEADS ** -0.5 * IDX_DH ** -0.5)
    lane = lax.broadcasted_iota(I32, (x.shape[0], 2 * IDX_DH), 1)
    for pair in range(IDX_HEADS // 2):
        wpair = jnp.where(lane < IDX_DH, wsc[:, 2 * pair:2 * pair + 1], wsc[:, 2 * pair + 1:2 * pair + 2])
        sl = slice(pair * 2 * IDX_DH, (pair + 1) * 2 * IDX_DH)
        o_iq[:, sl] = (iq[:, sl] * wpair).astype(BF16)


def _inproj(x2, g, ws, batch, seq):
    rows, d = x2.shape
    tm = ROW_TILE
    per_seq = seq // tm
    out_dtypes = [F32, BF16, F32, F32, BF16, BF16, BF16, BF16, F32, BF16]
    widths = [w.shape[1] for w in ws]
    out_widths = [widths[0], widths[1], widths[2]] + widths[4:]
    in_specs = [pl.BlockSpec((tm, d), lambda i: (i, 0)), _resident((1, d))]
    in_specs += [_resident(w.shape) for w in ws]
    out_specs = [pl.BlockSpec((tm, n), lambda i: (i, 0)) for n in out_widths]
    out_shape = [jax.ShapeDtypeStruct((rows, n), dt) for n, dt in zip(out_widths, out_dtypes)]
    av_pos = 7
    out_specs[av_pos] = pl.BlockSpec((None, out_widths[av_pos], tm), lambda i: (i // per_seq, 0, i % per_seq))
    out_shape[av_pos] = jax.ShapeDtypeStruct((batch, out_widths[av_pos], seq), BF16)
    return pl.pallas_call(
        _inproj_body,
        grid=(rows // tm,),
        in_specs=in_specs,
        out_specs=out_specs,
        out_shape=out_shape,
        compiler_params=pltpu.CompilerParams(
            dimension_semantics=("arbitrary",), vmem_limit_bytes=VMEM_LIMIT),
        name="inproj",
    )(x2, g, *ws)


def _row_scan(v, op, fill):
    n = v.shape[0]
    row = lax.broadcasted_iota(I32, v.shape, 0)
    sh = 1
    while sh < n:
        v = op(v, jnp.where(row >= sh, pltpu.roll(v, sh, 0), fill))
        sh *= 2
    return v


def _mlstm_body(mqk_ref, mv_ref, gate_ref, sm_ref, convw_ref, bpad_ref, mnorm_ref, out_ref,
                ext_ref, c_ref, n_ref, m_ref):
    L = M_CHUNK
    H, DK, DV = M_HEADS, M_QK, M_V

    @pl.when(pl.program_id(1) == 0)
    def _():
        ext_ref[0:CONV_HALO, :] = jnp.zeros((CONV_HALO, 2 * H * DK), F32)
        c_ref[...] = jnp.zeros(c_ref.shape, F32)
        n_ref[...] = jnp.zeros(n_ref.shape, F32)
        m_ref[...] = jnp.zeros(m_ref.shape, F32)

    raw = mqk_ref[...]
    ext_ref[CONV_HALO:CONV_HALO + L, :] = raw
    base = CONV_HALO - (CONV_W - 1)
    y = ext_ref[base:base + L, :] * convw_ref[0:1, :]
    for j in range(1, CONV_W):
        y = y + ext_ref[base + j:base + j + L, :] * convw_ref[j:j + 1, :]
    ext_ref[0:CONV_HALO, :] = raw[L - CONV_HALO:L, :]
    qk = _silu(y)

    g = sm_ref[...] + bpad_ref[...]
    lf = jax.nn.log_sigmoid(pltpu.roll(g, 128 - 8, 1))
    b = _row_scan(lf, jnp.add, 0.0)
    a = g - b
    m_prev = m_ref[0:1, :]
    mcol = jnp.maximum(m_prev, _row_scan(a, jnp.maximum, -jnp.inf))
    w_inter = jnp.exp(m_prev - mcol)
    e_neg_mt = jnp.exp(-(b + mcol))
    m_last = mcol[L - 1:L, :]
    wk = jnp.exp(a - m_last)
    decay = jnp.exp(m_prev - m_last)
    m_new = b[L - 1:L, :] + m_last
    a_rows = a.T

    ri = lax.broadcasted_iota(I32, (L, L), 0)
    ci = lax.broadcasted_iota(I32, (L, L), 1)
    causal = ci <= ri

    for h in range(H):
        q = qk[:, h * DK:(h + 1) * DK]
        k = qk[:, (H + h) * DK:(H + h + 1) * DK] * (DK ** -0.5)
        qb = q.astype(BF16)
        kb = k.astype(BF16)
        v = mv_ref[:, h * DV:(h + 1) * DV]
        m_col = mcol[:, h:h + 1]
        wi_col = w_inter[:, h:h + 1]
        emt_col = e_neg_mt[:, h:h + 1]
        wk_col = wk[:, h:h + 1]
        e = jnp.where(causal, jnp.exp(a_rows[h:h + 1, :] - m_col), 0.0)
        s = _dot_nt(qb, kb) * e
        c_old = c_ref[h]
        num = _dot(s.astype(BF16), v) + wi_col * _dot(qb, c_old.astype(BF16))
        den = (jnp.sum(s, axis=1, keepdims=True)
               + wi_col * jnp.sum(q * n_ref[h:h + 1, :], axis=1, keepdims=True))
        den = jnp.maximum(jnp.abs(den), emt_col)
        hh = num / den
        hn = hh * lax.rsqrt(jnp.mean(hh * hh, axis=-1, keepdims=True) + EPS)
        hn = hn * mnorm_ref[:, h * DV:(h + 1) * DV]
        out_ref[:, h * DV:(h + 1) * DV] = (gate_ref[:, h * DV:(h + 1) * DV] * hn).astype(BF16)
        kw = k * wk_col
        dec = decay[:, h:h + 1]
        c_ref[h] = dec * c_old + _dot_tn(kw.astype(BF16), v)
        n_ref[h:h + 1, :] = dec * n_ref[h:h + 1, :] + jnp.sum(kw, axis=0, keepdims=True)
    m_ref[...] = jnp.broadcast_to(m_new, m_ref.shape)


def _mlstm(mqk, mv, gate, sm, conv_w, bpad, mnorm, batch, seq):
    L = M_CHUNK
    nc = seq // L
    wqk = mqk.shape[1]
    wv = mv.shape[1]
    row = lambda b, c: (b * nc + c, 0)
    return pl.pallas_call(
        _mlstm_body,
        grid=(batch, nc),
        in_specs=[
            pl.BlockSpec((L, wqk), row),
            pl.BlockSpec((L, wv), row),
            pl.BlockSpec((L, wv), row),
            pl.BlockSpec((L, 128), row),
            _resident(conv_w.shape),
            _resident(bpad.shape),
            _resident(mnorm.shape),
        ],
        out_specs=pl.BlockSpec((L, wv), row),
        out_shape=jax.ShapeDtypeStruct((batch * seq, wv), BF16),
        scratch_shapes=[
            pltpu.VMEM((L + CONV_HALO, wqk), F32),
            pltpu.VMEM((M_HEADS, M_QK, M_V), F32),
            pltpu.VMEM((8, M_QK), F32),
            pltpu.VMEM((8, 128), F32),
        ],
        compiler_params=pltpu.CompilerParams(
            dimension_semantics=("arbitrary", "arbitrary"), vmem_limit_bytes=VMEM_LIMIT),
        name="mlstm",
    )(mqk, mv, gate, sm, conv_w, bpad, mnorm)


def _t5_bucket_table(max_dist):
    d = np.arange(max_dist)
    max_exact = REL_BUCKETS // 2
    tabs = []
    for dt in (np.float32, np.float64):
        ratio = np.maximum(d, 1).astype(dt) / dt(max_exact)
        large = max_exact + (np.log(ratio) / dt(math.log(REL_MAX_DIST / max_exact))
                             * dt(REL_BUCKETS - max_exact)).astype(np.int32)
        large = np.minimum(large, REL_BUCKETS - 1)
        tabs.append(np.where(d < max_exact, d, large).astype(np.int32))
    assert np.array_equal(tabs[0], tabs[1])
    return tabs[0]


def _bucket_maps():
    tab = _t5_bucket_table(Q_TILE + K_TILE)
    s = np.arange(K_TILE)[:, None]
    t = np.arange(Q_TILE)[None, :]
    diag = tab[np.maximum(t - s, 0)]
    prev = tab[K_TILE + t - s]
    assert np.all(tab[K_TILE:] == REL_BUCKETS - 1)
    return np.stack([prev, diag]).astype(np.int32)


def _dsa_body(rb_ref, iq_ref, sm_ref, aq_ref, az_ref, ik_ref, k_ref, vt_ref, map_ref, out_ref,
              sc_ref, iqm_ref, qm_ref, bias_ref, acc_ref, m_ref, l_ref, alpha_ref, mtile_ref, lg_ref, p_ref):
    TQ, TK, H, DH = Q_TILE, K_TILE, A_HEADS, A_DH
    j = pl.program_id(1)
    n_tiles = j + 1

    @pl.when((pl.program_id(0) == 0) & (j == 0))
    def _():
        bias_ref[...] = jnp.zeros(bias_ref.shape, F32)

        def fill(bkt, carry):
            for which in range(2):
                hit = map_ref[which] == bkt
                for h in range(H):
                    val = (rb_ref[bkt, h] - rb_ref[REL_BUCKETS - 1, h]) * LOG2E
                    bias_ref[which, h] = jnp.where(hit, val, bias_ref[which, h])
            return carry

        lax.fori_loop(0, REL_BUCKETS - 1, fill, 0)

    lane = lax.broadcasted_iota(I32, (TQ, 2 * DH), 1)
    for h in range(H):
        p0 = (h // 2) * 2 * DH
        keep = (lane < DH) if h % 2 == 0 else (lane >= DH)
        iqm_ref[h] = jnp.where(keep, iq_ref[:, p0:p0 + 2 * DH], jnp.zeros((), BF16))
        qm_ref[h] = jnp.where(keep, aq_ref[:, p0:p0 + 2 * DH], jnp.zeros((), BF16))
    w_rows = sm_ref[...].T[16:16 + IDX_HEADS, :]
    clamp_lo = jnp.where(w_rows >= 0.0, 0.0, -jnp.inf)
    clamp_hi = jnp.where(w_rows >= 0.0, jnp.inf, 0.0)

    s_loc = lax.broadcasted_iota(I32, (TK, TQ), 0)
    t_loc = lax.broadcasted_iota(I32, (TK, TQ), 1)

    fold = lambda v: v.reshape(v.shape[0] // 8, 8, TQ)

    def index_tile(kb, diag, stats):
        r0 = pl.multiple_of(kb * TK, TK)
        lhs = ik_ref[pl.ds(r0, TK), :]
        sc = None
        for h in range(H):
            y = _dot_nt(lhs, iqm_ref[h])
            term = jnp.minimum(jnp.maximum(y, clamp_lo[h:h + 1, :]), clamp_hi[h:h + 1, :])
            sc = term if sc is None else sc + term
        sc_min = sc
        if diag:
            admissible = s_loc <= t_loc
            sc_min = jnp.where(admissible, sc, -NEG_BIG)
            sc = jnp.where(admissible, sc, NEG_BIG)
        sc_ref[pl.ds(r0, TK), :] = sc
        c_pos_, c_nn_, mx_, mn_ = stats
        c_pos_ = c_pos_ + jnp.sum(fold(jnp.where(sc > 0.0, 1, 0).astype(I32)), axis=0)
        c_nn_ = c_nn_ + jnp.sum(fold(jnp.where(sc >= 0.0, 1, 0).astype(I32)), axis=0)
        mx_ = jnp.maximum(mx_, jnp.max(fold(sc), axis=0))
        mn_ = jnp.minimum(mn_, jnp.min(fold(sc_min), axis=0))
        return c_pos_, c_nn_, mx_, mn_

    stats0 = (jnp.zeros((8, TQ), I32), jnp.zeros((8, TQ), I32),
              jnp.full((8, TQ), NEG_BIG, F32), jnp.full((8, TQ), -NEG_BIG, F32))
    stats1 = lax.fori_loop(0, j, lambda kb, st: index_tile(kb, False, st), stats0)
    c_pos8, c_nn8, mx8, mn8 = index_tile(j, True, stats1)
    c_pos = jnp.sum(c_pos8, axis=0, keepdims=True)
    c_nn = jnp.sum(c_nn8, axis=0, keepdims=True)
    hi0 = jnp.max(mx8, axis=0, keepdims=True)
    mn = jnp.min(mn8, axis=0, keepdims=True)

    @pl.when((n_tiles & 1) == 1)
    def _():
        sc_ref[pl.ds(pl.multiple_of(n_tiles * TK, TK), TK), :] = jnp.full((TK, TQ), NEG_BIG, F32)

    n_pairs = (n_tiles + 1) >> 1

    def count_gt(thr):
        def body(kp, cnt):
            r0 = pl.multiple_of(kp * (2 * TK), 2 * TK)
            hit = jnp.where(sc_ref[pl.ds(r0, 2 * TK), :] > thr, 1, 0).astype(I32)
            return cnt + jnp.sum(hit.reshape(2 * TK // 32, 32, TQ), axis=0)
        cnt32 = lax.fori_loop(1, n_pairs, body, body(0, jnp.zeros((32, TQ), I32)))
        return jnp.sum(cnt32, axis=0, keepdims=True)

    t_glob = j * TQ + lax.broadcasted_iota(I32, (1, TQ), 1)
    k_eff = jnp.minimum(TOPK_MAX, t_glob + 1)

    zero = jnp.zeros((1, TQ), F32)
    all_in = (t_glob + 1) <= TOPK_MAX
    pos_ok = c_pos >= k_eff
    tie0 = (~all_in) & (~pos_ok) & (c_nn >= k_eff)
    lo_neg = jnp.maximum(mn + mn - 1.0, NEG_BIG)
    lo = jnp.where(all_in, NEG_BIG, jnp.where(pos_ok, zero, lo_neg))
    hi = jnp.where((~all_in) & (~pos_ok), zero, hi0)
    c_lo = jnp.where(pos_ok, c_pos, t_glob + 1)
    c_hi = jnp.where(pos_ok, 0, c_pos)
    fin0 = all_in | (pos_ok & (c_pos == k_eff)) | tie0

    def open_rows(fin_, c_lo_, c_hi_):
        return (fin_ == 0.0) & ((c_lo_ - c_hi_) > 2)

    def bis_cond(st):
        return (st[0] < BISECT_MAX_ITERS) & (st[1] > 0)

    def bis_body(st):
        it, _, lo_, hi_, c_lo_, c_hi_, fin_, tie_ = st
        go_n = (jnp.max(jnp.where(open_rows(fin_, c_lo_, c_hi_), 1.0, 0.0)) > 0.0).astype(I32)
        mid = 0.5 * lo_ + 0.5 * hi_
        collapsed = (mid <= lo_) | (mid >= hi_)
        cm = count_gt(mid)
        act = fin_ == 0.0
        up = act & (~collapsed) & (cm >= k_eff)
        dn = act & (~collapsed) & (cm < k_eff)
        tie_n = jnp.where(act & collapsed, 1.0, tie_)
        fin_n = jnp.where((act & collapsed) | (up & (cm == k_eff)), 1.0, fin_)
        return (it + 1, go_n, jnp.where(up, mid, lo_), jnp.where(dn, mid, hi_),
                jnp.where(up, cm, c_lo_), jnp.where(dn, cm, c_hi_), fin_n, tie_n)

    _, _, lo, hi, c_lo, c_hi, fin_f, tie_f = lax.while_loop(
        bis_cond, bis_body,
        (jnp.int32(0), jnp.int32(1), lo, hi, c_lo, c_hi, jnp.where(fin0, 1.0, 0.0), jnp.where(tie0, 1.0, 0.0)))

    def pick_two():
        def body(kp, carry):
            e_min, e_max = carry
            r0 = pl.multiple_of(kp * (2 * TK), 2 * TK)
            sc = sc_ref[pl.ds(r0, 2 * TK), :]
            above = jnp.where(sc > lo, sc, -NEG_BIG).reshape(2 * TK // 32, 32, TQ)
            below_hi = jnp.where(sc <= hi, sc, NEG_BIG).reshape(2 * TK // 32, 32, TQ)
            return jnp.minimum(e_min, jnp.min(above, axis=0)), jnp.maximum(e_max, jnp.max(below_hi, axis=0))
        init = (jnp.full((32, TQ), -NEG_BIG, F32), jnp.full((32, TQ), NEG_BIG, F32))
        e_min, e_max = lax.fori_loop(0, n_pairs, body, init)
        return jnp.min(e_min, axis=0, keepdims=True), jnp.max(e_max, axis=0, keepdims=True)

    e1, e2 = pick_two()
    left = fin_f == 0.0
    lo = jnp.where(left & (e1 < e2), e1, lo)
    hi = jnp.where(left & (e1 >= e2), e1, hi)
    tie_f = jnp.where(left & (e1 >= e2), 1.0, tie_f)
    tie = tie_f > 0.0
    theta_ref = m_ref
    theta_ref[0:1, :] = lo

    @pl.when(jnp.max(tie_f) > 0.0)
    def _():
        v = hi

        def pass_a(kb, carry):
            n_gt, below = carry
            r0 = pl.multiple_of(kb * TK, TK)
            sc = sc_ref[pl.ds(r0, TK), :]
            n_gt = n_gt + jnp.sum(jnp.where(sc > v, 1.0, 0.0), axis=0, keepdims=True)
            below = jnp.maximum(below, jnp.max(jnp.where(sc < v, sc, NEG_BIG), axis=0, keepdims=True))
            return n_gt, below

        n_gt, below = lax.fori_loop(
            0, n_tiles, pass_a, (jnp.zeros((1, TQ), F32), jnp.full((1, TQ), NEG_BIG, F32)))
        need = k_eff.astype(F32) - n_gt
        tri = jnp.where(lax.broadcasted_iota(I32, (TK, TK), 1) < lax.broadcasted_iota(I32, (TK, TK), 0),
                        1.0, 0.0).astype(BF16)

        def pass_b(kb, seen):
            r0 = pl.multiple_of(kb * TK, TK)
            sc = sc_ref[pl.ds(r0, TK), :]
            eq = (sc == v) & tie
            eqf = jnp.where(eq, 1.0, 0.0)
            rank = seen + _dot(tri, eqf.astype(BF16))
            sc_ref[pl.ds(r0, TK), :] = jnp.where(eq & (rank >= need), NEG_BIG, sc)
            return seen + jnp.sum(eqf, axis=0, keepdims=True)

        lax.fori_loop(0, n_tiles, pass_b, jnp.zeros((1, TQ), F32))
        theta_ref[0:1, :] = jnp.where(tie, below, lo)

    theta = theta_ref[0:1, :]

    m_ref[...] = jnp.full(m_ref.shape, LOGIT_NEG, F32)
    l_ref[...] = jnp.zeros(l_ref.shape, F32)
    acc_ref[...] = jnp.zeros(acc_ref.shape, F32)

    def tile_mask(kb):
        r0 = pl.multiple_of(kb * TK, TK)
        return jnp.where(sc_ref[pl.ds(r0, TK), :] > theta, 0.0, LOGIT_NEG)

    def logits_head(kb, slot, near, mb, h):
        r0 = pl.multiple_of(kb * TK, TK)
        p0 = (h // 2) * 2 * DH
        lg = _dot_nt(k_ref[pl.ds(r0, TK), p0:p0 + 2 * DH], qm_ref[h]) + mb
        if near is not None:
            lg = lg + bias_ref[near, h]
        lg_ref[slot, h] = lg
        m_old = m_ref[h:h + 1, :]
        m_new = jnp.maximum(m_old, jnp.max(lg, axis=0, keepdims=True))
        alpha_ref[slot, h:h + 1, :] = jnp.exp2(m_old - m_new)
        mtile_ref[slot, h:h + 1, :] = m_new
        m_ref[h:h + 1, :] = m_new

    def exp_head(slot, h):
        p_ref[slot, h] = jnp.exp2(lg_ref[slot, h] - mtile_ref[slot, h:h + 1, :]).astype(BF16)

    def pv_head(kb, slot, h):
        r0 = pl.multiple_of(kb * TK, TK)
        ones = jnp.ones((ONES_ROWS, TK), BF16)
        lhs = jnp.concatenate([vt_ref[h * DH:(h + 1) * DH, pl.ds(r0, TK)], ones], axis=0)
        pv = _dot(lhs, p_ref[slot, h])
        alpha = alpha_ref[slot, h:h + 1, :]
        acc_ref[h * DH:(h + 1) * DH, :] = alpha * acc_ref[h * DH:(h + 1) * DH, :] + pv[0:DH, :]
        l_ref[h:h + 1, :] = alpha * l_ref[h:h + 1, :] + pv[DH:DH + 1, :]

    def fill(kb, slot, near):
        mb = tile_mask(kb)
        for h in range(H):
            logits_head(kb, slot, near, mb, h)

    def drain(kb, slot):
        for h in range(H):
            exp_head(slot, h)
        for h in range(H):
            pv_head(kb, slot, h)

    def step(kb_prev, slot_prev, kb_next, slot_next, near_next):
        mb = tile_mask(kb_next)
        for h in range(H):
            exp_head(slot_prev, h)
            logits_head(kb_next, slot_next, near_next, mb, h)
            pv_head(kb_prev, slot_prev, h)

    n_far = j - 1

    @pl.when(n_far >= 1)
    def _():
        fill(0, 0, None)

        def far_loop(t, carry):
            step(2 * t, 0, 2 * t + 1, 1, None)
            step(2 * t + 1, 1, 2 * t + 2, 0, None)
            return carry

        lax.fori_loop(0, (n_far - 1) >> 1, far_loop, 0)

        @pl.when(((n_far - 1) & 1) == 1)
        def _():
            step(n_far - 2, 0, n_far - 1, 1, None)
            drain(n_far - 1, 1)

        @pl.when(((n_far - 1) & 1) == 0)
        def _():
            drain(n_far - 1, 0)

    @pl.when(j >= 1)
    def _():
        fill(j - 1, 0, 0)
        step(j - 1, 0, j, 1, 1)
        drain(j, 1)

    @pl.when(j == 0)
    def _():
        fill(j, 0, 1)
        drain(j, 0)

    for h in range(H):
        acc_ref[h * DH:(h + 1) * DH, :] = acc_ref[h * DH:(h + 1) * DH, :] / l_ref[h:h + 1, :]
    out_ref[...] = (acc_ref[...].T * az_ref[...]).astype(BF16)


def _dsa(rel_bias, iq, sm, aq, az, ik2, ak, avt, batch, seq):
    TQ = Q_TILE
    nq = seq // TQ
    width = A_HEADS * A_DH
    maps = jnp.asarray(_bucket_maps())
    qrow = lambda b, j: (b * nq + j, 0)
    return pl.pallas_call(
        _dsa_body,
        grid=(batch, nq),
        in_specs=[
            pl.BlockSpec(memory_space=pltpu.SMEM),
            pl.BlockSpec((TQ, width), qrow),
            pl.BlockSpec((TQ, 128), qrow),
            pl.BlockSpec((TQ, width), qrow),
            pl.BlockSpec((TQ, width), qrow),
            pl.BlockSpec((seq, 128), lambda b, j: (b, 0), pipeline_mode=pl.Buffered(1)),
            pl.BlockSpec((seq, width), lambda b, j: (b, 0), pipeline_mode=pl.Buffered(1)),
            pl.BlockSpec((None, width, seq), lambda b, j: (b, 0, 0), pipeline_mode=pl.Buffered(1)),
            _resident(maps.shape),
        ],
        out_specs=pl.BlockSpec((TQ, width), qrow),
        out_shape=jax.ShapeDtypeStruct((batch * seq, width), BF16),
        scratch_shapes=[
            pltpu.VMEM((seq, TQ), F32),
            pltpu.VMEM((A_HEADS, TQ, 2 * A_DH), BF16),
            pltpu.VMEM((A_HEADS, TQ, 2 * A_DH), BF16),
            pltpu.VMEM((2, A_HEADS, K_TILE, TQ), F32),
            pltpu.VMEM((A_HEADS * A_DH, TQ), F32),
            pltpu.VMEM((8, TQ), F32),
            pltpu.VMEM((8, TQ), F32),
            pltpu.VMEM((2, 8, TQ), F32),
            pltpu.VMEM((2, 8, TQ), F32),
            pltpu.VMEM((2, A_HEADS, K_TILE, TQ), F32),
            pltpu.VMEM((2, A_HEADS, K_TILE, TQ), BF16),
        ],
        compiler_params=pltpu.CompilerParams(
            dimension_semantics=("arbitrary", "arbitrary"), vmem_limit_bytes=VMEM_LIMIT),
        name="dsa",
    )(rel_bias, iq, sm, aq, az, ik2, ak, avt, maps)


def _outproj_body(x_ref, hm_ref, ha_ref, p_ref, wom_ref, woa_ref, wple_ref, wg_ref, g_ref, o_ref):
    h1 = x_ref[...] + _dot(hm_ref[...], wom_ref[...]) + _dot(ha_ref[...], woa_ref[...])
    gate = jax.nn.sigmoid(_dot(h1.astype(BF16), wg_ref[...]))
    h2 = h1 + _dot(p_ref[...].astype(BF16), wple_ref[...]) * gate
    ms = jnp.mean(h2 * h2, axis=-1, keepdims=True)
    o_ref[...] = h2 * lax.rsqrt(ms + EPS) * g_ref[...]


def _outproj(x2, hm, ha, p2, wom, woa, wple, wg, g):
    rows, d = x2.shape
    tm = ROW_TILE
    tile = lambda n: pl.BlockSpec((tm, n), lambda i: (i, 0))
    return pl.pallas_call(
        _outproj_body,
        grid=(rows // tm,),
        in_specs=[tile(d), tile(hm.shape[1]), tile(ha.shape[1]), tile(p2.shape[1]),
                  _resident(wom.shape), _resident(woa.shape), _resident(wple.shape),
                  _resident(wg.shape), _resident(g.shape)],
        out_specs=tile(d),
        out_shape=jax.ShapeDtypeStruct((rows, d), F32),
        compiler_params=pltpu.CompilerParams(
            dimension_semantics=("arbitrary",), vmem_limit_bytes=VMEM_LIMIT),
        name="outproj",
    )(x2, hm, ha, p2, wom, woa, wple, wg, g)


def kernel(x, p, w_in, b_gate, conv_w, w_out, norm_in, m_norm, rel_bias, w_ple, w_ple_gate, norm_final):
    batch, seq, d = x.shape
    assert w_in.shape[0] == 1 and p.shape[0] == 1, "single-layer block"
    assert seq % Q_TILE == 0 and seq % M_CHUNK == 0 and (batch * seq) % ROW_TILE == 0
    rows = batch * seq
    x2 = x.reshape(rows, d)

    sizes = [M_HEADS * M_QK, M_HEADS * M_QK, M_HEADS * M_V, M_HEADS * M_V, M_HEADS * M_V, 2 * M_HEADS,
             A_HEADS * A_DH, A_HEADS * A_DH, A_HEADS * A_DH, A_HEADS * A_DH,
             IDX_HEADS * IDX_DH, IDX_DH, IDX_HEADS]
    offs = np.concatenate([[0], np.cumsum(sizes)])
    wi = w_in[0]
    col = lambda i: wi[:, int(offs[i]):int(offs[i + 1])]
    w_if, w_ixk, w_ixw = col(5), col(11), col(12)
    zeros = lambda n: jnp.zeros((d, n), wi.dtype)
    w_sm = jnp.concatenate([w_if[:, :M_HEADS], zeros(4), w_if[:, M_HEADS:], zeros(4), w_ixw, zeros(104)], axis=1)
    w_ik2 = jnp.concatenate([w_ixk, w_ixk], axis=1)
    ws = [jnp.concatenate([col(0), col(1)], axis=1), col(2), col(3), col(4), w_sm, w_ik2,
          col(6), col(7), col(8), col(9), col(10)]
    ws = [w.astype(BF16) for w in ws]
    bg = b_gate[0]
    bpad = jnp.concatenate([bg[:M_HEADS], jnp.zeros((4,), F32), bg[M_HEADS:], jnp.zeros((116,), F32)])[None, :]

    mqk, mv, gate, sm, ik2, aq, ak, avt, az, iq = _inproj(x2, norm_in[0][None, :], ws, batch, seq)

    hm = _mlstm(mqk, mv, gate, sm, conv_w[0], bpad, m_norm[0][None, :], batch, seq)

    ha = _dsa(rel_bias, iq, sm, aq, az, ik2, ak, avt, batch, seq)

    wo = w_out[0].astype(BF16)
    out = _outproj(x2, hm, ha, p[0].reshape(rows, -1), wo[:M_HEADS * M_V], wo[M_HEADS * M_V:],
                   w_ple[0].astype(BF16), w_ple_gate[0].astype(BF16), norm_final[None, :])
    return out.reshape(batch, seq, d)
```

```python
import functools
import math

import numpy as np
import jax
import jax.numpy as jnp
from jax import lax
from jax.experimental import pallas as pl
from jax.experimental.pallas import tpu as pltpu

F32 = jnp.float32
BF16 = jnp.bfloat16
I32 = jnp.int32

M_HEADS = 4
M_QK = 128
M_V = 256
CONV_W = 4
A_HEADS = 8
A_DH = 64
IDX_HEADS = 8
IDX_DH = 64
TOPK_MAX = 256
REL_BUCKETS = 32
REL_MAX_DIST = 128
EPS = 1e-6

LOG2E = 1.4426950408889634
NEG_BIG = -3.0e38
LOGIT_NEG = -1.0e30

ROW_TILE = 512
M_CHUNK = 256
Q_TILE = 256
K_TILE = 256
ONES_ROWS = 16
CONV_HALO = 8
BISECT_MAX_ITERS = 256
VMEM_LIMIT = 56 * 1024 * 1024


def _silu(v):
    return v * jax.nn.sigmoid(v)


def _dot(a, b):
    return jnp.dot(a, b, preferred_element_type=F32)


def _dot_nt(a, b):
    return lax.dot_general(a, b, (((1,), (1,)), ((), ())), preferred_element_type=F32)


def _dot_tn(a, b):
    return lax.dot_general(a, b, (((0,), (0,)), ((), ())), preferred_element_type=F32)


def _resident(shape):
    nd = len(shape)
    return pl.BlockSpec(shape, lambda *_: (0,) * nd, pipeline_mode=pl.Buffered(1))


def _inproj_body(x_ref, g_ref, w_mqk, w_mv, w_mo, w_mz, w_sm, w_ik, w_aq, w_ak, w_av, w_az, w_iq,
                 o_mqk, o_mv, o_gate, o_sm, o_ik, o_aq, o_ak, o_avt, o_az, o_iq):
    x = x_ref[...]
    ms = jnp.mean(x * x, axis=-1, keepdims=True)
    xn = (x * lax.rsqrt(ms + EPS) * g_ref[...]).astype(BF16)
    o_mqk[...] = _dot(xn, w_mqk[...])
    o_mv[...] = _dot(xn, w_mv[...]).astype(BF16)
    o_gate[...] = jax.nn.sigmoid(_dot(xn, w_mo[...])) * _silu(_dot(xn, w_mz[...]))
    sm = _dot(xn, w_sm[...])
    o_sm[...] = sm
    o_ik[...] = _dot(xn, w_ik[...]).astype(BF16)
    o_aq[...] = (_dot(xn, w_aq[...]) * (A_DH ** -0.5 * LOG2E)).astype(BF16)
    o_ak[...] = _dot(xn, w_ak[...]).astype(BF16)
    o_avt[...] = _dot(xn, w_av[...]).T.astype(BF16)
    o_az[...] = _silu(_dot(xn, w_az[...]))
    iq = _dot(xn, w_iq[...])
    wsc = sm[:, 16:16 + IDX_HEADS] * (IDX_HEADS ** -0.5 * IDX_DH ** -0.5)
    lane = lax.broadcasted_iota(I32, (x.shape[0], 2 * IDX_DH), 1)
    for pair in range(IDX_HEADS // 2):
        wpair = jnp.where(lane < IDX_DH, wsc[:, 2 * pair:2 * pair + 1], wsc[:, 2 * pair + 1:2 * pair + 2])
        sl = slice(pair * 2 * IDX_DH, (pair + 1) * 2 * IDX_DH)
        o_iq[:, sl] = (iq[:, sl] * wpair).astype(BF16)


def _inproj(x2, g, ws, batch, seq):
    rows, d = x2.shape
    tm = ROW_TILE
    per_seq = seq // tm
    out_dtypes = [F32, BF16, F32, F32, BF16, BF16, BF16, BF16, F32, BF16]
    widths = [w.shape[1] for w in ws]
    out_widths = [widths[0], widths[1], widths[2]] + widths[4:]
    in_specs = [pl.BlockSpec((tm, d), lambda i: (i, 0)), _resident((1, d))]
    in_specs += [_resident(w.shape) for w in ws]
    out_specs = [pl.BlockSpec((tm, n), lambda i: (i, 0)) for n in out_widths]
    out_shape = [jax.ShapeDtypeStruct((rows, n), dt) for n, dt in zip(out_widths, out_dtypes)]
    av_pos = 7
    out_specs[av_pos] = pl.BlockSpec((None, out_widths[av_pos], tm), lambda i: (i // per_seq, 0, i % per_seq))
    out_shape[av_pos] = jax.ShapeDtypeStruct((batch, out_widths[av_pos], seq), BF16)
    return pl.pallas_call(
        _inproj_body,
        grid=(rows // tm,),
        in_specs=in_specs,
        out_specs=out_specs,
        out_shape=out_shape,
        compiler_params=pltpu.CompilerParams(
            dimension_semantics=("arbitrary",), vmem_limit_bytes=VMEM_LIMIT),
        name="inproj",
    )(x2, g, *ws)


def _row_scan(v, op, fill):
    n = v.shape[0]
    row = lax.broadcasted_iota(I32, v.shape, 0)
    sh = 1
    while sh < n:
        v = op(v, jnp.where(row >= sh, pltpu.roll(v, sh, 0), fill))
        sh *= 2
    return v


def _mlstm_body(mqk_ref, mv_ref, gate_ref, sm_ref, convw_ref, bpad_ref, mnorm_ref, out_ref,
                ext_ref, c_ref, n_ref, m_ref):
    L = M_CHUNK
    H, DK, DV = M_HEADS, M_QK, M_V

    @pl.when(pl.program_id(1) == 0)
    def _():
        ext_ref[0:CONV_HALO, :] = jnp.zeros((CONV_HALO, 2 * H * DK), F32)
        c_ref[...] = jnp.zeros(c_ref.shape, F32)
        n_ref[...] = jnp.zeros(n_ref.shape, F32)
        m_ref[...] = jnp.zeros(m_ref.shape, F32)

    raw = mqk_ref[...]
    ext_ref[CONV_HALO:CONV_HALO + L, :] = raw
    base = CONV_HALO - (CONV_W - 1)
    y = ext_ref[base:base + L, :] * convw_ref[0:1, :]
    for j in range(1, CONV_W):
        y = y + ext_ref[base + j:base + j + L, :] * convw_ref[j:j + 1, :]
    ext_ref[0:CONV_HALO, :] = raw[L - CONV_HALO:L, :]
    qk = _silu(y)

    g = sm_ref[...] + bpad_ref[...]
    lf = jax.nn.log_sigmoid(pltpu.roll(g, 128 - 8, 1))
    b = _row_scan(lf, jnp.add, 0.0)
    a = g - b
    m_prev = m_ref[0:1, :]
    mcol = jnp.maximum(m_prev, _row_scan(a, jnp.maximum, -jnp.inf))
    w_inter = jnp.exp(m_prev - mcol)
    e_neg_mt = jnp.exp(-(b + mcol))
    m_last = mcol[L - 1:L, :]
    wk = jnp.exp(a - m_last)
    decay = jnp.exp(m_prev - m_last)
    m_new = b[L - 1:L, :] + m_last
    a_rows = a.T

    ri = lax.broadcasted_iota(I32, (L, L), 0)
    ci = lax.broadcasted_iota(I32, (L, L), 1)
    causal = ci <= ri

    heads = range(H)
    q = [qk[:, h * DK:(h + 1) * DK] for h in heads]
    k = [qk[:, (H + h) * DK:(H + h + 1) * DK] * (DK ** -0.5) for h in heads]
    qb = [q[h].astype(BF16) for h in heads]
    v = [mv_ref[:, h * DV:(h + 1) * DV] for h in heads]
    c_old = [c_ref[h] for h in heads]
    qkt = [_dot_nt(qb[h], k[h].astype(BF16)) for h in heads]
    q_c = [_dot(qb[h], c_old[h].astype(BF16)) for h in heads]
    s = [qkt[h] * jnp.where(causal, jnp.exp(a_rows[h:h + 1, :] - mcol[:, h:h + 1]), 0.0) for h in heads]
    num = [_dot(s[h].astype(BF16), v[h]) + w_inter[:, h:h + 1] * q_c[h] for h in heads]
    kw = [k[h] * wk[:, h:h + 1] for h in heads]
    upd = [_dot_tn(kw[h].astype(BF16), v[h]) for h in heads]
    for h in heads:
        den = (jnp.sum(s[h], axis=1, keepdims=True)
               + w_inter[:, h:h + 1] * jnp.sum(q[h] * n_ref[h:h + 1, :], axis=1, keepdims=True))
        den = jnp.maximum(jnp.abs(den), e_neg_mt[:, h:h + 1])
        hh = num[h] / den
        hn = hh * lax.rsqrt(jnp.mean(hh * hh, axis=-1, keepdims=True) + EPS)
        hn = hn * mnorm_ref[:, h * DV:(h + 1) * DV]
        out_ref[:, h * DV:(h + 1) * DV] = (gate_ref[:, h * DV:(h + 1) * DV] * hn).astype(BF16)
    for h in heads:
        dec = decay[:, h:h + 1]
        c_ref[h] = dec * c_old[h] + upd[h]
        n_ref[h:h + 1, :] = dec * n_ref[h:h + 1, :] + jnp.sum(kw[h], axis=0, keepdims=True)
    m_ref[...] = jnp.broadcast_to(m_new, m_ref.shape)


def _mlstm(mqk, mv, gate, sm, conv_w, bpad, mnorm, batch, seq):
    L = M_CHUNK
    nc = seq // L
    wqk = mqk.shape[1]
    wv = mv.shape[1]
    row = lambda b, c: (b * nc + c, 0)
    return pl.pallas_call(
        _mlstm_body,
        grid=(batch, nc),
        in_specs=[
            pl.BlockSpec((L, wqk), row),
            pl.BlockSpec((L, wv), row),
            pl.BlockSpec((L, wv), row),
            pl.BlockSpec((L, 128), row),
            _resident(conv_w.shape),
            _resident(bpad.shape),
            _resident(mnorm.shape),
        ],
        out_specs=pl.BlockSpec((L, wv), row),
        out_shape=jax.ShapeDtypeStruct((batch * seq, wv), BF16),
        scratch_shapes=[
            pltpu.VMEM((L + CONV_HALO, wqk), F32),
            pltpu.VMEM((M_HEADS, M_QK, M_V), F32),
            pltpu.VMEM((8, M_QK), F32),
            pltpu.VMEM((8, 128), F32),
        ],
        compiler_params=pltpu.CompilerParams(
            dimension_semantics=("arbitrary", "arbitrary"), vmem_limit_bytes=VMEM_LIMIT),
        name="mlstm",
    )(mqk, mv, gate, sm, conv_w, bpad, mnorm)


def _t5_bucket_table(max_dist):
    d = np.arange(max_dist)
    max_exact = REL_BUCKETS // 2
    tabs = []
    for dt in (np.float32, np.float64):
        ratio = np.maximum(d, 1).astype(dt) / dt(max_exact)
        large = max_exact + (np.log(ratio) / dt(math.log(REL_MAX_DIST / max_exact))
                             * dt(REL_BUCKETS - max_exact)).astype(np.int32)
        large = np.minimum(large, REL_BUCKETS - 1)
        tabs.append(np.where(d < max_exact, d, large).astype(np.int32))
    assert np.array_equal(tabs[0], tabs[1])
    return tabs[0]


def _bucket_maps():
    tab = _t5_bucket_table(Q_TILE + K_TILE)
    s = np.arange(K_TILE)[:, None]
    t = np.arange(Q_TILE)[None, :]
    diag = tab[np.maximum(t - s, 0)]
    prev = tab[K_TILE + t - s]
    assert np.all(tab[K_TILE:] == REL_BUCKETS - 1)
    return np.stack([prev, diag]).astype(np.int32)


def _dsa_body(rb_ref, iq_ref, sm_ref, aq_ref, az_ref, ik_ref, k_ref, vt_ref, map_ref, out_ref,
              sc_ref, stat_ref, iqm_ref, qm_ref, bias_ref, acc_ref, m_ref, l_ref, alpha_ref, mtile_ref, lg_ref,
              p_ref):
    TQ, TK, H, DH = Q_TILE, K_TILE, A_HEADS, A_DH
    j = pl.program_id(1)
    n_tiles = j + 1

    @pl.when((pl.program_id(0) == 0) & (j == 0))
    def _():
        bias_ref[...] = jnp.zeros(bias_ref.shape, F32)

        def fill(bkt, carry):
            for which in range(2):
                hit = map_ref[which] == bkt
                for h in range(H):
                    val = (rb_ref[bkt, h] - rb_ref[REL_BUCKETS - 1, h]) * LOG2E
                    bias_ref[which, h] = jnp.where(hit, val, bias_ref[which, h])
            return carry

        lax.fori_loop(0, REL_BUCKETS - 1, fill, 0)

    lane = lax.broadcasted_iota(I32, (TQ, 2 * DH), 1)
    for h in range(H):
        p0 = (h // 2) * 2 * DH
        keep = (lane < DH) if h % 2 == 0 else (lane >= DH)
        iqm_ref[h] = jnp.where(keep, iq_ref[:, p0:p0 + 2 * DH], jnp.zeros((), BF16))
        qm_ref[h] = jnp.where(keep, aq_ref[:, p0:p0 + 2 * DH], jnp.zeros((), BF16))
    w_rows = sm_ref[...].T[16:16 + IDX_HEADS, :]
    clamp_lo = jnp.where(w_rows >= 0.0, 0.0, -jnp.inf)
    clamp_hi = jnp.where(w_rows >= 0.0, jnp.inf, 0.0)

    s_loc = lax.broadcasted_iota(I32, (TK, TQ), 0)
    t_loc = lax.broadcasted_iota(I32, (TK, TQ), 1)

    fold = lambda v: v.reshape(v.shape[0] // 8, 8, TQ)

    def index_tile(kb, diag, stats):
        r0 = pl.multiple_of(kb * TK, TK)
        lhs = ik_ref[pl.ds(r0, TK), :]
        sc = None
        for h in range(H):
            y = _dot_nt(lhs, iqm_ref[h])
            term = jnp.minimum(jnp.maximum(y, clamp_lo[h:h + 1, :]), clamp_hi[h:h + 1, :])
            sc = term if sc is None else sc + term
        sc_min = sc
        if diag:
            admissible = s_loc <= t_loc
            sc_min = jnp.where(admissible, sc, -NEG_BIG)
            sc = jnp.where(admissible, sc, NEG_BIG)
        sc_ref[pl.ds(r0, TK), :] = sc
        c_pos_, c_nn_, mx_, mn_ = stats
        c_pos_ = c_pos_ + jnp.sum(fold(jnp.where(sc > 0.0, 1.0, 0.0)), axis=0)
        c_nn_ = c_nn_ + jnp.sum(fold(jnp.where(sc >= 0.0, 1.0, 0.0)), axis=0)
        mx_ = jnp.maximum(mx_, jnp.max(fold(sc), axis=0))
        mn_ = jnp.minimum(mn_, jnp.min(fold(sc_min), axis=0))
        return c_pos_, c_nn_, mx_, mn_

    stat_ref[0] = jnp.zeros((8, TQ), F32)
    stat_ref[1] = jnp.zeros((8, TQ), F32)
    stat_ref[2] = jnp.full((8, TQ), NEG_BIG, F32)
    stat_ref[3] = jnp.full((8, TQ), -NEG_BIG, F32)

    def index_tiles(tiles):
        st = tuple(stat_ref[i] for i in range(4))
        for kb, diag in tiles:
            st = index_tile(kb, diag, st)
        for i in range(4):
            stat_ref[i] = st[i]

    def index_pairs(i, carry):
        index_tiles([(2 * i, False), (2 * i + 1, False)])
        return carry

    lax.fori_loop(0, j >> 1, index_pairs, 0)

    @pl.when((j & 1) == 1)
    def _():
        index_tiles([(j - 1, False), (j, True)])

    @pl.when((j & 1) == 0)
    def _():
        index_tiles([(j, True)])

    c_pos = jnp.sum(stat_ref[0], axis=0, keepdims=True).astype(I32)
    c_nn = jnp.sum(stat_ref[1], axis=0, keepdims=True).astype(I32)
    hi0 = jnp.max(stat_ref[2], axis=0, keepdims=True)
    mn = jnp.min(stat_ref[3], axis=0, keepdims=True)

    @pl.when((n_tiles & 1) == 1)
    def _():
        sc_ref[pl.ds(pl.multiple_of(n_tiles * TK, TK), TK), :] = jnp.full((TK, TQ), NEG_BIG, F32)

    n_pairs = (n_tiles + 1) >> 1

    def count_gt(thr):
        def body(kp, cnt):
            r0 = pl.multiple_of(kp * (2 * TK), 2 * TK)
            hit = jnp.where(sc_ref[pl.ds(r0, 2 * TK), :] > thr, 1, 0).astype(I32)
            return cnt + jnp.sum(hit.reshape(2 * TK // 32, 32, TQ), axis=0)
        cnt32 = lax.fori_loop(1, n_pairs, body, body(0, jnp.zeros((32, TQ), I32)))
        return jnp.sum(cnt32, axis=0, keepdims=True)

    t_glob = j * TQ + lax.broadcasted_iota(I32, (1, TQ), 1)
    k_eff = jnp.minimum(TOPK_MAX, t_glob + 1)

    zero = jnp.zeros((1, TQ), F32)
    all_in = (t_glob + 1) <= TOPK_MAX
    pos_ok = c_pos >= k_eff
    tie0 = (~all_in) & (~pos_ok) & (c_nn >= k_eff)
    lo_neg = jnp.maximum(mn + mn - 1.0, NEG_BIG)
    lo = jnp.where(all_in, NEG_BIG, jnp.where(pos_ok, zero, lo_neg))
    hi = jnp.where((~all_in) & (~pos_ok), zero, hi0)
    c_lo = jnp.where(pos_ok, c_pos, t_glob + 1)
    c_hi = jnp.where(pos_ok, 0, c_pos)
    fin0 = all_in | (pos_ok & (c_pos == k_eff)) | tie0

    def open_rows(fin_, c_lo_, c_hi_):
        return (fin_ == 0.0) & ((c_lo_ - c_hi_) > 2)

    def bis_cond(st):
        return (st[0] < BISECT_MAX_ITERS) & (st[1] > 0)

    def bis_body(st):
        it, _, lo_, hi_, c_lo_, c_hi_, fin_, tie_ = st
        go_n = (jnp.max(jnp.where(open_rows(fin_, c_lo_, c_hi_), 1.0, 0.0)) > 0.0).astype(I32)
        mid = 0.5 * lo_ + 0.5 * hi_
        collapsed = (mid <= lo_) | (mid >= hi_)
        cm = count_gt(mid)
        act = fin_ == 0.0
        up = act & (~collapsed) & (cm >= k_eff)
        dn = act & (~collapsed) & (cm < k_eff)
        tie_n = jnp.where(act & collapsed, 1.0, tie_)
        fin_n = jnp.where((act & collapsed) | (up & (cm == k_eff)), 1.0, fin_)
        return (it + 1, go_n, jnp.where(up, mid, lo_), jnp.where(dn, mid, hi_),
                jnp.where(up, cm, c_lo_), jnp.where(dn, cm, c_hi_), fin_n, tie_n)

    _, _, lo, hi, c_lo, c_hi, fin_f, tie_f = lax.while_loop(
        bis_cond, bis_body,
        (jnp.int32(0), jnp.int32(1), lo, hi, c_lo, c_hi, jnp.where(fin0, 1.0, 0.0), jnp.where(tie0, 1.0, 0.0)))

    def pick_two():
        def body(kp, carry):
            e_min, e_max = carry
            r0 = pl.multiple_of(kp * (2 * TK), 2 * TK)
            sc = sc_ref[pl.ds(r0, 2 * TK), :]
            above = jnp.where(sc > lo, sc, -NEG_BIG).reshape(2 * TK // 32, 32, TQ)
            below_hi = jnp.where(sc <= hi, sc, NEG_BIG).reshape(2 * TK // 32, 32, TQ)
            return jnp.minimum(e_min, jnp.min(above, axis=0)), jnp.maximum(e_max, jnp.max(below_hi, axis=0))
        init = (jnp.full((32, TQ), -NEG_BIG, F32), jnp.full((32, TQ), NEG_BIG, F32))
        e_min, e_max = lax.fori_loop(0, n_pairs, body, init)
        return jnp.min(e_min, axis=0, keepdims=True), jnp.max(e_max, axis=0, keepdims=True)

    e1, e2 = pick_two()
    left = fin_f == 0.0
    lo = jnp.where(left & (e1 < e2), e1, lo)
    hi = jnp.where(left & (e1 >= e2), e1, hi)
    tie_f = jnp.where(left & (e1 >= e2), 1.0, tie_f)
    tie = tie_f > 0.0
    theta_ref = m_ref
    theta_ref[0:1, :] = lo

    @pl.when(jnp.max(tie_f) > 0.0)
    def _():
        v = hi

        def pass_a(kb, carry):
            n_gt, below = carry
            r0 = pl.multiple_of(kb * TK, TK)
            sc = sc_ref[pl.ds(r0, TK), :]
            n_gt = n_gt + jnp.sum(jnp.where(sc > v, 1.0, 0.0), axis=0, keepdims=True)
            below = jnp.maximum(below, jnp.max(jnp.where(sc < v, sc, NEG_BIG), axis=0, keepdims=True))
            return n_gt, below

        n_gt, below = lax.fori_loop(
            0, n_tiles, pass_a, (jnp.zeros((1, TQ), F32), jnp.full((1, TQ), NEG_BIG, F32)))
        need = k_eff.astype(F32) - n_gt
        tri = jnp.where(lax.broadcasted_iota(I32, (TK, TK), 1) < lax.broadcasted_iota(I32, (TK, TK), 0),
                        1.0, 0.0).astype(BF16)

        def pass_b(kb, seen):
            r0 = pl.multiple_of(kb * TK, TK)
            sc = sc_ref[pl.ds(r0, TK), :]
            eq = (sc == v) & tie
            eqf = jnp.where(eq, 1.0, 0.0)
            rank = seen + _dot(tri, eqf.astype(BF16))
            sc_ref[pl.ds(r0, TK), :] = jnp.where(eq & (rank >= need), NEG_BIG, sc)
            return seen + jnp.sum(eqf, axis=0, keepdims=True)

        lax.fori_loop(0, n_tiles, pass_b, jnp.zeros((1, TQ), F32))
        theta_ref[0:1, :] = jnp.where(tie, below, lo)

    theta = theta_ref[0:1, :]

    m_ref[...] = jnp.full(m_ref.shape, LOGIT_NEG, F32)
    l_ref[...] = jnp.zeros(l_ref.shape, F32)
    acc_ref[...] = jnp.zeros(acc_ref.shape, F32)

    def tile_mask(kb):
        r0 = pl.multiple_of(kb * TK, TK)
        return jnp.where(sc_ref[pl.ds(r0, TK), :] > theta, 0.0, LOGIT_NEG)

    def logits_head(kb, slot, near, mb, h):
        r0 = pl.multiple_of(kb * TK, TK)
        p0 = (h // 2) * 2 * DH
        lg = _dot_nt(k_ref[pl.ds(r0, TK), p0:p0 + 2 * DH], qm_ref[h]) + mb
        if near is not None:
            lg = lg + bias_ref[near, h]
        lg_ref[slot, h] = lg
        m_old = m_ref[h:h + 1, :]
        m_new = jnp.maximum(m_old, jnp.max(lg, axis=0, keepdims=True))
        alpha_ref[slot, h:h + 1, :] = jnp.exp2(m_old - m_new)
        mtile_ref[slot, h:h + 1, :] = m_new
        m_ref[h:h + 1, :] = m_new

    def exp_head(slot, h):
        p_ref[slot, h] = jnp.exp2(lg_ref[slot, h] - mtile_ref[slot, h:h + 1, :]).astype(BF16)

    def pv_head(kb, slot, h):
        r0 = pl.multiple_of(kb * TK, TK)
        ones = jnp.ones((ONES_ROWS, TK), BF16)
        lhs = jnp.concatenate([vt_ref[h * DH:(h + 1) * DH, pl.ds(r0, TK)], ones], axis=0)
        pv = _dot(lhs, p_ref[slot, h])
        alpha = alpha_ref[slot, h:h + 1, :]
        acc_ref[h * DH:(h + 1) * DH, :] = alpha * acc_ref[h * DH:(h + 1) * DH, :] + pv[0:DH, :]
        l_ref[h:h + 1, :] = alpha * l_ref[h:h + 1, :] + pv[DH:DH + 1, :]

    def fill(kb, slot, near):
        mb = tile_mask(kb)
        for h in range(H):
            logits_head(kb, slot, near, mb, h)

    def drain(kb, slot):
        for h in range(H):
            exp_head(slot, h)
        for h in range(H):
            pv_head(kb, slot, h)

    def step(kb_prev, slot_prev, kb_next, slot_next, near_next):
        mb = tile_mask(kb_next)
        for h in range(H):
            exp_head(slot_prev, h)
            logits_head(kb_next, slot_next, near_next, mb, h)
            pv_head(kb_prev, slot_prev, h)

    n_far = j - 1

    @pl.when(n_far >= 1)
    def _():
        fill(0, 0, None)

        def far_loop(t, carry):
            step(2 * t, 0, 2 * t + 1, 1, None)
            step(2 * t + 1, 1, 2 * t + 2, 0, None)
            return carry

        lax.fori_loop(0, (n_far - 1) >> 1, far_loop, 0)

        @pl.when(((n_far - 1) & 1) == 1)
        def _():
            step(n_far - 2, 0, n_far - 1, 1, None)
            drain(n_far - 1, 1)

        @pl.when(((n_far - 1) & 1) == 0)
        def _():
            drain(n_far - 1, 0)

    @pl.when(j >= 1)
    def _():
        fill(j - 1, 0, 0)
        step(j - 1, 0, j, 1, 1)
        drain(j, 1)

    @pl.when(j == 0)
    def _():
        fill(j, 0, 1)
        drain(j, 0)

    for h in range(H):
        acc_ref[h * DH:(h + 1) * DH, :] = acc_ref[h * DH:(h + 1) * DH, :] / l_ref[h:h + 1, :]
    out_ref[...] = (acc_ref[...].T * az_ref[...]).astype(BF16)


def _dsa(rel_bias, iq, sm, aq, az, ik2, ak, avt, batch, seq):
    TQ = Q_TILE
    nq = seq // TQ
    width = A_HEADS * A_DH
    maps = jnp.asarray(_bucket_maps())
    qrow = lambda b, j: (b * nq + j, 0)
    return pl.pallas_call(
        _dsa_body,
        grid=(batch, nq),
        in_specs=[
            pl.BlockSpec(memory_space=pltpu.SMEM),
            pl.BlockSpec((TQ, width), qrow),
            pl.BlockSpec((TQ, 128), qrow),
            pl.BlockSpec((TQ, width), qrow),
            pl.BlockSpec((TQ, width), qrow),
            pl.BlockSpec((seq, 128), lambda b, j: (b, 0), pipeline_mode=pl.Buffered(1)),
            pl.BlockSpec((seq, width), lambda b, j: (b, 0), pipeline_mode=pl.Buffered(1)),
            pl.BlockSpec((None, width, seq), lambda b, j: (b, 0, 0), pipeline_mode=pl.Buffered(1)),
            _resident(maps.shape),
        ],
        out_specs=pl.BlockSpec((TQ, width), qrow),
        out_shape=jax.ShapeDtypeStruct((batch * seq, width), BF16),
        scratch_shapes=[
            pltpu.VMEM((seq, TQ), F32),
            pltpu.VMEM((4, 8, TQ), F32),
            pltpu.VMEM((A_HEADS, TQ, 2 * A_DH), BF16),
            pltpu.VMEM((A_HEADS, TQ, 2 * A_DH), BF16),
            pltpu.VMEM((2, A_HEADS, K_TILE, TQ), F32),
            pltpu.VMEM((A_HEADS * A_DH, TQ), F32),
            pltpu.VMEM((8, TQ), F32),
            pltpu.VMEM((8, TQ), F32),
            pltpu.VMEM((2, 8, TQ), F32),
            pltpu.VMEM((2, 8, TQ), F32),
            pltpu.VMEM((2, A_HEADS, K_TILE, TQ), F32),
            pltpu.VMEM((2, A_HEADS, K_TILE, TQ), BF16),
        ],
        compiler_params=pltpu.CompilerParams(
            dimension_semantics=("arbitrary", "arbitrary"), vmem_limit_bytes=VMEM_LIMIT),
        name="dsa",
    )(rel_bias, iq, sm, aq, az, ik2, ak, avt, maps)


def _outproj_body(x_ref, hm_ref, ha_ref, p_ref, wom_ref, woa_ref, wple_ref, wg_ref, g_ref, o_ref):
    h1 = x_ref[...] + _dot(hm_ref[...], wom_ref[...]) + _dot(ha_ref[...], woa_ref[...])
    gate = jax.nn.sigmoid(_dot(h1.astype(BF16), wg_ref[...]))
    h2 = h1 + _dot(p_ref[...].astype(BF16), wple_ref[...]) * gate
    ms = jnp.mean(h2 * h2, axis=-1, keepdims=True)
    o_ref[...] = h2 * lax.rsqrt(ms + EPS) * g_ref[...]


def _outproj(x2, hm, ha, p2, wom, woa, wple, wg, g):
    rows, d = x2.shape
    tm = ROW_TILE
    tile = lambda n: pl.BlockSpec((tm, n), lambda i: (i, 0))
    return pl.pallas_call(
        _outproj_body,
        grid=(rows // tm,),
        in_specs=[tile(d), tile(hm.shape[1]), tile(ha.shape[1]), tile(p2.shape[1]),
                  _resident(wom.shape), _resident(woa.shape), _resident(wple.shape),
                  _resident(wg.shape), _resident(g.shape)],
        out_specs=tile(d),
        out_shape=jax.ShapeDtypeStruct((rows, d), F32),
        compiler_params=pltpu.CompilerParams(
            dimension_semantics=("arbitrary",), vmem_limit_bytes=VMEM_LIMIT),
        name="outproj",
    )(x2, hm, ha, p2, wom, woa, wple, wg, g)


def kernel(x, p, w_in, b_gate, conv_w, w_out, norm_in, m_norm, rel_bias, w_ple, w_ple_gate, norm_final):
    batch, seq, d = x.shape
    assert w_in.shape[0] == 1 and p.shape[0] == 1, "single-layer block"
    assert seq % Q_TILE == 0 and seq % M_CHUNK == 0 and (batch * seq) % ROW_TILE == 0
    rows = batch * seq
    x2 = x.reshape(rows, d)

    sizes = [M_HEADS * M_QK, M_HEADS * M_QK, M_HEADS * M_V, M_HEADS * M_V, M_HEADS * M_V, 2 * M_HEADS,
             A_HEADS * A_DH, A_HEADS * A_DH, A_HEADS * A_DH, A_HEADS * A_DH,
             IDX_HEADS * IDX_DH, IDX_DH, IDX_HEADS]
    offs = np.concatenate([[0], np.cumsum(sizes)])
    wi = w_in[0]
    col = lambda i: wi[:, int(offs[i]):int(offs[i + 1])]
    w_if, w_ixk, w_ixw = col(5), col(11), col(12)
    zeros = lambda n: jnp.zeros((d, n), wi.dtype)
    w_sm = jnp.concatenate([w_if[:, :M_HEADS], zeros(4), w_if[:, M_HEADS:], zeros(4), w_ixw, zeros(104)], axis=1)
    w_ik2 = jnp.concatenate([w_ixk, w_ixk], axis=1)
    ws = [jnp.concatenate([col(0), col(1)], axis=1), col(2), col(3), col(4), w_sm, w_ik2,
          col(6), col(7), col(8), col(9), col(10)]
    ws = [w.astype(BF16) for w in ws]
    bg = b_gate[0]
    bpad = jnp.concatenate([bg[:M_HEADS], jnp.zeros((4,), F32), bg[M_HEADS:], jnp.zeros((116,), F32)])[None, :]

    mqk, mv, gate, sm, ik2, aq, ak, avt, az, iq = _inproj(x2, norm_in[0][None, :], ws, batch, seq)

    hm = _mlstm(mqk, mv, gate, sm, conv_w[0], bpad, m_norm[0][None, :], batch, seq)

    ha = _dsa(rel_bias, iq, sm, aq, az, ik2, ak, avt, batch, seq)

    wo = w_out[0].astype(BF16)
    out = _outproj(x2, hm, ha, p[0].reshape(rows, -1), wo[:M_HEADS * M_V], wo[M_HEADS * M_V:],
                   w_ple[0].astype(BF16), w_ple_gate[0].astype(BF16), norm_final[None, :])
    return out.reshape(batch, seq, d)
```

```python
import functools
import math

import numpy as np
import jax
import jax.numpy as jnp
from jax import lax
from jax.experimental import pallas as pl
from jax.experimental.pallas import tpu as pltpu

F32 = jnp.float32
BF16 = jnp.bfloat16
I32 = jnp.int32

M_HEADS = 4
M_QK = 128
M_V = 256
CONV_W = 4
A_HEADS = 8
A_DH = 64
IDX_HEADS = 8
IDX_DH = 64
TOPK_MAX = 256
REL_BUCKETS = 32
REL_MAX_DIST = 128
EPS = 1e-6

LOG2E = 1.4426950408889634
NEG_BIG = -3.0e38
LOGIT_NEG = -1.0e30

ROW_TILE = 512
M_CHUNK = 256
Q_TILE = 256
K_TILE = 256
ONES_ROWS = 16
CONV_HALO = 8
LANES = 128
GATE_I_LANE = 0
GATE_F_LANE = 8
IDX_W_LANE = 16
assert GATE_I_LANE == 0
BISECT_MAX_ITERS = 256
VMEM_LIMIT = 56 * 1024 * 1024


def _silu(v):
    return v * jax.nn.sigmoid(v)


def _dot(a, b):
    return jnp.dot(a, b, preferred_element_type=F32)


def _dot_nt(a, b):
    return lax.dot_general(a, b, (((1,), (1,)), ((), ())), preferred_element_type=F32)


def _dot_tn(a, b):
    return lax.dot_general(a, b, (((0,), (0,)), ((), ())), preferred_element_type=F32)


def _resident(shape):
    nd = len(shape)
    return pl.BlockSpec(shape, lambda *_: (0,) * nd, pipeline_mode=pl.Buffered(1))


def _inproj_body(x_ref, g_ref, w_mqk, w_mv, w_mo, w_mz, w_sm, w_ik, w_aq, w_ak, w_av, w_az, w_iq,
                 o_mqk, o_mv, o_gate, o_sm, o_ik, o_aq, o_ak, o_avt, o_az, o_iq):
    x = x_ref[...]
    ms = jnp.mean(x * x, axis=-1, keepdims=True)
    xn = (x * lax.rsqrt(ms + EPS) * g_ref[...]).astype(BF16)
    o_mqk[...] = _dot(xn, w_mqk[...])
    o_mv[...] = _dot(xn, w_mv[...]).astype(BF16)
    o_gate[...] = jax.nn.sigmoid(_dot(xn, w_mo[...])) * _silu(_dot(xn, w_mz[...]))
    sm = _dot(xn, w_sm[...])
    o_sm[...] = sm
    o_ik[...] = _dot(xn, w_ik[...]).astype(BF16)
    o_aq[...] = (_dot(xn, w_aq[...]) * (A_DH ** -0.5 * LOG2E)).astype(BF16)
    o_ak[...] = _dot(xn, w_ak[...]).astype(BF16)
    o_avt[...] = _dot(xn, w_av[...]).T.astype(BF16)
    o_az[...] = _silu(_dot(xn, w_az[...]))
    iq = _dot(xn, w_iq[...])
    wsc = sm[:, IDX_W_LANE:IDX_W_LANE + IDX_HEADS] * (IDX_HEADS ** -0.5 * IDX_DH ** -0.5)
    lane = lax.broadcasted_iota(I32, (x.shape[0], 2 * IDX_DH), 1)
    for pair in range(IDX_HEADS // 2):
        wpair = jnp.where(lane < IDX_DH, wsc[:, 2 * pair:2 * pair + 1], wsc[:, 2 * pair + 1:2 * pair + 2])
        sl = slice(pair * 2 * IDX_DH, (pair + 1) * 2 * IDX_DH)
        o_iq[:, sl] = (iq[:, sl] * wpair).astype(BF16)


def _inproj(x2, g, ws, batch, seq):
    rows, d = x2.shape
    tm = ROW_TILE
    per_seq = seq // tm
    out_dtypes = [F32, BF16, F32, F32, BF16, BF16, BF16, BF16, F32, BF16]
    widths = [w.shape[1] for w in ws]
    out_widths = [widths[0], widths[1], widths[2]] + widths[4:]
    in_specs = [pl.BlockSpec((tm, d), lambda i: (i, 0)), _resident((1, d))]
    in_specs += [_resident(w.shape) for w in ws]
    out_specs = [pl.BlockSpec((tm, n), lambda i: (i, 0)) for n in out_widths]
    out_shape = [jax.ShapeDtypeStruct((rows, n), dt) for n, dt in zip(out_widths, out_dtypes)]
    av_pos = 7
    out_specs[av_pos] = pl.BlockSpec((None, out_widths[av_pos], tm), lambda i: (i // per_seq, 0, i % per_seq))
    out_shape[av_pos] = jax.ShapeDtypeStruct((batch, out_widths[av_pos], seq), BF16)
    return pl.pallas_call(
        _inproj_body,
        grid=(rows // tm,),
        in_specs=in_specs,
        out_specs=out_specs,
        out_shape=out_shape,
        compiler_params=pltpu.CompilerParams(
            dimension_semantics=("arbitrary",), vmem_limit_bytes=VMEM_LIMIT),
        name="inproj",
    )(x2, g, *ws)


def _row_scan(v, op, fill):
    n = v.shape[0]
    row = lax.broadcasted_iota(I32, v.shape, 0)
    sh = 1
    while sh < n:
        v = op(v, jnp.where(row >= sh, pltpu.roll(v, sh, 0), fill))
        sh *= 2
    return v


def _mlstm_body(mqk_ref, mv_ref, gate_ref, sm_ref, convw_ref, bpad_ref, mnorm_ref, out_ref,
                ext_ref, c_ref, n_ref, m_ref):
    L = M_CHUNK
    H, DK, DV = M_HEADS, M_QK, M_V

    @pl.when(pl.program_id(1) == 0)
    def _():
        ext_ref[0:CONV_HALO, :] = jnp.zeros((CONV_HALO, 2 * H * DK), F32)
        c_ref[...] = jnp.zeros(c_ref.shape, F32)
        n_ref[...] = jnp.zeros(n_ref.shape, F32)
        m_ref[...] = jnp.zeros(m_ref.shape, F32)

    raw = mqk_ref[...]
    ext_ref[CONV_HALO:CONV_HALO + L, :] = raw
    base = CONV_HALO - (CONV_W - 1)
    y = ext_ref[base:base + L, :] * convw_ref[0:1, :]
    for j in range(1, CONV_W):
        y = y + ext_ref[base + j:base + j + L, :] * convw_ref[j:j + 1, :]
    ext_ref[0:CONV_HALO, :] = raw[L - CONV_HALO:L, :]
    qk = _silu(y)

    g = sm_ref[...] + bpad_ref[...]
    lf = jax.nn.log_sigmoid(pltpu.roll(g, LANES - (GATE_F_LANE - GATE_I_LANE), 1))
    b = _row_scan(lf, jnp.add, 0.0)
    a = g - b
    m_prev = m_ref[0:1, :]
    mcol = jnp.maximum(m_prev, _row_scan(a, jnp.maximum, -jnp.inf))
    w_inter = jnp.exp(m_prev - mcol)
    e_neg_mt = jnp.exp(-(b + mcol))
    m_last = mcol[L - 1:L, :]
    wk = jnp.exp(a - m_last)
    decay = jnp.exp(m_prev - m_last)
    m_new = b[L - 1:L, :] + m_last
    a_rows = a.T

    ri = lax.broadcasted_iota(I32, (L, L), 0)
    ci = lax.broadcasted_iota(I32, (L, L), 1)
    causal = ci <= ri

    heads = range(H)
    q = [qk[:, h * DK:(h + 1) * DK] for h in heads]
    k = [qk[:, (H + h) * DK:(H + h + 1) * DK] * (DK ** -0.5) for h in heads]
    qb = [q[h].astype(BF16) for h in heads]
    v = [mv_ref[:, h * DV:(h + 1) * DV] for h in heads]
    c_old = [c_ref[h] for h in heads]
    qkt = [_dot_nt(qb[h], k[h].astype(BF16)) for h in heads]
    q_c = [_dot(qb[h], c_old[h].astype(BF16)) for h in heads]
    s = [qkt[h] * jnp.where(causal, jnp.exp(a_rows[h:h + 1, :] - mcol[:, h:h + 1]), 0.0) for h in heads]
    num = [_dot(s[h].astype(BF16), v[h]) + w_inter[:, h:h + 1] * q_c[h] for h in heads]
    kw = [k[h] * wk[:, h:h + 1] for h in heads]
    upd = [_dot_tn(kw[h].astype(BF16), v[h]) for h in heads]
    for h in heads:
        den = (jnp.sum(s[h], axis=1, keepdims=True)
               + w_inter[:, h:h + 1] * jnp.sum(q[h] * n_ref[h:h + 1, :], axis=1, keepdims=True))
        den = jnp.maximum(jnp.abs(den), e_neg_mt[:, h:h + 1])
        hh = num[h] / den
        hn = hh * lax.rsqrt(jnp.mean(hh * hh, axis=-1, keepdims=True) + EPS)
        hn = hn * mnorm_ref[:, h * DV:(h + 1) * DV]
        out_ref[:, h * DV:(h + 1) * DV] = (gate_ref[:, h * DV:(h + 1) * DV] * hn).astype(BF16)
    for h in heads:
        dec = decay[:, h:h + 1]
        c_ref[h] = dec * c_old[h] + upd[h]
        n_ref[h:h + 1, :] = dec * n_ref[h:h + 1, :] + jnp.sum(kw[h], axis=0, keepdims=True)
    m_ref[...] = jnp.broadcast_to(m_new, m_ref.shape)


def _mlstm(mqk, mv, gate, sm, conv_w, bpad, mnorm, batch, seq):
    L = M_CHUNK
    nc = seq // L
    wqk = mqk.shape[1]
    wv = mv.shape[1]
    row = lambda b, c: (b * nc + c, 0)
    return pl.pallas_call(
        _mlstm_body,
        grid=(batch, nc),
        in_specs=[
            pl.BlockSpec((L, wqk), row),
            pl.BlockSpec((L, wv), row),
            pl.BlockSpec((L, wv), row),
            pl.BlockSpec((L, 128), row),
            _resident(conv_w.shape),
            _resident(bpad.shape),
            _resident(mnorm.shape),
        ],
        out_specs=pl.BlockSpec((L, wv), row),
        out_shape=jax.ShapeDtypeStruct((batch * seq, wv), BF16),
        scratch_shapes=[
            pltpu.VMEM((L + CONV_HALO, wqk), F32),
            pltpu.VMEM((M_HEADS, M_QK, M_V), F32),
            pltpu.VMEM((8, M_QK), F32),
            pltpu.VMEM((8, 128), F32),
        ],
        compiler_params=pltpu.CompilerParams(
            dimension_semantics=("arbitrary", "arbitrary"), vmem_limit_bytes=VMEM_LIMIT),
        name="mlstm",
    )(mqk, mv, gate, sm, conv_w, bpad, mnorm)


def _t5_bucket_table(max_dist):
    d = np.arange(max_dist)
    max_exact = REL_BUCKETS // 2
    tabs = []
    for dt in (np.float32, np.float64):
        ratio = np.maximum(d, 1).astype(dt) / dt(max_exact)
        large = max_exact + (np.log(ratio) / dt(math.log(REL_MAX_DIST / max_exact))
                             * dt(REL_BUCKETS - max_exact)).astype(np.int32)
        large = np.minimum(large, REL_BUCKETS - 1)
        tabs.append(np.where(d < max_exact, d, large).astype(np.int32))
    assert np.array_equal(tabs[0], tabs[1])
    return tabs[0]


def _bucket_maps():
    tab = _t5_bucket_table(Q_TILE + K_TILE)
    s = np.arange(K_TILE)[:, None]
    t = np.arange(Q_TILE)[None, :]
    diag = tab[np.maximum(t - s, 0)]
    prev = tab[K_TILE + t - s]
    assert np.all(tab[K_TILE:] == REL_BUCKETS - 1)
    return np.stack([prev, diag]).astype(np.int32)


def _dsa_body(rb_ref, iq_ref, sm_ref, aq_ref, az_ref, ik_ref, k_ref, vt_ref, map_ref, out_ref,
              sc_ref, stat_ref, iqm_ref, qm_ref, bias_ref, acc_ref, m_ref, l_ref, alpha_ref, mtile_ref, lg_ref,
              p_ref):
    TQ, TK, H, DH = Q_TILE, K_TILE, A_HEADS, A_DH
    j = pl.program_id(1)
    n_tiles = j + 1

    @pl.when((pl.program_id(0) == 0) & (j == 0))
    def _():
        bias_ref[...] = jnp.zeros(bias_ref.shape, F32)

        def fill(bkt, carry):
            for which in range(2):
                hit = map_ref[which] == bkt
                for h in range(H):
                    val = (rb_ref[bkt, h] - rb_ref[REL_BUCKETS - 1, h]) * LOG2E
                    bias_ref[which, h] = jnp.where(hit, val, bias_ref[which, h])
            return carry

        lax.fori_loop(0, REL_BUCKETS - 1, fill, 0)

    lane = lax.broadcasted_iota(I32, (TQ, 2 * DH), 1)
    for h in range(H):
        p0 = (h // 2) * 2 * DH
        keep = (lane < DH) if h % 2 == 0 else (lane >= DH)
        iqm_ref[h] = jnp.where(keep, iq_ref[:, p0:p0 + 2 * DH], jnp.zeros((), BF16))
        qm_ref[h] = jnp.where(keep, aq_ref[:, p0:p0 + 2 * DH], jnp.zeros((), BF16))
    w_rows = sm_ref[...].T[IDX_W_LANE:IDX_W_LANE + IDX_HEADS, :]
    clamp_lo = jnp.where(w_rows >= 0.0, 0.0, -jnp.inf)
    clamp_hi = jnp.where(w_rows >= 0.0, jnp.inf, 0.0)

    s_loc = lax.broadcasted_iota(I32, (TK, TQ), 0)
    t_loc = lax.broadcasted_iota(I32, (TK, TQ), 1)

    fold = lambda v: v.reshape(v.shape[0] // 8, 8, TQ)

    def index_tile(kb, diag, stats):
        r0 = pl.multiple_of(kb * TK, TK)
        lhs = ik_ref[pl.ds(r0, TK), :]
        sc = None
        for h in range(H):
            y = _dot_nt(lhs, iqm_ref[h])
            term = jnp.minimum(jnp.maximum(y, clamp_lo[h:h + 1, :]), clamp_hi[h:h + 1, :])
            sc = term if sc is None else sc + term
        sc_min = sc
        if diag:
            admissible = s_loc <= t_loc
            sc_min = jnp.where(admissible, sc, -NEG_BIG)
            sc = jnp.where(admissible, sc, NEG_BIG)
        sc_ref[pl.ds(r0, TK), :] = sc
        c_pos_, c_nn_, mx_, mn_ = stats
        c_pos_ = c_pos_ + jnp.sum(fold(jnp.where(sc > 0.0, 1.0, 0.0)), axis=0)
        c_nn_ = c_nn_ + jnp.sum(fold(jnp.where(sc >= 0.0, 1.0, 0.0)), axis=0)
        mx_ = jnp.maximum(mx_, jnp.max(fold(sc), axis=0))
        mn_ = jnp.minimum(mn_, jnp.min(fold(sc_min), axis=0))
        return c_pos_, c_nn_, mx_, mn_

    stat_ref[0] = jnp.zeros((8, TQ), F32)
    stat_ref[1] = jnp.zeros((8, TQ), F32)
    stat_ref[2] = jnp.full((8, TQ), NEG_BIG, F32)
    stat_ref[3] = jnp.full((8, TQ), -NEG_BIG, F32)

    def index_tiles(tiles):
        st = tuple(stat_ref[i] for i in range(4))
        for kb, diag in tiles:
            st = index_tile(kb, diag, st)
        for i in range(4):
            stat_ref[i] = st[i]

    def index_pairs(i, carry):
        index_tiles([(2 * i, False), (2 * i + 1, False)])
        return carry

    lax.fori_loop(0, j >> 1, index_pairs, 0)

    @pl.when((j & 1) == 1)
    def _():
        index_tiles([(j - 1, False), (j, True)])

    @pl.when((j & 1) == 0)
    def _():
        index_tiles([(j, True)])

    c_pos = jnp.sum(stat_ref[0], axis=0, keepdims=True).astype(I32)
    c_nn = jnp.sum(stat_ref[1], axis=0, keepdims=True).astype(I32)
    hi0 = jnp.max(stat_ref[2], axis=0, keepdims=True)
    mn = jnp.min(stat_ref[3], axis=0, keepdims=True)

    @pl.when((n_tiles & 1) == 1)
    def _():
        sc_ref[pl.ds(pl.multiple_of(n_tiles * TK, TK), TK), :] = jnp.full((TK, TQ), NEG_BIG, F32)

    n_pairs = (n_tiles + 1) >> 1

    def count_gt(thr):
        def body(kp, cnt):
            r0 = pl.multiple_of(kp * (2 * TK), 2 * TK)
            hit = jnp.where(sc_ref[pl.ds(r0, 2 * TK), :] > thr, 1, 0).astype(I32)
            return cnt + jnp.sum(hit.reshape(2 * TK // 32, 32, TQ), axis=0)
        cnt32 = lax.fori_loop(1, n_pairs, body, body(0, jnp.zeros((32, TQ), I32)))
        return jnp.sum(cnt32, axis=0, keepdims=True)

    t_glob = j * TQ + lax.broadcasted_iota(I32, (1, TQ), 1)
    k_eff = jnp.minimum(TOPK_MAX, t_glob + 1)

    zero = jnp.zeros((1, TQ), F32)
    all_in = (t_glob + 1) <= TOPK_MAX
    pos_ok = c_pos >= k_eff
    tie0 = (~all_in) & (~pos_ok) & (c_nn >= k_eff)
    lo_neg = jnp.maximum(mn + mn - 1.0, NEG_BIG)
    lo = jnp.where(all_in, NEG_BIG, jnp.where(pos_ok, zero, lo_neg))
    hi = jnp.where((~all_in) & (~pos_ok), zero, hi0)
    c_lo = jnp.where(pos_ok, c_pos, t_glob + 1)
    c_hi = jnp.where(pos_ok, 0, c_pos)
    fin0 = all_in | (pos_ok & (c_pos == k_eff)) | tie0

    def open_rows(fin_, c_lo_, c_hi_):
        return (fin_ == 0.0) & ((c_lo_ - c_hi_) > 2)

    def bis_cond(st):
        return (st[0] < BISECT_MAX_ITERS) & (st[1] > 0)

    def bis_body(st):
        it, _, lo_, hi_, c_lo_, c_hi_, fin_, tie_ = st
        go_n = (jnp.max(jnp.where(open_rows(fin_, c_lo_, c_hi_), 1.0, 0.0)) > 0.0).astype(I32)
        mid = 0.5 * lo_ + 0.5 * hi_
        collapsed = (mid <= lo_) | (mid >= hi_)
        cm = count_gt(mid)
        act = fin_ == 0.0
        up = act & (~collapsed) & (cm >= k_eff)
        dn = act & (~collapsed) & (cm < k_eff)
        tie_n = jnp.where(act & collapsed, 1.0, tie_)
        fin_n = jnp.where((act & collapsed) | (up & (cm == k_eff)), 1.0, fin_)
        return (it + 1, go_n, jnp.where(up, mid, lo_), jnp.where(dn, mid, hi_),
                jnp.where(up, cm, c_lo_), jnp.where(dn, cm, c_hi_), fin_n, tie_n)

    _, _, lo, hi, c_lo, c_hi, fin_f, tie_f = lax.while_loop(
        bis_cond, bis_body,
        (jnp.int32(0), jnp.int32(1), lo, hi, c_lo, c_hi, jnp.where(fin0, 1.0, 0.0), jnp.where(tie0, 1.0, 0.0)))

    def pick_two():
        def body(kp, carry):
            e_min, e_max = carry
            r0 = pl.multiple_of(kp * (2 * TK), 2 * TK)
            sc = sc_ref[pl.ds(r0, 2 * TK), :]
            above = jnp.where(sc > lo, sc, -NEG_BIG).reshape(2 * TK // 32, 32, TQ)
            below_hi = jnp.where(sc <= hi, sc, NEG_BIG).reshape(2 * TK // 32, 32, TQ)
            return jnp.minimum(e_min, jnp.min(above, axis=0)), jnp.maximum(e_max, jnp.max(below_hi, axis=0))
        init = (jnp.full((32, TQ), -NEG_BIG, F32), jnp.full((32, TQ), NEG_BIG, F32))
        e_min, e_max = lax.fori_loop(0, n_pairs, body, init)
        return jnp.min(e_min, axis=0, keepdims=True), jnp.max(e_max, axis=0, keepdims=True)

    e1, e2 = pick_two()
    left = fin_f == 0.0
    lo = jnp.where(left & (e1 < e2), e1, lo)
    hi = jnp.where(left & (e1 >= e2), e1, hi)
    tie_f = jnp.where(left & (e1 >= e2), 1.0, tie_f)
    tie = tie_f > 0.0
    theta_ref = m_ref
    theta_ref[0:1, :] = lo

    @pl.when(jnp.max(tie_f) > 0.0)
    def _():
        v = jnp.where(tie, hi, jnp.inf)
        need = (k_eff - c_hi).astype(F32)
        tri = jnp.where(lax.broadcasted_iota(I32, (TK, TK), 1) < lax.broadcasted_iota(I32, (TK, TK), 0),
                        1.0, 0.0).astype(BF16)

        def tie_pair(kp, seen):
            for half in range(2):
                r0 = pl.multiple_of(kp * (2 * TK) + half * TK, TK)
                sc = sc_ref[pl.ds(r0, TK), :]
                eqf = jnp.where(sc == v, 1.0, 0.0)
                rank = seen + _dot(tri, eqf.astype(BF16))
                sc_ref[pl.ds(r0, TK), :] = jnp.where(eqf * (need - rank) > 0.0, -NEG_BIG, sc)
                seen = seen + jnp.sum(eqf, axis=0, keepdims=True)
            return seen

        lax.fori_loop(0, n_pairs, tie_pair, jnp.zeros((1, TQ), F32))
        theta_ref[0:1, :] = jnp.where(tie, hi, lo)

    theta = theta_ref[0:1, :]

    m_ref[...] = jnp.full(m_ref.shape, LOGIT_NEG, F32)
    l_ref[...] = jnp.zeros(l_ref.shape, F32)
    acc_ref[...] = jnp.zeros(acc_ref.shape, F32)

    def tile_mask(kb):
        r0 = pl.multiple_of(kb * TK, TK)
        return jnp.where(sc_ref[pl.ds(r0, TK), :] > theta, 0.0, LOGIT_NEG)

    def logits_head(kb, slot, near, mb, h):
        r0 = pl.multiple_of(kb * TK, TK)
        p0 = (h // 2) * 2 * DH
        lg = _dot_nt(k_ref[pl.ds(r0, TK), p0:p0 + 2 * DH], qm_ref[h]) + mb
        if near is not None:
            lg = lg + bias_ref[near, h]
        lg_ref[slot, h] = lg
        m_old = m_ref[h:h + 1, :]
        m_new = jnp.maximum(m_old, jnp.max(lg, axis=0, keepdims=True))
        alpha_ref[slot, h:h + 1, :] = jnp.exp2(m_old - m_new)
        mtile_ref[slot, h:h + 1, :] = m_new
        m_ref[h:h + 1, :] = m_new

    def exp_head(slot, h):
        p_ref[slot, h] = jnp.exp2(lg_ref[slot, h] - mtile_ref[slot, h:h + 1, :]).astype(BF16)

    def pv_head(kb, slot, h):
        r0 = pl.multiple_of(kb * TK, TK)
        ones = jnp.ones((ONES_ROWS, TK), BF16)
        lhs = jnp.concatenate([vt_ref[h * DH:(h + 1) * DH, pl.ds(r0, TK)], ones], axis=0)
        pv = _dot(lhs, p_ref[slot, h])
        alpha = alpha_ref[slot, h:h + 1, :]
        acc_ref[h * DH:(h + 1) * DH, :] = alpha * acc_ref[h * DH:(h + 1) * DH, :] + pv[0:DH, :]
        l_ref[h:h + 1, :] = alpha * l_ref[h:h + 1, :] + pv[DH:DH + 1, :]

    def fill(kb, slot, near):
        mb = tile_mask(kb)
        for h in range(H):
            logits_head(kb, slot, near, mb, h)

    def drain(kb, slot):
        for h in range(H):
            exp_head(slot, h)
        for h in range(H):
            pv_head(kb, slot, h)

    def step(kb_prev, slot_prev, kb_next, slot_next, near_next):
        mb = tile_mask(kb_next)
        for h in range(H):
            exp_head(slot_prev, h)
            logits_head(kb_next, slot_next, near_next, mb, h)
            pv_head(kb_prev, slot_prev, h)

    n_far = j - 1

    @pl.when(n_far >= 1)
    def _():
        fill(0, 0, None)

        def far_loop(t, carry):
            step(2 * t, 0, 2 * t + 1, 1, None)
            step(2 * t + 1, 1, 2 * t + 2, 0, None)
            return carry

        lax.fori_loop(0, (n_far - 1) >> 1, far_loop, 0)

        @pl.when(((n_far - 1) & 1) == 1)
        def _():
            step(j - 3, 0, j - 2, 1, None)
            step(j - 2, 1, j - 1, 0, 0)
            step(j - 1, 0, j, 1, 1)
            drain(j, 1)

        @pl.when(((n_far - 1) & 1) == 0)
        def _():
            step(j - 2, 0, j - 1, 1, 0)
            step(j - 1, 1, j, 0, 1)
            drain(j, 0)

    @pl.when(j == 1)
    def _():
        fill(0, 0, 0)
        step(0, 0, 1, 1, 1)
        drain(1, 1)

    @pl.when(j == 0)
    def _():
        fill(0, 0, 1)
        drain(0, 0)

    for h in range(H):
        acc_ref[h * DH:(h + 1) * DH, :] = acc_ref[h * DH:(h + 1) * DH, :] / l_ref[h:h + 1, :]
    out_ref[...] = (acc_ref[...].T * az_ref[...]).astype(BF16)


def _dsa(rel_bias, iq, sm, aq, az, ik2, ak, avt, batch, seq):
    TQ = Q_TILE
    nq = seq // TQ
    width = A_HEADS * A_DH
    maps = jnp.asarray(_bucket_maps())
    qrow = lambda b, j: (b * nq + j, 0)
    return pl.pallas_call(
        _dsa_body,
        grid=(batch, nq),
        in_specs=[
            pl.BlockSpec(memory_space=pltpu.SMEM),
            pl.BlockSpec((TQ, width), qrow),
            pl.BlockSpec((TQ, 128), qrow),
            pl.BlockSpec((TQ, width), qrow),
            pl.BlockSpec((TQ, width), qrow),
            pl.BlockSpec((seq, 128), lambda b, j: (b, 0), pipeline_mode=pl.Buffered(1)),
            pl.BlockSpec((seq, width), lambda b, j: (b, 0), pipeline_mode=pl.Buffered(1)),
            pl.BlockSpec((None, width, seq), lambda b, j: (b, 0, 0), pipeline_mode=pl.Buffered(1)),
            _resident(maps.shape),
        ],
        out_specs=pl.BlockSpec((TQ, width), qrow),
        out_shape=jax.ShapeDtypeStruct((batch * seq, width), BF16),
        scratch_shapes=[
            pltpu.VMEM((seq, TQ), F32),
            pltpu.VMEM((4, 8, TQ), F32),
            pltpu.VMEM((A_HEADS, TQ, 2 * A_DH), BF16),
            pltpu.VMEM((A_HEADS, TQ, 2 * A_DH), BF16),
            pltpu.VMEM((2, A_HEADS, K_TILE, TQ), F32),
            pltpu.VMEM((A_HEADS * A_DH, TQ), F32),
            pltpu.VMEM((8, TQ), F32),
            pltpu.VMEM((8, TQ), F32),
            pltpu.VMEM((2, 8, TQ), F32),
            pltpu.VMEM((2, 8, TQ), F32),
            pltpu.VMEM((2, A_HEADS, K_TILE, TQ), F32),
            pltpu.VMEM((2, A_HEADS, K_TILE, TQ), BF16),
        ],
        compiler_params=pltpu.CompilerParams(
            dimension_semantics=("arbitrary", "arbitrary"), vmem_limit_bytes=VMEM_LIMIT),
        name="dsa",
    )(rel_bias, iq, sm, aq, az, ik2, ak, avt, maps)


def _outproj_body(x_ref, hm_ref, ha_ref, p_ref, wom_ref, woa_ref, wple_ref, wg_ref, g_ref, o_ref):
    h1 = x_ref[...] + _dot(hm_ref[...], wom_ref[...]) + _dot(ha_ref[...], woa_ref[...])
    gate = jax.nn.sigmoid(_dot(h1.astype(BF16), wg_ref[...]))
    h2 = h1 + _dot(p_ref[...].astype(BF16), wple_ref[...]) * gate
    ms = jnp.mean(h2 * h2, axis=-1, keepdims=True)
    o_ref[...] = h2 * lax.rsqrt(ms + EPS) * g_ref[...]


def _outproj(x2, hm, ha, p2, wom, woa, wple, wg, g):
    rows, d = x2.shape
    tm = ROW_TILE
    tile = lambda n: pl.BlockSpec((tm, n), lambda i: (i, 0))
    return pl.pallas_call(
        _outproj_body,
        grid=(rows // tm,),
        in_specs=[tile(d), tile(hm.shape[1]), tile(ha.shape[1]), tile(p2.shape[1]),
                  _resident(wom.shape), _resident(woa.shape), _resident(wple.shape),
                  _resident(wg.shape), _resident(g.shape)],
        out_specs=tile(d),
        out_shape=jax.ShapeDtypeStruct((rows, d), F32),
        compiler_params=pltpu.CompilerParams(
            dimension_semantics=("arbitrary",), vmem_limit_bytes=VMEM_LIMIT),
        name="outproj",
    )(x2, hm, ha, p2, wom, woa, wple, wg, g)


def kernel(x, p, w_in, b_gate, conv_w, w_out, norm_in, m_norm, rel_bias, w_ple, w_ple_gate, norm_final):
    batch, seq, d = x.shape
    assert w_in.shape[0] == 1 and p.shape[0] == 1, "single-layer block"
    assert seq % Q_TILE == 0 and seq % M_CHUNK == 0 and (batch * seq) % ROW_TILE == 0
    rows = batch * seq
    x2 = x.reshape(rows, d)

    sizes = [M_HEADS * M_QK, M_HEADS * M_QK, M_HEADS * M_V, M_HEADS * M_V, M_HEADS * M_V, 2 * M_HEADS,
             A_HEADS * A_DH, A_HEADS * A_DH, A_HEADS * A_DH, A_HEADS * A_DH,
             IDX_HEADS * IDX_DH, IDX_DH, IDX_HEADS]
    offs = np.concatenate([[0], np.cumsum(sizes)])
    wi = w_in[0]
    col = lambda i: wi[:, int(offs[i]):int(offs[i + 1])]
    w_if, w_ixk, w_ixw = col(5), col(11), col(12)

    def small_tile(parts, n_rows):
        out = jnp.zeros((n_rows, LANES), F32)
        for lane0, block in parts.items():
            out = out.at[:, lane0:lane0 + block.shape[1]].set(block)
        return out

    w_sm = small_tile({GATE_I_LANE: w_if[:, :M_HEADS], GATE_F_LANE: w_if[:, M_HEADS:], IDX_W_LANE: w_ixw}, d)
    w_ik2 = jnp.concatenate([w_ixk, w_ixk], axis=1)
    ws = [jnp.concatenate([col(0), col(1)], axis=1), col(2), col(3), col(4), w_sm, w_ik2,
          col(6), col(7), col(8), col(9), col(10)]
    ws = [w.astype(BF16) for w in ws]
    bg = b_gate[0][None, :]
    bpad = small_tile({GATE_I_LANE: bg[:, :M_HEADS], GATE_F_LANE: bg[:, M_HEADS:]}, 1)

    mqk, mv, gate, sm, ik2, aq, ak, avt, az, iq = _inproj(x2, norm_in[0][None, :], ws, batch, seq)

    hm = _mlstm(mqk, mv, gate, sm, conv_w[0], bpad, m_norm[0][None, :], batch, seq)

    ha = _dsa(rel_bias, iq, sm, aq, az, ik2, ak, avt, batch, seq)

    wo = w_out[0].astype(BF16)
    out = _outproj(x2, hm, ha, p[0].reshape(rows, -1), wo[:M_HEADS * M_V], wo[M_HEADS * M_V:],
                   w_ple[0].astype(BF16), w_ple_gate[0].astype(BF16), norm_final[None, :])
    return out.reshape(batch, seq, d)
```

```python
import functools
import math

import numpy as np
import jax
import jax.numpy as jnp
from jax import lax
from jax.experimental import pallas as pl
from jax.experimental.pallas import tpu as pltpu

F32 = jnp.float32
BF16 = jnp.bfloat16
I32 = jnp.int32

M_HEADS = 4
M_QK = 128
M_V = 256
CONV_W = 4
A_HEADS = 8
A_DH = 64
IDX_HEADS = 8
IDX_DH = 64
TOPK_MAX = 256
REL_BUCKETS = 32
REL_MAX_DIST = 128
EPS = 1e-6

LOG2E = 1.4426950408889634
NEG_BIG = -3.0e38
LOGIT_NEG = -1.0e30

ROW_TILE = 512
M_CHUNK = 256
Q_TILE = 256
K_TILE = 256
ONES_ROWS = 16
CONV_HALO = 8
LANES = 128
GATE_I_LANE = 0
GATE_F_LANE = 8
IDX_W_LANE = 16
assert GATE_I_LANE == 0
BISECT_MAX_ITERS = 256
VMEM_LIMIT = 56 * 1024 * 1024


def _silu(v):
    return v * jax.nn.sigmoid(v)


def _dot(a, b):
    return jnp.dot(a, b, preferred_element_type=F32)


def _dot_nt(a, b):
    return lax.dot_general(a, b, (((1,), (1,)), ((), ())), preferred_element_type=F32)


def _dot_tn(a, b):
    return lax.dot_general(a, b, (((0,), (0,)), ((), ())), preferred_element_type=F32)


def _resident(shape):
    nd = len(shape)
    return pl.BlockSpec(shape, lambda *_: (0,) * nd, pipeline_mode=pl.Buffered(1))


def _inproj_body(x_ref, g_ref, w_mqk, w_mv, w_mo, w_mz, w_sm, w_ik, w_aq, w_ak, w_av, w_az, w_iq,
                 o_mqk, o_mv, o_gate, o_sm, o_ik, o_aq, o_ak, o_avt, o_az, o_iq):
    x = x_ref[...]
    ms = jnp.mean(x * x, axis=-1, keepdims=True)
    xn = (x * lax.rsqrt(ms + EPS) * g_ref[...]).astype(BF16)
    o_mqk[...] = _dot(xn, w_mqk[...])
    o_mv[...] = _dot(xn, w_mv[...]).astype(BF16)
    o_gate[...] = jax.nn.sigmoid(_dot(xn, w_mo[...])) * _silu(_dot(xn, w_mz[...]))
    sm = _dot(xn, w_sm[...])
    o_sm[...] = sm
    o_ik[...] = _dot(xn, w_ik[...]).astype(BF16)
    o_aq[...] = (_dot(xn, w_aq[...]) * (A_DH ** -0.5 * LOG2E)).astype(BF16)
    o_ak[...] = _dot(xn, w_ak[...]).astype(BF16)
    o_avt[...] = _dot(xn, w_av[...]).T.astype(BF16)
    o_az[...] = _silu(_dot(xn, w_az[...]))
    iq = _dot(xn, w_iq[...])
    wsc = sm[:, IDX_W_LANE:IDX_W_LANE + IDX_HEADS] * (IDX_HEADS ** -0.5 * IDX_DH ** -0.5)
    lane = lax.broadcasted_iota(I32, (x.shape[0], 2 * IDX_DH), 1)
    for pair in range(IDX_HEADS // 2):
        wpair = jnp.where(lane < IDX_DH, wsc[:, 2 * pair:2 * pair + 1], wsc[:, 2 * pair + 1:2 * pair + 2])
        sl = slice(pair * 2 * IDX_DH, (pair + 1) * 2 * IDX_DH)
        o_iq[:, sl] = (iq[:, sl] * wpair).astype(BF16)


def _inproj(x2, g, ws, batch, seq):
    rows, d = x2.shape
    tm = ROW_TILE
    per_seq = seq // tm
    out_dtypes = [F32, BF16, F32, F32, BF16, BF16, BF16, BF16, F32, BF16]
    widths = [w.shape[1] for w in ws]
    out_widths = [widths[0], widths[1], widths[2]] + widths[4:]
    in_specs = [pl.BlockSpec((tm, d), lambda i: (i, 0)), _resident((1, d))]
    in_specs += [_resident(w.shape) for w in ws]
    out_specs = [pl.BlockSpec((tm, n), lambda i: (i, 0)) for n in out_widths]
    out_shape = [jax.ShapeDtypeStruct((rows, n), dt) for n, dt in zip(out_widths, out_dtypes)]
    av_pos = 7
    out_specs[av_pos] = pl.BlockSpec((None, out_widths[av_pos], tm), lambda i: (i // per_seq, 0, i % per_seq))
    out_shape[av_pos] = jax.ShapeDtypeStruct((batch, out_widths[av_pos], seq), BF16)
    return pl.pallas_call(
        _inproj_body,
        grid=(rows // tm,),
        in_specs=in_specs,
        out_specs=out_specs,
        out_shape=out_shape,
        compiler_params=pltpu.CompilerParams(
            dimension_semantics=("arbitrary",), vmem_limit_bytes=VMEM_LIMIT),
        name="inproj",
    )(x2, g, *ws)


def _row_scan(v, op, fill):
    n = v.shape[0]
    row = lax.broadcasted_iota(I32, v.shape, 0)
    sh = 1
    while sh < n:
        v = op(v, jnp.where(row >= sh, pltpu.roll(v, sh, 0), fill))
        sh *= 2
    return v


def _mlstm_body(mqk_ref, mv_ref, gate_ref, sm_ref, convw_ref, bpad_ref, mnorm_ref, out_ref,
                halo_ref, c_ref, n_ref, m_ref):
    L = M_CHUNK
    H, DK, DV = M_HEADS, M_QK, M_V

    @pl.when(pl.program_id(1) == 0)
    def _():
        halo_ref[...] = jnp.zeros(halo_ref.shape, F32)
        c_ref[...] = jnp.zeros(c_ref.shape, F32)
        n_ref[...] = jnp.zeros(n_ref.shape, F32)
        m_ref[...] = jnp.zeros(m_ref.shape, F32)

    raw = mqk_ref[...]
    halo = halo_ref[...]
    row8 = lax.broadcasted_iota(I32, halo.shape, 0)
    y = raw * convw_ref[CONV_W - 1:CONV_W, :]
    for back in range(1, CONV_W):
        rot = pltpu.roll(raw, back, 0)
        head = jnp.where(row8 < back, pltpu.roll(halo, back, 0), rot[0:CONV_HALO, :])
        tap = jnp.concatenate([head, rot[CONV_HALO:, :]], axis=0)
        y = y + tap * convw_ref[CONV_W - 1 - back:CONV_W - back, :]
    halo_ref[...] = raw[L - CONV_HALO:L, :]
    qk = _silu(y)

    g = sm_ref[...] + bpad_ref[...]
    lf = jax.nn.log_sigmoid(pltpu.roll(g, LANES - (GATE_F_LANE - GATE_I_LANE), 1))
    b = _row_scan(lf, jnp.add, 0.0)
    a = g - b
    m_prev = m_ref[0:1, :]
    mcol = jnp.maximum(m_prev, _row_scan(a, jnp.maximum, -jnp.inf))
    w_inter = jnp.exp(m_prev - mcol)
    e_neg_mt = jnp.exp(-(b + mcol))
    m_last = mcol[L - 1:L, :]
    wk = jnp.exp(a - m_last)
    decay = jnp.exp(m_prev - m_last)
    m_new = b[L - 1:L, :] + m_last
    a_rows = a.T

    ri = lax.broadcasted_iota(I32, (L, L), 0)
    ci = lax.broadcasted_iota(I32, (L, L), 1)
    causal = ci <= ri

    heads = range(H)
    q = [qk[:, h * DK:(h + 1) * DK] for h in heads]
    k = [qk[:, (H + h) * DK:(H + h + 1) * DK] * (DK ** -0.5) for h in heads]
    qb = [q[h].astype(BF16) for h in heads]
    v = [mv_ref[:, h * DV:(h + 1) * DV] for h in heads]
    c_old = [c_ref[h] for h in heads]
    qkt = [_dot_nt(qb[h], k[h].astype(BF16)) for h in heads]
    q_c = [_dot(qb[h], c_old[h].astype(BF16)) for h in heads]
    s = [qkt[h] * jnp.where(causal, jnp.exp(a_rows[h:h + 1, :] - mcol[:, h:h + 1]), 0.0) for h in heads]
    num = [_dot(s[h].astype(BF16), v[h]) + w_inter[:, h:h + 1] * q_c[h] for h in heads]
    kw = [k[h] * wk[:, h:h + 1] for h in heads]
    upd = [_dot_tn(kw[h].astype(BF16), v[h]) for h in heads]
    for h in heads:
        den = (jnp.sum(s[h], axis=1, keepdims=True)
               + w_inter[:, h:h + 1] * jnp.sum(q[h] * n_ref[h:h + 1, :], axis=1, keepdims=True))
        den = jnp.maximum(jnp.abs(den), e_neg_mt[:, h:h + 1])
        hh = num[h] / den
        hn = hh * lax.rsqrt(jnp.mean(hh * hh, axis=-1, keepdims=True) + EPS)
        hn = hn * mnorm_ref[:, h * DV:(h + 1) * DV]
        out_ref[:, h * DV:(h + 1) * DV] = (gate_ref[:, h * DV:(h + 1) * DV] * hn).astype(BF16)
    for h in heads:
        dec = decay[:, h:h + 1]
        c_ref[h] = dec * c_old[h] + upd[h]
        n_ref[h:h + 1, :] = dec * n_ref[h:h + 1, :] + jnp.sum(kw[h], axis=0, keepdims=True)
    m_ref[...] = jnp.broadcast_to(m_new, m_ref.shape)


def _mlstm(mqk, mv, gate, sm, conv_w, bpad, mnorm, batch, seq):
    L = M_CHUNK
    nc = seq // L
    wqk = mqk.shape[1]
    wv = mv.shape[1]
    row = lambda b, c: (b * nc + c, 0)
    return pl.pallas_call(
        _mlstm_body,
        grid=(batch, nc),
        in_specs=[
            pl.BlockSpec((L, wqk), row),
            pl.BlockSpec((L, wv), row),
            pl.BlockSpec((L, wv), row),
            pl.BlockSpec((L, 128), row),
            _resident(conv_w.shape),
            _resident(bpad.shape),
            _resident(mnorm.shape),
        ],
        out_specs=pl.BlockSpec((L, wv), row),
        out_shape=jax.ShapeDtypeStruct((batch * seq, wv), BF16),
        scratch_shapes=[
            pltpu.VMEM((CONV_HALO, wqk), F32),
            pltpu.VMEM((M_HEADS, M_QK, M_V), F32),
            pltpu.VMEM((8, M_QK), F32),
            pltpu.VMEM((8, 128), F32),
        ],
        compiler_params=pltpu.CompilerParams(
            dimension_semantics=("arbitrary", "arbitrary"), vmem_limit_bytes=VMEM_LIMIT),
        name="mlstm",
    )(mqk, mv, gate, sm, conv_w, bpad, mnorm)


def _t5_bucket_table(max_dist):
    d = np.arange(max_dist)
    max_exact = REL_BUCKETS // 2
    tabs = []
    for dt in (np.float32, np.float64):
        ratio = np.maximum(d, 1).astype(dt) / dt(max_exact)
        large = max_exact + (np.log(ratio) / dt(math.log(REL_MAX_DIST / max_exact))
                             * dt(REL_BUCKETS - max_exact)).astype(np.int32)
        large = np.minimum(large, REL_BUCKETS - 1)
        tabs.append(np.where(d < max_exact, d, large).astype(np.int32))
    assert np.array_equal(tabs[0], tabs[1])
    return tabs[0]


def _bucket_maps():
    tab = _t5_bucket_table(Q_TILE + K_TILE)
    s = np.arange(K_TILE)[:, None]
    t = np.arange(Q_TILE)[None, :]
    diag = tab[np.maximum(t - s, 0)]
    prev = tab[K_TILE + t - s]
    assert np.all(tab[K_TILE:] == REL_BUCKETS - 1)
    return np.stack([prev, diag]).astype(np.int32)


def _dsa_body(rb_ref, iq_ref, sm_ref, aq_ref, az_ref, ik_ref, k_ref, vt_ref, map_ref, out_ref,
              sc_ref, stat_ref, iqm_ref, qm_ref, bias_ref, acc_ref, m_ref, l_ref, alpha_ref, mtile_ref, lg_ref,
              p_ref):
    TQ, TK, H, DH = Q_TILE, K_TILE, A_HEADS, A_DH
    j = pl.program_id(1)
    n_tiles = j + 1

    @pl.when((pl.program_id(0) == 0) & (j == 0))
    def _():
        bias_ref[...] = jnp.zeros(bias_ref.shape, F32)

        def fill(bkt, carry):
            for which in range(2):
                hit = map_ref[which] == bkt
                for h in range(H):
                    val = (rb_ref[bkt, h] - rb_ref[REL_BUCKETS - 1, h]) * LOG2E
                    bias_ref[which, h] = jnp.where(hit, val, bias_ref[which, h])
            return carry

        lax.fori_loop(0, REL_BUCKETS - 1, fill, 0)

    lane = lax.broadcasted_iota(I32, (TQ, 2 * DH), 1)
    for h in range(H):
        p0 = (h // 2) * 2 * DH
        keep = (lane < DH) if h % 2 == 0 else (lane >= DH)
        iqm_ref[h] = jnp.where(keep, iq_ref[:, p0:p0 + 2 * DH], jnp.zeros((), BF16))
        qm_ref[h] = jnp.where(keep, aq_ref[:, p0:p0 + 2 * DH], jnp.zeros((), BF16))
    w_rows = sm_ref[...].T[IDX_W_LANE:IDX_W_LANE + IDX_HEADS, :]
    clamp_lo = jnp.where(w_rows >= 0.0, 0.0, -jnp.inf)
    clamp_hi = jnp.where(w_rows >= 0.0, jnp.inf, 0.0)

    s_loc = lax.broadcasted_iota(I32, (TK, TQ), 0)
    t_loc = lax.broadcasted_iota(I32, (TK, TQ), 1)

    fold = lambda v: v.reshape(v.shape[0] // 8, 8, TQ)

    def index_tile(kb, diag, stats):
        r0 = pl.multiple_of(kb * TK, TK)
        lhs = ik_ref[pl.ds(r0, TK), :]
        sc = None
        for h in range(H):
            y = _dot_nt(lhs, iqm_ref[h])
            term = jnp.minimum(jnp.maximum(y, clamp_lo[h:h + 1, :]), clamp_hi[h:h + 1, :])
            sc = term if sc is None else sc + term
        sc_min = sc
        if diag:
            admissible = s_loc <= t_loc
            sc_min = jnp.where(admissible, sc, -NEG_BIG)
            sc = jnp.where(admissible, sc, NEG_BIG)
        sc_ref[pl.ds(r0, TK), :] = sc
        c_pos_, c_nn_, mx_, mn_ = stats
        c_pos_ = c_pos_ + jnp.sum(fold(jnp.where(sc > 0.0, 1.0, 0.0)), axis=0)
        c_nn_ = c_nn_ + jnp.sum(fold(jnp.where(sc >= 0.0, 1.0, 0.0)), axis=0)
        mx_ = jnp.maximum(mx_, jnp.max(fold(sc), axis=0))
        mn_ = jnp.minimum(mn_, jnp.min(fold(sc_min), axis=0))
        return c_pos_, c_nn_, mx_, mn_

    stat_ref[0] = jnp.zeros((8, TQ), F32)
    stat_ref[1] = jnp.zeros((8, TQ), F32)
    stat_ref[2] = jnp.full((8, TQ), NEG_BIG, F32)
    stat_ref[3] = jnp.full((8, TQ), -NEG_BIG, F32)

    def index_tiles(tiles):
        st = tuple(stat_ref[i] for i in range(4))
        for kb, diag in tiles:
            st = index_tile(kb, diag, st)
        for i in range(4):
            stat_ref[i] = st[i]

    def index_pairs(i, carry):
        index_tiles([(2 * i, False), (2 * i + 1, False)])
        return carry

    lax.fori_loop(0, j >> 1, index_pairs, 0)

    @pl.when((j & 1) == 1)
    def _():
        index_tiles([(j - 1, False), (j, True)])

    @pl.when((j & 1) == 0)
    def _():
        index_tiles([(j, True)])

    c_pos = jnp.sum(stat_ref[0], axis=0, keepdims=True).astype(I32)
    c_nn = jnp.sum(stat_ref[1], axis=0, keepdims=True).astype(I32)
    hi0 = jnp.max(stat_ref[2], axis=0, keepdims=True)
    mn = jnp.min(stat_ref[3], axis=0, keepdims=True)

    @pl.when((n_tiles & 1) == 1)
    def _():
        sc_ref[pl.ds(pl.multiple_of(n_tiles * TK, TK), TK), :] = jnp.full((TK, TQ), NEG_BIG, F32)

    n_pairs = (n_tiles + 1) >> 1

    def count_gt(thr):
        def body(kp, cnt):
            r0 = pl.multiple_of(kp * (2 * TK), 2 * TK)
            hit = jnp.where(sc_ref[pl.ds(r0, 2 * TK), :] > thr, 1, 0).astype(I32)
            return cnt + jnp.sum(hit.reshape(2 * TK // 32, 32, TQ), axis=0)
        cnt32 = lax.fori_loop(1, n_pairs, body, body(0, jnp.zeros((32, TQ), I32)))
        return jnp.sum(cnt32, axis=0, keepdims=True)

    t_glob = j * TQ + lax.broadcasted_iota(I32, (1, TQ), 1)
    k_eff = jnp.minimum(TOPK_MAX, t_glob + 1)

    zero = jnp.zeros((1, TQ), F32)
    all_in = (t_glob + 1) <= TOPK_MAX
    pos_ok = c_pos >= k_eff
    tie0 = (~all_in) & (~pos_ok) & (c_nn >= k_eff)
    lo_neg = jnp.maximum(mn + mn - 1.0, NEG_BIG)
    lo = jnp.where(all_in, NEG_BIG, jnp.where(pos_ok, zero, lo_neg))
    hi = jnp.where((~all_in) & (~pos_ok), zero, hi0)
    c_lo = jnp.where(pos_ok, c_pos, t_glob + 1)
    c_hi = jnp.where(pos_ok, 0, c_pos)
    fin0 = all_in | (pos_ok & (c_pos == k_eff)) | tie0

    def open_rows(fin_, c_lo_, c_hi_):
        return (fin_ == 0.0) & ((c_lo_ - c_hi_) > 2)

    def bis_cond(st):
        return (st[0] < BISECT_MAX_ITERS) & (st[1] > 0)

    def bis_body(st):
        it, _, lo_, hi_, c_lo_, c_hi_, fin_, tie_ = st
        go_n = (jnp.max(jnp.where(open_rows(fin_, c_lo_, c_hi_), 1.0, 0.0)) > 0.0).astype(I32)
        mid = 0.5 * lo_ + 0.5 * hi_
        collapsed = (mid <= lo_) | (mid >= hi_)
        cm = count_gt(mid)
        act = fin_ == 0.0
        up = act & (~collapsed) & (cm >= k_eff)
        dn = act & (~collapsed) & (cm < k_eff)
        tie_n = jnp.where(act & collapsed, 1.0, tie_)
        fin_n = jnp.where((act & collapsed) | (up & (cm == k_eff)), 1.0, fin_)
        return (it + 1, go_n, jnp.where(up, mid, lo_), jnp.where(dn, mid, hi_),
                jnp.where(up, cm, c_lo_), jnp.where(dn, cm, c_hi_), fin_n, tie_n)

    _, _, lo, hi, c_lo, c_hi, fin_f, tie_f = lax.while_loop(
        bis_cond, bis_body,
        (jnp.int32(0), jnp.int32(1), lo, hi, c_lo, c_hi, jnp.where(fin0, 1.0, 0.0), jnp.where(tie0, 1.0, 0.0)))

    def pick_two():
        def body(kp, carry):
            e_min, e_max = carry
            r0 = pl.multiple_of(kp * (2 * TK), 2 * TK)
            sc = sc_ref[pl.ds(r0, 2 * TK), :]
            above = jnp.where(sc > lo, sc, -NEG_BIG).reshape(2 * TK // 32, 32, TQ)
            below_hi = jnp.where(sc <= hi, sc, NEG_BIG).reshape(2 * TK // 32, 32, TQ)
            return jnp.minimum(e_min, jnp.min(above, axis=0)), jnp.maximum(e_max, jnp.max(below_hi, axis=0))
        init = (jnp.full((32, TQ), -NEG_BIG, F32), jnp.full((32, TQ), NEG_BIG, F32))
        e_min, e_max = lax.fori_loop(0, n_pairs, body, init)
        return jnp.min(e_min, axis=0, keepdims=True), jnp.max(e_max, axis=0, keepdims=True)

    e1, e2 = pick_two()
    left = fin_f == 0.0
    lo = jnp.where(left & (e1 < e2), e1, lo)
    hi = jnp.where(left & (e1 >= e2), e1, hi)
    tie_f = jnp.where(left & (e1 >= e2), 1.0, tie_f)
    tie = tie_f > 0.0
    theta_ref = m_ref
    theta_ref[0:1, :] = lo

    @pl.when(jnp.max(tie_f) > 0.0)
    def _():
        v = jnp.where(tie, hi, jnp.inf)
        need = (k_eff - c_hi).astype(F32)
        tri = jnp.where(lax.broadcasted_iota(I32, (TK, TK), 1) < lax.broadcasted_iota(I32, (TK, TK), 0),
                        1.0, 0.0).astype(BF16)

        def tie_pair(kp, seen):
            r0 = pl.multiple_of(kp * (2 * TK), 2 * TK)
            sc2 = sc_ref[pl.ds(r0, 2 * TK), :]
            out = []
            for half in range(2):
                sc = sc2[half * TK:(half + 1) * TK, :]
                eqf = jnp.where(sc == v, 1.0, 0.0)
                rank = seen + _dot(tri, eqf.astype(BF16))
                out.append(jnp.where(eqf * (need - rank) > 0.0, -NEG_BIG, sc))
                seen = seen + jnp.sum(eqf, axis=0, keepdims=True)
            sc_ref[pl.ds(r0, 2 * TK), :] = jnp.concatenate(out, axis=0)
            return seen

        lax.fori_loop(0, n_pairs, tie_pair, jnp.zeros((1, TQ), F32))
        theta_ref[0:1, :] = jnp.where(tie, hi, lo)

    theta = theta_ref[0:1, :]

    m_ref[...] = jnp.full(m_ref.shape, LOGIT_NEG, F32)
    l_ref[...] = jnp.zeros(l_ref.shape, F32)
    acc_ref[...] = jnp.zeros(acc_ref.shape, F32)

    def tile_mask(kb):
        r0 = pl.multiple_of(kb * TK, TK)
        return jnp.where(sc_ref[pl.ds(r0, TK), :] > theta, 0.0, LOGIT_NEG)

    def logits_head(kb, slot, near, mb, h):
        r0 = pl.multiple_of(kb * TK, TK)
        p0 = (h // 2) * 2 * DH
        lg = _dot_nt(k_ref[pl.ds(r0, TK), p0:p0 + 2 * DH], qm_ref[h]) + mb
        if near is not None:
            lg = lg + bias_ref[near, h]
        lg_ref[slot, h] = lg
        m_old = m_ref[h:h + 1, :]
        m_new = jnp.maximum(m_old, jnp.max(lg, axis=0, keepdims=True))
        alpha_ref[slot, h:h + 1, :] = jnp.exp2(m_old - m_new)
        mtile_ref[slot, h:h + 1, :] = m_new
        m_ref[h:h + 1, :] = m_new

    def exp_head(slot, h):
        p_ref[slot, h] = jnp.exp2(lg_ref[slot, h] - mtile_ref[slot, h:h + 1, :]).astype(BF16)

    def pv_head(kb, slot, h):
        r0 = pl.multiple_of(kb * TK, TK)
        ones = jnp.ones((ONES_ROWS, TK), BF16)
        lhs = jnp.concatenate([vt_ref[h * DH:(h + 1) * DH, pl.ds(r0, TK)], ones], axis=0)
        pv = _dot(lhs, p_ref[slot, h])
        alpha = alpha_ref[slot, h:h + 1, :]
        acc_ref[h * DH:(h + 1) * DH, :] = alpha * acc_ref[h * DH:(h + 1) * DH, :] + pv[0:DH, :]
        l_ref[h:h + 1, :] = alpha * l_ref[h:h + 1, :] + pv[DH:DH + 1, :]

    def fill(kb, slot, near):
        mb = tile_mask(kb)
        for h in range(H):
            logits_head(kb, slot, near, mb, h)

    def drain(kb, slot):
        for h in range(H):
            exp_head(slot, h)
        for h in range(H):
            pv_head(kb, slot, h)

    def step(kb_prev, slot_prev, kb_next, slot_next, near_next):
        mb = tile_mask(kb_next)
        for h in range(H):
            exp_head(slot_prev, h)
            logits_head(kb_next, slot_next, near_next, mb, h)
            pv_head(kb_prev, slot_prev, h)

    n_far = j - 1

    @pl.when(n_far >= 1)
    def _():
        fill(0, 0, None)

        def far_loop(t, carry):
            step(2 * t, 0, 2 * t + 1, 1, None)
            step(2 * t + 1, 1, 2 * t + 2, 0, None)
            return carry

        lax.fori_loop(0, (n_far - 1) >> 1, far_loop, 0)

        @pl.when(((n_far - 1) & 1) == 1)
        def _():
            step(j - 3, 0, j - 2, 1, None)
            step(j - 2, 1, j - 1, 0, 0)
            step(j - 1, 0, j, 1, 1)
            drain(j, 1)

        @pl.when(((n_far - 1) & 1) == 0)
        def _():
            step(j - 2, 0, j - 1, 1, 0)
            step(j - 1, 1, j, 0, 1)
            drain(j, 0)

    @pl.when(j == 1)
    def _():
        fill(0, 0, 0)
        step(0, 0, 1, 1, 1)
        drain(1, 1)

    @pl.when(j == 0)
    def _():
        fill(0, 0, 1)
        drain(0, 0)

    for h in range(H):
        acc_ref[h * DH:(h + 1) * DH, :] = acc_ref[h * DH:(h + 1) * DH, :] / l_ref[h:h + 1, :]
    out_ref[...] = (acc_ref[...].T * az_ref[...]).astype(BF16)


def _dsa(rel_bias, iq, sm, aq, az, ik2, ak, avt, batch, seq):
    TQ = Q_TILE
    nq = seq // TQ
    width = A_HEADS * A_DH
    maps = jnp.asarray(_bucket_maps())
    qrow = lambda b, j: (b * nq + j, 0)
    return pl.pallas_call(
        _dsa_body,
        grid=(batch, nq),
        in_specs=[
            pl.BlockSpec(memory_space=pltpu.SMEM),
            pl.BlockSpec((TQ, width), qrow),
            pl.BlockSpec((TQ, 128), qrow),
            pl.BlockSpec((TQ, width), qrow),
            pl.BlockSpec((TQ, width), qrow),
            pl.BlockSpec((seq, 128), lambda b, j: (b, 0), pipeline_mode=pl.Buffered(1)),
            pl.BlockSpec((seq, width), lambda b, j: (b, 0), pipeline_mode=pl.Buffered(1)),
            pl.BlockSpec((None, width, seq), lambda b, j: (b, 0, 0), pipeline_mode=pl.Buffered(1)),
            _resident(maps.shape),
        ],
        out_specs=pl.BlockSpec((TQ, width), qrow),
        out_shape=jax.ShapeDtypeStruct((batch * seq, width), BF16),
        scratch_shapes=[
            pltpu.VMEM((seq, TQ), F32),
            pltpu.VMEM((4, 8, TQ), F32),
            pltpu.VMEM((A_HEADS, TQ, 2 * A_DH), BF16),
            pltpu.VMEM((A_HEADS, TQ, 2 * A_DH), BF16),
            pltpu.VMEM((2, A_HEADS, K_TILE, TQ), F32),
            pltpu.VMEM((A_HEADS * A_DH, TQ), F32),
            pltpu.VMEM((8, TQ), F32),
            pltpu.VMEM((8, TQ), F32),
            pltpu.VMEM((2, 8, TQ), F32),
            pltpu.VMEM((2, 8, TQ), F32),
            pltpu.VMEM((2, A_HEADS, K_TILE, TQ), F32),
            pltpu.VMEM((2, A_HEADS, K_TILE, TQ), BF16),
        ],
        compiler_params=pltpu.CompilerParams(
            dimension_semantics=("arbitrary", "arbitrary"), vmem_limit_bytes=VMEM_LIMIT),
        name="dsa",
    )(rel_bias, iq, sm, aq, az, ik2, ak, avt, maps)


def _outproj_body(x_ref, hm_ref, ha_ref, p_ref, wom_ref, woa_ref, wple_ref, wg_ref, g_ref, o_ref):
    h1 = x_ref[...] + _dot(hm_ref[...], wom_ref[...]) + _dot(ha_ref[...], woa_ref[...])
    gate = jax.nn.sigmoid(_dot(h1.astype(BF16), wg_ref[...]))
    h2 = h1 + _dot(p_ref[...].astype(BF16), wple_ref[...]) * gate
    ms = jnp.mean(h2 * h2, axis=-1, keepdims=True)
    o_ref[...] = h2 * lax.rsqrt(ms + EPS) * g_ref[...]


def _outproj(x2, hm, ha, p2, wom, woa, wple, wg, g):
    rows, d = x2.shape
    tm = ROW_TILE
    tile = lambda n: pl.BlockSpec((tm, n), lambda i: (i, 0))
    return pl.pallas_call(
        _outproj_body,
        grid=(rows // tm,),
        in_specs=[tile(d), tile(hm.shape[1]), tile(ha.shape[1]), tile(p2.shape[1]),
                  _resident(wom.shape), _resident(woa.shape), _resident(wple.shape),
                  _resident(wg.shape), _resident(g.shape)],
        out_specs=tile(d),
        out_shape=jax.ShapeDtypeStruct((rows, d), F32),
        compiler_params=pltpu.CompilerParams(
            dimension_semantics=("arbitrary",), vmem_limit_bytes=VMEM_LIMIT),
        name="outproj",
    )(x2, hm, ha, p2, wom, woa, wple, wg, g)


def kernel(x, p, w_in, b_gate, conv_w, w_out, norm_in, m_norm, rel_bias, w_ple, w_ple_gate, norm_final):
    batch, seq, d = x.shape
    assert w_in.shape[0] == 1 and p.shape[0] == 1, "single-layer block"
    assert seq % Q_TILE == 0 and seq % M_CHUNK == 0 and (batch * seq) % ROW_TILE == 0
    rows = batch * seq
    x2 = x.reshape(rows, d)

    sizes = [M_HEADS * M_QK, M_HEADS * M_QK, M_HEADS * M_V, M_HEADS * M_V, M_HEADS * M_V, 2 * M_HEADS,
             A_HEADS * A_DH, A_HEADS * A_DH, A_HEADS * A_DH, A_HEADS * A_DH,
             IDX_HEADS * IDX_DH, IDX_DH, IDX_HEADS]
    offs = np.concatenate([[0], np.cumsum(sizes)])
    wi = w_in[0]
    col = lambda i: wi[:, int(offs[i]):int(offs[i + 1])]
    w_if, w_ixk, w_ixw = col(5), col(11), col(12)

    def small_tile(parts, n_rows):
        out = jnp.zeros((n_rows, LANES), F32)
        for lane0, block in parts.items():
            out = out.at[:, lane0:lane0 + block.shape[1]].set(block)
        return out

    w_sm = small_tile({GATE_I_LANE: w_if[:, :M_HEADS], GATE_F_LANE: w_if[:, M_HEADS:], IDX_W_LANE: w_ixw}, d)
    w_ik2 = jnp.concatenate([w_ixk, w_ixk], axis=1)
    ws = [jnp.concatenate([col(0), col(1)], axis=1), col(2), col(3), col(4), w_sm, w_ik2,
          col(6), col(7), col(8), col(9), col(10)]
    ws = [w.astype(BF16) for w in ws]
    bg = b_gate[0][None, :]
    bpad = small_tile({GATE_I_LANE: bg[:, :M_HEADS], GATE_F_LANE: bg[:, M_HEADS:]}, 1)

    mqk, mv, gate, sm, ik2, aq, ak, avt, az, iq = _inproj(x2, norm_in[0][None, :], ws, batch, seq)

    hm = _mlstm(mqk, mv, gate, sm, conv_w[0], bpad, m_norm[0][None, :], batch, seq)

    ha = _dsa(rel_bias, iq, sm, aq, az, ik2, ak, avt, batch, seq)

    wo = w_out[0].astype(BF16)
    out = _outproj(x2, hm, ha, p[0].reshape(rows, -1), wo[:M_HEADS * M_V], wo[M_HEADS * M_V:],
                   w_ple[0].astype(BF16), w_ple_gate[0].astype(BF16), norm_final[None, :])
    return out.reshape(batch, seq, d)
```

```python
import math

import numpy as np
import jax
import jax.numpy as jnp
from jax import lax
from jax.experimental import pallas as pl
from jax.experimental.pallas import tpu as pltpu

F32 = jnp.float32
BF16 = jnp.bfloat16
I32 = jnp.int32

M_HEADS = 4
M_QK = 128
M_V = 256
CONV_W = 4
A_HEADS = 8
A_DH = 64
IDX_HEADS = 8
IDX_DH = 64
TOPK_MAX = 256
REL_BUCKETS = 32
REL_MAX_DIST = 128
EPS = 1e-6

LOG2E = 1.4426950408889634
NEG_BIG = -3.0e38
LOGIT_NEG = -1.0e30

ROW_TILE = 512
M_CHUNK = 256
Q_TILE = 256
K_TILE = 256
ONES_ROWS = 16
CONV_HALO = 8
LANES = 128
GATE_I_LANE = 0
GATE_F_LANE = 8
IDX_W_LANE = 16
assert GATE_I_LANE == 0
BISECT_MAX_ITERS = 256
VMEM_LIMIT = 56 * 1024 * 1024


def _silu(v):
    return v * jax.nn.sigmoid(v)


def _dot(a, b):
    return jnp.dot(a, b, preferred_element_type=F32)


def _dot_nt(a, b):
    return lax.dot_general(a, b, (((1,), (1,)), ((), ())), preferred_element_type=F32)


def _dot_tn(a, b):
    return lax.dot_general(a, b, (((0,), (0,)), ((), ())), preferred_element_type=F32)


def _resident(shape):
    nd = len(shape)
    return pl.BlockSpec(shape, lambda *_: (0,) * nd, pipeline_mode=pl.Buffered(1))


def _inproj_body(x_ref, g_ref, w_mqk, w_mv, w_mo, w_mz, w_sm, w_ik, w_aq, w_ak, w_av, w_az, w_iq,
                 o_mqk, o_mv, o_gate, o_sm, o_ik, o_aq, o_ak, o_avt, o_az, o_iq):
    x = x_ref[...]
    ms = jnp.mean(x * x, axis=-1, keepdims=True)
    xn = (x * lax.rsqrt(ms + EPS) * g_ref[...]).astype(BF16)
    o_mqk[...] = _dot(xn, w_mqk[...])
    o_mv[...] = _dot(xn, w_mv[...]).astype(BF16)
    o_gate[...] = jax.nn.sigmoid(_dot(xn, w_mo[...])) * _silu(_dot(xn, w_mz[...]))
    sm = _dot(xn, w_sm[...])
    o_sm[...] = sm
    o_ik[...] = _dot(xn, w_ik[...]).astype(BF16)
    o_aq[...] = (_dot(xn, w_aq[...]) * (A_DH ** -0.5 * LOG2E)).astype(BF16)
    o_ak[...] = _dot(xn, w_ak[...]).astype(BF16)
    o_avt[...] = _dot(xn, w_av[...]).T.astype(BF16)
    o_az[...] = _silu(_dot(xn, w_az[...]))
    iq = _dot(xn, w_iq[...])
    wsc = sm[:, IDX_W_LANE:IDX_W_LANE + IDX_HEADS] * (IDX_HEADS ** -0.5 * IDX_DH ** -0.5)
    lane = lax.broadcasted_iota(I32, (x.shape[0], 2 * IDX_DH), 1)
    for pair in range(IDX_HEADS // 2):
        wpair = jnp.where(lane < IDX_DH, wsc[:, 2 * pair:2 * pair + 1], wsc[:, 2 * pair + 1:2 * pair + 2])
        sl = slice(pair * 2 * IDX_DH, (pair + 1) * 2 * IDX_DH)
        o_iq[:, sl] = (iq[:, sl] * wpair).astype(BF16)


def _inproj(x2, g, ws, batch, seq):
    rows, d = x2.shape
    tm = ROW_TILE
    per_seq = seq // tm
    out_dtypes = [F32, BF16, F32, F32, BF16, BF16, BF16, BF16, F32, BF16]
    widths = [w.shape[1] for w in ws]
    out_widths = [widths[0], widths[1], widths[2]] + widths[4:]
    in_specs = [pl.BlockSpec((tm, d), lambda i: (i, 0)), _resident((1, d))]
    in_specs += [_resident(w.shape) for w in ws]
    out_specs = [pl.BlockSpec((tm, n), lambda i: (i, 0)) for n in out_widths]
    out_shape = [jax.ShapeDtypeStruct((rows, n), dt) for n, dt in zip(out_widths, out_dtypes)]
    av_pos = 7
    out_specs[av_pos] = pl.BlockSpec((None, out_widths[av_pos], tm), lambda i: (i // per_seq, 0, i % per_seq))
    out_shape[av_pos] = jax.ShapeDtypeStruct((batch, out_widths[av_pos], seq), BF16)
    return pl.pallas_call(
        _inproj_body,
        grid=(rows // tm,),
        in_specs=in_specs,
        out_specs=out_specs,
        out_shape=out_shape,
        compiler_params=pltpu.CompilerParams(
            dimension_semantics=("arbitrary",), vmem_limit_bytes=VMEM_LIMIT),
        name="inproj",
    )(x2, g, *ws)


def _row_scan(v, op, fill):
    n = v.shape[0]
    row = lax.broadcasted_iota(I32, v.shape, 0)
    sh = 1
    while sh < n:
        v = op(v, jnp.where(row >= sh, pltpu.roll(v, sh, 0), fill))
        sh *= 2
    return v


def _mlstm_body(mqk_ref, mv_ref, gate_ref, sm_ref, convw_ref, bpad_ref, mnorm_ref, out_ref,
                halo_ref, c_ref, n_ref, m_ref):
    L = M_CHUNK
    H, DK, DV = M_HEADS, M_QK, M_V

    @pl.when(pl.program_id(1) == 0)
    def _():
        halo_ref[...] = jnp.zeros(halo_ref.shape, F32)
        c_ref[...] = jnp.zeros(c_ref.shape, F32)
        n_ref[...] = jnp.zeros(n_ref.shape, F32)
        m_ref[...] = jnp.zeros(m_ref.shape, F32)

    raw = mqk_ref[...]
    halo = halo_ref[...]
    row8 = lax.broadcasted_iota(I32, halo.shape, 0)
    y = raw * convw_ref[CONV_W - 1:CONV_W, :]
    for back in range(1, CONV_W):
        rot = pltpu.roll(raw, back, 0)
        head = jnp.where(row8 < back, pltpu.roll(halo, back, 0), rot[0:CONV_HALO, :])
        tap = jnp.concatenate([head, rot[CONV_HALO:, :]], axis=0)
        y = y + tap * convw_ref[CONV_W - 1 - back:CONV_W - back, :]
    halo_ref[...] = raw[L - CONV_HALO:L, :]
    qk = _silu(y)

    g = sm_ref[...] + bpad_ref[...]
    lf = jax.nn.log_sigmoid(pltpu.roll(g, LANES - (GATE_F_LANE - GATE_I_LANE), 1))
    b = _row_scan(lf, jnp.add, 0.0)
    a = g - b
    m_prev = m_ref[0:1, :]
    mcol = jnp.maximum(m_prev, _row_scan(a, jnp.maximum, -jnp.inf))
    w_inter = jnp.exp(m_prev - mcol)
    e_neg_mt = jnp.exp(-(b + mcol))
    m_last = mcol[L - 1:L, :]
    wk = jnp.exp(a - m_last)
    decay = jnp.exp(m_prev - m_last)
    m_new = b[L - 1:L, :] + m_last
    a_rows = a.T

    ri = lax.broadcasted_iota(I32, (L, L), 0)
    ci = lax.broadcasted_iota(I32, (L, L), 1)
    causal = ci <= ri

    heads = range(H)
    q = [qk[:, h * DK:(h + 1) * DK] for h in heads]
    k = [qk[:, (H + h) * DK:(H + h + 1) * DK] * (DK ** -0.5) for h in heads]
    qb = [q[h].astype(BF16) for h in heads]
    v = [mv_ref[:, h * DV:(h + 1) * DV] for h in heads]
    c_old = [c_ref[h] for h in heads]
    qkt = [_dot_nt(qb[h], k[h].astype(BF16)) for h in heads]
    q_c = [_dot(qb[h], c_old[h].astype(BF16)) for h in heads]
    s = [qkt[h] * jnp.where(causal, jnp.exp(a_rows[h:h + 1, :] - mcol[:, h:h + 1]), 0.0) for h in heads]
    num = [_dot(s[h].astype(BF16), v[h]) + w_inter[:, h:h + 1] * q_c[h] for h in heads]
    kw = [k[h] * wk[:, h:h + 1] for h in heads]
    upd = [_dot_tn(kw[h].astype(BF16), v[h]) for h in heads]
    for h in heads:
        den = (jnp.sum(s[h], axis=1, keepdims=True)
               + w_inter[:, h:h + 1] * jnp.sum(q[h] * n_ref[h:h + 1, :], axis=1, keepdims=True))
        den = jnp.maximum(jnp.abs(den), e_neg_mt[:, h:h + 1])
        hh = num[h] / den
        hn = hh * lax.rsqrt(jnp.mean(hh * hh, axis=-1, keepdims=True) + EPS)
        hn = hn * mnorm_ref[:, h * DV:(h + 1) * DV]
        out_ref[:, h * DV:(h + 1) * DV] = (gate_ref[:, h * DV:(h + 1) * DV] * hn).astype(BF16)
    for h in heads:
        dec = decay[:, h:h + 1]
        c_ref[h] = dec * c_old[h] + upd[h]
        n_ref[h:h + 1, :] = dec * n_ref[h:h + 1, :] + jnp.sum(kw[h], axis=0, keepdims=True)
    m_ref[...] = jnp.broadcast_to(m_new, m_ref.shape)


def _mlstm(mqk, mv, gate, sm, conv_w, bpad, mnorm, batch, seq):
    L = M_CHUNK
    nc = seq // L
    wqk = mqk.shape[1]
    wv = mv.shape[1]
    row = lambda b, c: (b * nc + c, 0)
    return pl.pallas_call(
        _mlstm_body,
        grid=(batch, nc),
        in_specs=[
            pl.BlockSpec((L, wqk), row),
            pl.BlockSpec((L, wv), row),
            pl.BlockSpec((L, wv), row),
            pl.BlockSpec((L, 128), row),
            _resident(conv_w.shape),
            _resident(bpad.shape),
            _resident(mnorm.shape),
        ],
        out_specs=pl.BlockSpec((L, wv), row),
        out_shape=jax.ShapeDtypeStruct((batch * seq, wv), BF16),
        scratch_shapes=[
            pltpu.VMEM((CONV_HALO, wqk), F32),
            pltpu.VMEM((M_HEADS, M_QK, M_V), F32),
            pltpu.VMEM((8, M_QK), F32),
            pltpu.VMEM((8, 128), F32),
        ],
        compiler_params=pltpu.CompilerParams(
            dimension_semantics=("arbitrary", "arbitrary"), vmem_limit_bytes=VMEM_LIMIT),
        name="mlstm",
    )(mqk, mv, gate, sm, conv_w, bpad, mnorm)


def _t5_bucket_table(max_dist):
    d = np.arange(max_dist)
    max_exact = REL_BUCKETS // 2
    tabs = []
    for dt in (np.float32, np.float64):
        ratio = np.maximum(d, 1).astype(dt) / dt(max_exact)
        large = max_exact + (np.log(ratio) / dt(math.log(REL_MAX_DIST / max_exact))
                             * dt(REL_BUCKETS - max_exact)).astype(np.int32)
        large = np.minimum(large, REL_BUCKETS - 1)
        tabs.append(np.where(d < max_exact, d, large).astype(np.int32))
    assert np.array_equal(tabs[0], tabs[1])
    return tabs[0]


def _bucket_maps():
    tab = _t5_bucket_table(Q_TILE + K_TILE)
    s = np.arange(K_TILE)[:, None]
    t = np.arange(Q_TILE)[None, :]
    diag = tab[np.maximum(t - s, 0)]
    prev = tab[K_TILE + t - s]
    assert np.all(tab[K_TILE:] == REL_BUCKETS - 1)
    return np.stack([prev, diag]).astype(np.int32)


def _dsa_body(rb_ref, iq_ref, sm_ref, aq_ref, az_ref, ik_ref, k_ref, vt_ref, map_ref, out_ref,
              sc_ref, stat_ref, iqm_ref, qm_ref, bias_ref, acc_ref, m_ref, l_ref, alpha_ref, mtile_ref, lg_ref,
              p_ref):
    TQ, TK, H, DH = Q_TILE, K_TILE, A_HEADS, A_DH
    j = pl.program_id(1)
    n_tiles = j + 1

    @pl.when((pl.program_id(0) == 0) & (j == 0))
    def _():
        bias_ref[...] = jnp.zeros(bias_ref.shape, F32)

        def fill(bkt, carry):
            for which in range(2):
                hit = map_ref[which] == bkt
                for h in range(H):
                    val = (rb_ref[bkt, h] - rb_ref[REL_BUCKETS - 1, h]) * LOG2E
                    bias_ref[which, h] = jnp.where(hit, val, bias_ref[which, h])
            return carry

        lax.fori_loop(0, REL_BUCKETS - 1, fill, 0)

    lane = lax.broadcasted_iota(I32, (TQ, 2 * DH), 1)
    for h in range(H):
        p0 = (h // 2) * 2 * DH
        keep = (lane < DH) if h % 2 == 0 else (lane >= DH)
        iqm_ref[h] = jnp.where(keep, iq_ref[:, p0:p0 + 2 * DH], jnp.zeros((), BF16))
        qm_ref[h] = jnp.where(keep, aq_ref[:, p0:p0 + 2 * DH], jnp.zeros((), BF16))
    w_rows = sm_ref[...].T[IDX_W_LANE:IDX_W_LANE + IDX_HEADS, :]
    clamp_lo = jnp.where(w_rows >= 0.0, 0.0, -jnp.inf)
    clamp_hi = jnp.where(w_rows >= 0.0, jnp.inf, 0.0)

    s_loc = lax.broadcasted_iota(I32, (TK, TQ), 0)
    t_loc = lax.broadcasted_iota(I32, (TK, TQ), 1)

    fold = lambda v: v.reshape(v.shape[0] // 8, 8, TQ)

    def index_tile(kb, diag, stats):
        r0 = pl.multiple_of(kb * TK, TK)
        lhs = ik_ref[pl.ds(r0, TK), :]
        sc = None
        for h in range(H):
            y = _dot_nt(lhs, iqm_ref[h])
            term = jnp.minimum(jnp.maximum(y, clamp_lo[h:h + 1, :]), clamp_hi[h:h + 1, :])
            sc = term if sc is None else sc + term
        sc_min = sc
        if diag:
            admissible = s_loc <= t_loc
            sc_min = jnp.where(admissible, sc, -NEG_BIG)
            sc = jnp.where(admissible, sc, NEG_BIG)
        sc_ref[pl.ds(r0, TK), :] = sc
        c_pos_, c_nn_, mx_, mn_ = stats
        c_pos_ = c_pos_ + jnp.sum(fold(jnp.where(sc > 0.0, 1.0, 0.0)), axis=0)
        c_nn_ = c_nn_ + jnp.sum(fold(jnp.where(sc >= 0.0, 1.0, 0.0)), axis=0)
        mx_ = jnp.maximum(mx_, jnp.max(fold(sc), axis=0))
        mn_ = jnp.minimum(mn_, jnp.min(fold(sc_min), axis=0))
        return c_pos_, c_nn_, mx_, mn_

    stat_ref[0] = jnp.zeros((8, TQ), F32)
    stat_ref[1] = jnp.zeros((8, TQ), F32)
    stat_ref[2] = jnp.full((8, TQ), NEG_BIG, F32)
    stat_ref[3] = jnp.full((8, TQ), -NEG_BIG, F32)

    def index_tiles(tiles):
        st = tuple(stat_ref[i] for i in range(4))
        for kb, diag in tiles:
            st = index_tile(kb, diag, st)
        for i in range(4):
            stat_ref[i] = st[i]

    def index_triples(i, carry):
        index_tiles([(3 * i, False), (3 * i + 1, False), (3 * i + 2, False)])
        return carry

    n_triples = j // 3
    lax.fori_loop(0, n_triples, index_triples, 0)
    left = j - 3 * n_triples

    @pl.when(left == 2)
    def _():
        index_tiles([(j - 2, False), (j - 1, False), (j, True)])

    @pl.when(left == 1)
    def _():
        index_tiles([(j - 1, False), (j, True)])

    @pl.when(left == 0)
    def _():
        index_tiles([(j, True)])

    c_pos = jnp.sum(stat_ref[0], axis=0, keepdims=True).astype(I32)
    c_nn = jnp.sum(stat_ref[1], axis=0, keepdims=True).astype(I32)
    hi0 = jnp.max(stat_ref[2], axis=0, keepdims=True)
    mn = jnp.min(stat_ref[3], axis=0, keepdims=True)

    @pl.when((n_tiles & 1) == 1)
    def _():
        sc_ref[pl.ds(pl.multiple_of(n_tiles * TK, TK), TK), :] = jnp.full((TK, TQ), NEG_BIG, F32)

    n_pairs = (n_tiles + 1) >> 1

    def count_gt(thr):
        def body(kp, cnt):
            r0 = pl.multiple_of(kp * (2 * TK), 2 * TK)
            hit = jnp.where(sc_ref[pl.ds(r0, 2 * TK), :] > thr, 1, 0).astype(I32)
            return cnt + jnp.sum(hit.reshape(2 * TK // 32, 32, TQ), axis=0)
        cnt32 = lax.fori_loop(1, n_pairs, body, body(0, jnp.zeros((32, TQ), I32)))
        return jnp.sum(cnt32, axis=0, keepdims=True)

    t_glob = j * TQ + lax.broadcasted_iota(I32, (1, TQ), 1)
    k_eff = jnp.minimum(TOPK_MAX, t_glob + 1)

    zero = jnp.zeros((1, TQ), F32)
    all_in = (t_glob + 1) <= TOPK_MAX
    pos_ok = c_pos >= k_eff
    tie0 = (~all_in) & (~pos_ok) & (c_nn >= k_eff)
    lo_neg = jnp.maximum(mn + mn - 1.0, NEG_BIG)
    lo = jnp.where(all_in, NEG_BIG, jnp.where(pos_ok, zero, lo_neg))
    hi = jnp.where((~all_in) & (~pos_ok), zero, hi0)
    c_lo = jnp.where(pos_ok, c_pos, t_glob + 1)
    c_hi = jnp.where(pos_ok, 0, c_pos)
    fin0 = all_in | (pos_ok & (c_pos == k_eff)) | tie0

    def open_rows(fin_, c_lo_, c_hi_):
        return (fin_ == 0.0) & ((c_lo_ - c_hi_) > 2)

    def bis_cond(st):
        return (st[0] < BISECT_MAX_ITERS) & (st[1] > 0)

    def bis_body(st):
        it, _, lo_, hi_, c_lo_, c_hi_, fin_, tie_ = st
        go_n = (jnp.max(jnp.where(open_rows(fin_, c_lo_, c_hi_), 1.0, 0.0)) > 0.0).astype(I32)
        mid = 0.5 * lo_ + 0.5 * hi_
        collapsed = (mid <= lo_) | (mid >= hi_)
        cm = count_gt(mid)
        act = fin_ == 0.0
        up = act & (~collapsed) & (cm >= k_eff)
        dn = act & (~collapsed) & (cm < k_eff)
        tie_n = jnp.where(act & collapsed, 1.0, tie_)
        fin_n = jnp.where((act & collapsed) | (up & (cm == k_eff)), 1.0, fin_)
        return (it + 1, go_n, jnp.where(up, mid, lo_), jnp.where(dn, mid, hi_),
                jnp.where(up, cm, c_lo_), jnp.where(dn, cm, c_hi_), fin_n, tie_n)

    _, _, lo, hi, c_lo, c_hi, fin_f, tie_f = lax.while_loop(
        bis_cond, bis_body,
        (jnp.int32(0), jnp.int32(1), lo, hi, c_lo, c_hi, jnp.where(fin0, 1.0, 0.0), jnp.where(tie0, 1.0, 0.0)))

    def pick_two():
        def body(kp, carry):
            e_min, e_max = carry
            r0 = pl.multiple_of(kp * (2 * TK), 2 * TK)
            sc = sc_ref[pl.ds(r0, 2 * TK), :]
            above = jnp.where(sc > lo, sc, -NEG_BIG).reshape(2 * TK // 32, 32, TQ)
            below_hi = jnp.where(sc <= hi, sc, NEG_BIG).reshape(2 * TK // 32, 32, TQ)
            return jnp.minimum(e_min, jnp.min(above, axis=0)), jnp.maximum(e_max, jnp.max(below_hi, axis=0))
        init = (jnp.full((32, TQ), -NEG_BIG, F32), jnp.full((32, TQ), NEG_BIG, F32))
        e_min, e_max = lax.fori_loop(0, n_pairs, body, init)
        return jnp.min(e_min, axis=0, keepdims=True), jnp.max(e_max, axis=0, keepdims=True)

    e1, e2 = pick_two()
    left = fin_f == 0.0
    lo = jnp.where(left & (e1 < e2), e1, lo)
    hi = jnp.where(left & (e1 >= e2), e1, hi)
    tie_f = jnp.where(left & (e1 >= e2), 1.0, tie_f)
    tie = tie_f > 0.0
    theta_ref = m_ref
    theta_ref[0:1, :] = lo

    @pl.when(jnp.max(tie_f) > 0.0)
    def _():
        v = jnp.where(tie, hi, jnp.inf)
        need = (k_eff - c_hi).astype(F32)
        tri = jnp.where(lax.broadcasted_iota(I32, (TK, TK), 1) < lax.broadcasted_iota(I32, (TK, TK), 0),
                        1.0, 0.0).astype(BF16)

        def tie_pair(kp, seen):
            r0 = pl.multiple_of(kp * (2 * TK), 2 * TK)
            sc2 = sc_ref[pl.ds(r0, 2 * TK), :]
            out = []
            for half in range(2):
                sc = sc2[half * TK:(half + 1) * TK, :]
                eqf = jnp.where(sc == v, 1.0, 0.0)
                rank = seen + _dot(tri, eqf.astype(BF16))
                out.append(jnp.where(eqf * (need - rank) > 0.0, -NEG_BIG, sc))
                seen = seen + jnp.sum(eqf, axis=0, keepdims=True)
            sc_ref[pl.ds(r0, 2 * TK), :] = jnp.concatenate(out, axis=0)
            return seen

        lax.fori_loop(0, n_pairs, tie_pair, jnp.zeros((1, TQ), F32))
        theta_ref[0:1, :] = jnp.where(tie, hi, lo)

    theta = theta_ref[0:1, :]

    m_ref[...] = jnp.full(m_ref.shape, LOGIT_NEG, F32)
    l_ref[...] = jnp.zeros(l_ref.shape, F32)
    acc_ref[...] = jnp.zeros(acc_ref.shape, F32)

    def tile_mask(kb):
        r0 = pl.multiple_of(kb * TK, TK)
        return jnp.where(sc_ref[pl.ds(r0, TK), :] > theta, 0.0, LOGIT_NEG)

    def logits_head(kb, slot, near, mb, h):
        r0 = pl.multiple_of(kb * TK, TK)
        p0 = (h // 2) * 2 * DH
        lg = _dot_nt(k_ref[pl.ds(r0, TK), p0:p0 + 2 * DH], qm_ref[h]) + mb
        if near is not None:
            lg = lg + bias_ref[near, h]
        lg_ref[slot, h] = lg
        m_old = m_ref[h:h + 1, :]
        m_new = jnp.maximum(m_old, jnp.max(lg, axis=0, keepdims=True))
        alpha_ref[slot, h:h + 1, :] = jnp.exp2(m_old - m_new)
        mtile_ref[slot, h:h + 1, :] = m_new
        m_ref[h:h + 1, :] = m_new

    def exp_head(slot, h):
        p_ref[slot, h] = jnp.exp2(lg_ref[slot, h] - mtile_ref[slot, h:h + 1, :]).astype(BF16)

    def pv_head(kb, slot, h):
        r0 = pl.multiple_of(kb * TK, TK)
        ones = jnp.ones((ONES_ROWS, TK), BF16)
        lhs = jnp.concatenate([vt_ref[h * DH:(h + 1) * DH, pl.ds(r0, TK)], ones], axis=0)
        pv = _dot(lhs, p_ref[slot, h])
        alpha = alpha_ref[slot, h:h + 1, :]
        acc_ref[h * DH:(h + 1) * DH, :] = alpha * acc_ref[h * DH:(h + 1) * DH, :] + pv[0:DH, :]
        l_ref[h:h + 1, :] = alpha * l_ref[h:h + 1, :] + pv[DH:DH + 1, :]

    def fill(kb, slot, near):
        mb = tile_mask(kb)
        for h in range(H):
            logits_head(kb, slot, near, mb, h)

    def drain(kb, slot):
        for h in range(H):
            exp_head(slot, h)
        for h in range(H):
            pv_head(kb, slot, h)

    def step(kb_prev, slot_prev, kb_next, slot_next, near_next):
        mb = tile_mask(kb_next)
        for h in range(H):
            exp_head(slot_prev, h)
            logits_head(kb_next, slot_next, near_next, mb, h)
            pv_head(kb_prev, slot_prev, h)

    n_far = j - 1

    @pl.when(n_far >= 1)
    def _():
        fill(0, 0, None)

        def far_loop(t, carry):
            step(2 * t, 0, 2 * t + 1, 1, None)
            step(2 * t + 1, 1, 2 * t + 2, 0, None)
            return carry

        lax.fori_loop(0, (n_far - 1) >> 1, far_loop, 0)

        @pl.when(((n_far - 1) & 1) == 1)
        def _():
            step(j - 3, 0, j - 2, 1, None)
            step(j - 2, 1, j - 1, 0, 0)
            step(j - 1, 0, j, 1, 1)
            drain(j, 1)

        @pl.when(((n_far - 1) & 1) == 0)
        def _():
            step(j - 2, 0, j - 1, 1, 0)
            step(j - 1, 1, j, 0, 1)
            drain(j, 0)

    @pl.when(j == 1)
    def _():
        fill(0, 0, 0)
        step(0, 0, 1, 1, 1)
        drain(1, 1)

    @pl.when(j == 0)
    def _():
        fill(0, 0, 1)
        drain(0, 0)

    for h in range(H):
        acc_ref[h * DH:(h + 1) * DH, :] = acc_ref[h * DH:(h + 1) * DH, :] / l_ref[h:h + 1, :]
    out_ref[...] = (acc_ref[...].T * az_ref[...]).astype(BF16)


def _dsa(rel_bias, iq, sm, aq, az, ik2, ak, avt, batch, seq):
    TQ = Q_TILE
    nq = seq // TQ
    width = A_HEADS * A_DH
    maps = jnp.asarray(_bucket_maps())
    qrow = lambda b, j: (b * nq + j, 0)
    return pl.pallas_call(
        _dsa_body,
        grid=(batch, nq),
        in_specs=[
            pl.BlockSpec(memory_space=pltpu.SMEM),
            pl.BlockSpec((TQ, width), qrow),
            pl.BlockSpec((TQ, 128), qrow),
            pl.BlockSpec((TQ, width), qrow),
            pl.BlockSpec((TQ, width), qrow),
            pl.BlockSpec((seq, 128), lambda b, j: (b, 0), pipeline_mode=pl.Buffered(1)),
            pl.BlockSpec((seq, width), lambda b, j: (b, 0), pipeline_mode=pl.Buffered(1)),
            pl.BlockSpec((None, width, seq), lambda b, j: (b, 0, 0), pipeline_mode=pl.Buffered(1)),
            _resident(maps.shape),
        ],
        out_specs=pl.BlockSpec((TQ, width), qrow),
        out_shape=jax.ShapeDtypeStruct((batch * seq, width), BF16),
        scratch_shapes=[
            pltpu.VMEM((seq, TQ), F32),
            pltpu.VMEM((4, 8, TQ), F32),
            pltpu.VMEM((A_HEADS, TQ, 2 * A_DH), BF16),
            pltpu.VMEM((A_HEADS, TQ, 2 * A_DH), BF16),
            pltpu.VMEM((2, A_HEADS, K_TILE, TQ), F32),
            pltpu.VMEM((A_HEADS * A_DH, TQ), F32),
            pltpu.VMEM((8, TQ), F32),
            pltpu.VMEM((8, TQ), F32),
            pltpu.VMEM((2, 8, TQ), F32),
            pltpu.VMEM((2, 8, TQ), F32),
            pltpu.VMEM((2, A_HEADS, K_TILE, TQ), F32),
            pltpu.VMEM((2, A_HEADS, K_TILE, TQ), BF16),
        ],
        compiler_params=pltpu.CompilerParams(
            dimension_semantics=("arbitrary", "arbitrary"), vmem_limit_bytes=VMEM_LIMIT),
        name="dsa",
    )(rel_bias, iq, sm, aq, az, ik2, ak, avt, maps)


def _outproj_body(x_ref, hm_ref, ha_ref, p_ref, wom_ref, woa_ref, wple_ref, wg_ref, g_ref, o_ref):
    h1 = x_ref[...] + _dot(hm_ref[...], wom_ref[...]) + _dot(ha_ref[...], woa_ref[...])
    gate = jax.nn.sigmoid(_dot(h1.astype(BF16), wg_ref[...]))
    h2 = h1 + _dot(p_ref[...].astype(BF16), wple_ref[...]) * gate
    ms = jnp.mean(h2 * h2, axis=-1, keepdims=True)
    o_ref[...] = h2 * lax.rsqrt(ms + EPS) * g_ref[...]


def _outproj(x2, hm, ha, p2, wom, woa, wple, wg, g):
    rows, d = x2.shape
    tm = ROW_TILE
    tile = lambda n: pl.BlockSpec((tm, n), lambda i: (i, 0))
    return pl.pallas_call(
        _outproj_body,
        grid=(rows // tm,),
        in_specs=[tile(d), tile(hm.shape[1]), tile(ha.shape[1]), tile(p2.shape[1]),
                  _resident(wom.shape), _resident(woa.shape), _resident(wple.shape),
                  _resident(wg.shape), _resident(g.shape)],
        out_specs=tile(d),
        out_shape=jax.ShapeDtypeStruct((rows, d), F32),
        compiler_params=pltpu.CompilerParams(
            dimension_semantics=("arbitrary",), vmem_limit_bytes=VMEM_LIMIT),
        name="outproj",
    )(x2, hm, ha, p2, wom, woa, wple, wg, g)


def kernel(x, p, w_in, b_gate, conv_w, w_out, norm_in, m_norm, rel_bias, w_ple, w_ple_gate, norm_final):
    batch, seq, d = x.shape
    assert w_in.shape[0] == 1 and p.shape[0] == 1, "single-layer block"
    assert seq % Q_TILE == 0 and seq % M_CHUNK == 0 and (batch * seq) % ROW_TILE == 0
    rows = batch * seq
    x2 = x.reshape(rows, d)

    sizes = [M_HEADS * M_QK, M_HEADS * M_QK, M_HEADS * M_V, M_HEADS * M_V, M_HEADS * M_V, 2 * M_HEADS,
             A_HEADS * A_DH, A_HEADS * A_DH, A_HEADS * A_DH, A_HEADS * A_DH,
             IDX_HEADS * IDX_DH, IDX_DH, IDX_HEADS]
    offs = np.concatenate([[0], np.cumsum(sizes)])
    wi = w_in[0]
    col = lambda i: wi[:, int(offs[i]):int(offs[i + 1])]
    w_if, w_ixk, w_ixw = col(5), col(11), col(12)

    def small_tile(parts, n_rows):
        out = jnp.zeros((n_rows, LANES), F32)
        for lane0, block in parts.items():
            out = out.at[:, lane0:lane0 + block.shape[1]].set(block)
        return out

    w_sm = small_tile({GATE_I_LANE: w_if[:, :M_HEADS], GATE_F_LANE: w_if[:, M_HEADS:], IDX_W_LANE: w_ixw}, d)
    w_ik2 = jnp.concatenate([w_ixk, w_ixk], axis=1)
    ws = [jnp.concatenate([col(0), col(1)], axis=1), col(2), col(3), col(4), w_sm, w_ik2,
          col(6), col(7), col(8), col(9), col(10)]
    ws = [w.astype(BF16) for w in ws]
    bg = b_gate[0][None, :]
    bpad = small_tile({GATE_I_LANE: bg[:, :M_HEADS], GATE_F_LANE: bg[:, M_HEADS:]}, 1)

    mqk, mv, gate, sm, ik2, aq, ak, avt, az, iq = _inproj(x2, norm_in[0][None, :], ws, batch, seq)

    hm = _mlstm(mqk, mv, gate, sm, conv_w[0], bpad, m_norm[0][None, :], batch, seq)

    ha = _dsa(rel_bias, iq, sm, aq, az, ik2, ak, avt, batch, seq)

    wo = w_out[0].astype(BF16)
    out = _outproj(x2, hm, ha, p[0].reshape(rows, -1), wo[:M_HEADS * M_V], wo[M_HEADS * M_V:],
                   w_ple[0].astype(BF16), w_ple_gate[0].astype(BF16), norm_final[None, :])
    return out.reshape(batch, seq, d)
```

```python
import math

import numpy as np
import jax
import jax.numpy as jnp
from jax import lax
from jax.experimental import pallas as pl
from jax.experimental.pallas import tpu as pltpu

F32 = jnp.float32
BF16 = jnp.bfloat16
I32 = jnp.int32

M_HEADS = 4
M_QK = 128
M_V = 256
CONV_W = 4
A_HEADS = 8
A_DH = 64
IDX_HEADS = 8
IDX_DH = 64
TOPK_MAX = 256
REL_BUCKETS = 32
REL_MAX_DIST = 128
EPS = 1e-6

LOG2E = 1.4426950408889634
NEG_BIG = -3.0e38
LOGIT_NEG = -1.0e30

ROW_TILE = 512
M_CHUNK = 512
Q_TILE = 256
K_TILE = 256
ONES_ROWS = 16
CONV_HALO = 8
LANES = 128
GATE_I_LANE = 0
GATE_F_LANE = 8
IDX_W_LANE = 16
assert GATE_I_LANE == 0
BISECT_MAX_ITERS = 256
VMEM_LIMIT = 56 * 1024 * 1024


def _silu(v):
    return v * jax.nn.sigmoid(v)


def _dot(a, b):
    return jnp.dot(a, b, preferred_element_type=F32)


def _dot_nt(a, b):
    return lax.dot_general(a, b, (((1,), (1,)), ((), ())), preferred_element_type=F32)


def _dot_tn(a, b):
    return lax.dot_general(a, b, (((0,), (0,)), ((), ())), preferred_element_type=F32)


def _resident(shape):
    nd = len(shape)
    return pl.BlockSpec(shape, lambda *_: (0,) * nd, pipeline_mode=pl.Buffered(1))


def _inproj_body(x_ref, g_ref, w_mqk, w_mv, w_mo, w_mz, w_sm, w_ik, w_aq, w_ak, w_av, w_az, w_iq,
                 o_mqk, o_mv, o_gate, o_sm, o_ik, o_aq, o_ak, o_avt, o_az, o_iq):
    x = x_ref[...]
    ms = jnp.mean(x * x, axis=-1, keepdims=True)
    xn = (x * lax.rsqrt(ms + EPS) * g_ref[...]).astype(BF16)
    o_mqk[...] = _dot(xn, w_mqk[...])
    o_mv[...] = _dot(xn, w_mv[...]).astype(BF16)
    o_gate[...] = jax.nn.sigmoid(_dot(xn, w_mo[...])) * _silu(_dot(xn, w_mz[...]))
    sm = _dot(xn, w_sm[...])
    o_sm[...] = sm
    o_ik[...] = _dot(xn, w_ik[...]).astype(BF16)
    o_aq[...] = (_dot(xn, w_aq[...]) * (A_DH ** -0.5 * LOG2E)).astype(BF16)
    o_ak[...] = _dot(xn, w_ak[...]).astype(BF16)
    o_avt[...] = _dot(xn, w_av[...]).T.astype(BF16)
    o_az[...] = _silu(_dot(xn, w_az[...]))
    iq = _dot(xn, w_iq[...])
    wsc = sm[:, IDX_W_LANE:IDX_W_LANE + IDX_HEADS] * (IDX_HEADS ** -0.5 * IDX_DH ** -0.5)
    lane = lax.broadcasted_iota(I32, (x.shape[0], 2 * IDX_DH), 1)
    for pair in range(IDX_HEADS // 2):
        wpair = jnp.where(lane < IDX_DH, wsc[:, 2 * pair:2 * pair + 1], wsc[:, 2 * pair + 1:2 * pair + 2])
        sl = slice(pair * 2 * IDX_DH, (pair + 1) * 2 * IDX_DH)
        o_iq[:, sl] = (iq[:, sl] * wpair).astype(BF16)


def _inproj(x2, g, ws, batch, seq):
    rows, d = x2.shape
    tm = ROW_TILE
    per_seq = seq // tm
    out_dtypes = [F32, BF16, F32, F32, BF16, BF16, BF16, BF16, F32, BF16]
    widths = [w.shape[1] for w in ws]
    out_widths = [widths[0], widths[1], widths[2]] + widths[4:]
    in_specs = [pl.BlockSpec((tm, d), lambda i: (i, 0)), _resident((1, d))]
    in_specs += [_resident(w.shape) for w in ws]
    out_specs = [pl.BlockSpec((tm, n), lambda i: (i, 0)) for n in out_widths]
    out_shape = [jax.ShapeDtypeStruct((rows, n), dt) for n, dt in zip(out_widths, out_dtypes)]
    av_pos = 7
    out_specs[av_pos] = pl.BlockSpec((None, out_widths[av_pos], tm), lambda i: (i // per_seq, 0, i % per_seq))
    out_shape[av_pos] = jax.ShapeDtypeStruct((batch, out_widths[av_pos], seq), BF16)
    return pl.pallas_call(
        _inproj_body,
        grid=(rows // tm,),
        in_specs=in_specs,
        out_specs=out_specs,
        out_shape=out_shape,
        compiler_params=pltpu.CompilerParams(
            dimension_semantics=("arbitrary",), vmem_limit_bytes=VMEM_LIMIT),
        name="inproj",
    )(x2, g, *ws)


def _row_scan(v, op, fill):
    n = v.shape[0]
    row = lax.broadcasted_iota(I32, v.shape, 0)
    sh = 1
    while sh < n:
        v = op(v, jnp.where(row >= sh, pltpu.roll(v, sh, 0), fill))
        sh *= 2
    return v


def _mlstm_body(mqk_ref, mv_ref, gate_ref, sm_ref, convw_ref, bpad_ref, mnorm_ref, out_ref,
                halo_ref, c_ref, n_ref, m_ref):
    L = M_CHUNK
    H, DK, DV = M_HEADS, M_QK, M_V

    @pl.when(pl.program_id(1) == 0)
    def _():
        halo_ref[...] = jnp.zeros(halo_ref.shape, F32)
        c_ref[...] = jnp.zeros(c_ref.shape, F32)
        n_ref[...] = jnp.zeros(n_ref.shape, F32)
        m_ref[...] = jnp.zeros(m_ref.shape, F32)

    raw = mqk_ref[...]
    halo = halo_ref[...]
    row8 = lax.broadcasted_iota(I32, halo.shape, 0)
    y = raw * convw_ref[CONV_W - 1:CONV_W, :]
    for back in range(1, CONV_W):
        rot = pltpu.roll(raw, back, 0)
        head = jnp.where(row8 < back, pltpu.roll(halo, back, 0), rot[0:CONV_HALO, :])
        tap = jnp.concatenate([head, rot[CONV_HALO:, :]], axis=0)
        y = y + tap * convw_ref[CONV_W - 1 - back:CONV_W - back, :]
    halo_ref[...] = raw[L - CONV_HALO:L, :]
    qk = _silu(y)

    g = sm_ref[...] + bpad_ref[...]
    lf = jax.nn.log_sigmoid(pltpu.roll(g, LANES - (GATE_F_LANE - GATE_I_LANE), 1))
    b = _row_scan(lf, jnp.add, 0.0)
    a = g - b
    m_prev = m_ref[0:1, :]
    mcol = jnp.maximum(m_prev, _row_scan(a, jnp.maximum, -jnp.inf))
    w_inter = jnp.exp(m_prev - mcol)
    e_neg_mt = jnp.exp(-(b + mcol))
    m_last = mcol[L - 1:L, :]
    wk = jnp.exp(a - m_last)
    decay = jnp.exp(m_prev - m_last)
    m_new = b[L - 1:L, :] + m_last
    a_rows = a.T

    ri = lax.broadcasted_iota(I32, (L, L), 0)
    ci = lax.broadcasted_iota(I32, (L, L), 1)
    causal = ci <= ri

    heads = range(H)
    q = [qk[:, h * DK:(h + 1) * DK] for h in heads]
    k = [qk[:, (H + h) * DK:(H + h + 1) * DK] * (DK ** -0.5) for h in heads]
    qb = [q[h].astype(BF16) for h in heads]
    v = [mv_ref[:, h * DV:(h + 1) * DV] for h in heads]
    c_old = [c_ref[h] for h in heads]
    qkt = [_dot_nt(qb[h], k[h].astype(BF16)) for h in heads]
    q_c = [_dot(qb[h], c_old[h].astype(BF16)) for h in heads]
    s = [qkt[h] * jnp.where(causal, jnp.exp(a_rows[h:h + 1, :] - mcol[:, h:h + 1]), 0.0) for h in heads]
    num = [_dot(s[h].astype(BF16), v[h]) + w_inter[:, h:h + 1] * q_c[h] for h in heads]
    kw = [k[h] * wk[:, h:h + 1] for h in heads]
    upd = [_dot_tn(kw[h].astype(BF16), v[h]) for h in heads]
    for h in heads:
        den = (jnp.sum(s[h], axis=1, keepdims=True)
               + w_inter[:, h:h + 1] * jnp.sum(q[h] * n_ref[h:h + 1, :], axis=1, keepdims=True))
        den = jnp.maximum(jnp.abs(den), e_neg_mt[:, h:h + 1])
        hh = num[h] / den
        hn = hh * lax.rsqrt(jnp.mean(hh * hh, axis=-1, keepdims=True) + EPS)
        hn = hn * mnorm_ref[:, h * DV:(h + 1) * DV]
        out_ref[:, h * DV:(h + 1) * DV] = (gate_ref[:, h * DV:(h + 1) * DV] * hn).astype(BF16)
    for h in heads:
        dec = decay[:, h:h + 1]
        c_ref[h] = dec * c_old[h] + upd[h]
        n_ref[h:h + 1, :] = dec * n_ref[h:h + 1, :] + jnp.sum(kw[h], axis=0, keepdims=True)
    m_ref[...] = jnp.broadcast_to(m_new, m_ref.shape)


def _mlstm(mqk, mv, gate, sm, conv_w, bpad, mnorm, batch, seq):
    L = M_CHUNK
    nc = seq // L
    wqk = mqk.shape[1]
    wv = mv.shape[1]
    row = lambda b, c: (b * nc + c, 0)
    return pl.pallas_call(
        _mlstm_body,
        grid=(batch, nc),
        in_specs=[
            pl.BlockSpec((L, wqk), row),
            pl.BlockSpec((L, wv), row),
            pl.BlockSpec((L, wv), row),
            pl.BlockSpec((L, 128), row),
            _resident(conv_w.shape),
            _resident(bpad.shape),
            _resident(mnorm.shape),
        ],
        out_specs=pl.BlockSpec((L, wv), row),
        out_shape=jax.ShapeDtypeStruct((batch * seq, wv), BF16),
        scratch_shapes=[
            pltpu.VMEM((CONV_HALO, wqk), F32),
            pltpu.VMEM((M_HEADS, M_QK, M_V), F32),
            pltpu.VMEM((8, M_QK), F32),
            pltpu.VMEM((8, 128), F32),
        ],
        compiler_params=pltpu.CompilerParams(
            dimension_semantics=("arbitrary", "arbitrary"), vmem_limit_bytes=VMEM_LIMIT),
        name="mlstm",
    )(mqk, mv, gate, sm, conv_w, bpad, mnorm)


def _t5_bucket_table(max_dist):
    d = np.arange(max_dist)
    max_exact = REL_BUCKETS // 2
    tabs = []
    for dt in (np.float32, np.float64):
        ratio = np.maximum(d, 1).astype(dt) / dt(max_exact)
        large = max_exact + (np.log(ratio) / dt(math.log(REL_MAX_DIST / max_exact))
                             * dt(REL_BUCKETS - max_exact)).astype(np.int32)
        large = np.minimum(large, REL_BUCKETS - 1)
        tabs.append(np.where(d < max_exact, d, large).astype(np.int32))
    assert np.array_equal(tabs[0], tabs[1])
    return tabs[0]


def _bucket_maps():
    tab = _t5_bucket_table(Q_TILE + K_TILE)
    s = np.arange(K_TILE)[:, None]
    t = np.arange(Q_TILE)[None, :]
    diag = tab[np.maximum(t - s, 0)]
    prev = tab[K_TILE + t - s]
    assert np.all(tab[K_TILE:] == REL_BUCKETS - 1)
    return np.stack([prev, diag]).astype(np.int32)


def _dsa_body(rb_ref, iq_ref, sm_ref, aq_ref, az_ref, ik_ref, k_ref, vt_ref, map_ref, out_ref,
              sc_ref, stat_ref, iqm_ref, qm_ref, bias_ref, acc_ref, m_ref, l_ref, alpha_ref, mtile_ref, lg_ref,
              p_ref):
    TQ, TK, H, DH = Q_TILE, K_TILE, A_HEADS, A_DH
    j = pl.program_id(1)
    n_tiles = j + 1

    @pl.when((pl.program_id(0) == 0) & (j == 0))
    def _():
        bias_ref[...] = jnp.zeros(bias_ref.shape, F32)

        def fill(bkt, carry):
            for which in range(2):
                hit = map_ref[which] == bkt
                for h in range(H):
                    val = (rb_ref[bkt, h] - rb_ref[REL_BUCKETS - 1, h]) * LOG2E
                    bias_ref[which, h] = jnp.where(hit, val, bias_ref[which, h])
            return carry

        lax.fori_loop(0, REL_BUCKETS - 1, fill, 0)

    lane = lax.broadcasted_iota(I32, (TQ, 2 * DH), 1)
    for h in range(H):
        p0 = (h // 2) * 2 * DH
        keep = (lane < DH) if h % 2 == 0 else (lane >= DH)
        iqm_ref[h] = jnp.where(keep, iq_ref[:, p0:p0 + 2 * DH], jnp.zeros((), BF16))
        qm_ref[h] = jnp.where(keep, aq_ref[:, p0:p0 + 2 * DH], jnp.zeros((), BF16))
    w_rows = sm_ref[...].T[IDX_W_LANE:IDX_W_LANE + IDX_HEADS, :]
    clamp_lo = jnp.where(w_rows >= 0.0, 0.0, -jnp.inf)
    clamp_hi = jnp.where(w_rows >= 0.0, jnp.inf, 0.0)

    s_loc = lax.broadcasted_iota(I32, (TK, TQ), 0)
    t_loc = lax.broadcasted_iota(I32, (TK, TQ), 1)

    fold = lambda v: v.reshape(v.shape[0] // 8, 8, TQ)

    def index_tile(kb, diag, stats):
        r0 = pl.multiple_of(kb * TK, TK)
        lhs = ik_ref[pl.ds(r0, TK), :]
        sc = None
        for h in range(H):
            y = _dot_nt(lhs, iqm_ref[h])
            term = jnp.minimum(jnp.maximum(y, clamp_lo[h:h + 1, :]), clamp_hi[h:h + 1, :])
            sc = term if sc is None else sc + term
        sc_min = sc
        if diag:
            admissible = s_loc <= t_loc
            sc_min = jnp.where(admissible, sc, -NEG_BIG)
            sc = jnp.where(admissible, sc, NEG_BIG)
        sc_ref[pl.ds(r0, TK), :] = sc
        c_pos_, c_nn_, mx_, mn_ = stats
        c_pos_ = c_pos_ + jnp.sum(fold(jnp.where(sc > 0.0, 1.0, 0.0)), axis=0)
        c_nn_ = c_nn_ + jnp.sum(fold(jnp.where(sc >= 0.0, 1.0, 0.0)), axis=0)
        mx_ = jnp.maximum(mx_, jnp.max(fold(sc), axis=0))
        mn_ = jnp.minimum(mn_, jnp.min(fold(sc_min), axis=0))
        return c_pos_, c_nn_, mx_, mn_

    stat_ref[0] = jnp.zeros((8, TQ), F32)
    stat_ref[1] = jnp.zeros((8, TQ), F32)
    stat_ref[2] = jnp.full((8, TQ), NEG_BIG, F32)
    stat_ref[3] = jnp.full((8, TQ), -NEG_BIG, F32)

    def index_tiles(tiles):
        st = tuple(stat_ref[i] for i in range(4))
        for kb, diag in tiles:
            st = index_tile(kb, diag, st)
        for i in range(4):
            stat_ref[i] = st[i]

    def index_triples(i, carry):
        index_tiles([(3 * i, False), (3 * i + 1, False), (3 * i + 2, False)])
        return carry

    n_triples = j // 3
    lax.fori_loop(0, n_triples, index_triples, 0)
    left = j - 3 * n_triples

    @pl.when(left == 2)
    def _():
        index_tiles([(j - 2, False), (j - 1, False), (j, True)])

    @pl.when(left == 1)
    def _():
        index_tiles([(j - 1, False), (j, True)])

    @pl.when(left == 0)
    def _():
        index_tiles([(j, True)])

    c_pos = jnp.sum(stat_ref[0], axis=0, keepdims=True).astype(I32)
    c_nn = jnp.sum(stat_ref[1], axis=0, keepdims=True).astype(I32)
    hi0 = jnp.max(stat_ref[2], axis=0, keepdims=True)
    mn = jnp.min(stat_ref[3], axis=0, keepdims=True)

    @pl.when((n_tiles & 1) == 1)
    def _():
        sc_ref[pl.ds(pl.multiple_of(n_tiles * TK, TK), TK), :] = jnp.full((TK, TQ), NEG_BIG, F32)

    n_pairs = (n_tiles + 1) >> 1

    def count_gt(thr):
        def body(kp, cnt):
            r0 = pl.multiple_of(kp * (2 * TK), 2 * TK)
            hit = jnp.where(sc_ref[pl.ds(r0, 2 * TK), :] > thr, 1, 0).astype(I32)
            return cnt + jnp.sum(hit.reshape(2 * TK // 32, 32, TQ), axis=0)
        cnt32 = lax.fori_loop(1, n_pairs, body, body(0, jnp.zeros((32, TQ), I32)))
        return jnp.sum(cnt32, axis=0, keepdims=True)

    t_glob = j * TQ + lax.broadcasted_iota(I32, (1, TQ), 1)
    k_eff = jnp.minimum(TOPK_MAX, t_glob + 1)

    zero = jnp.zeros((1, TQ), F32)
    all_in = (t_glob + 1) <= TOPK_MAX
    pos_ok = c_pos >= k_eff
    tie0 = (~all_in) & (~pos_ok) & (c_nn >= k_eff)
    lo_neg = jnp.maximum(mn + mn - 1.0, NEG_BIG)
    lo = jnp.where(all_in, NEG_BIG, jnp.where(pos_ok, zero, lo_neg))
    hi = jnp.where((~all_in) & (~pos_ok), zero, hi0)
    c_lo = jnp.where(pos_ok, c_pos, t_glob + 1)
    c_hi = jnp.where(pos_ok, 0, c_pos)
    fin0 = all_in | (pos_ok & (c_pos == k_eff)) | tie0

    def open_rows(fin_, c_lo_, c_hi_):
        return (fin_ == 0.0) & ((c_lo_ - c_hi_) > 2)

    def bis_cond(st):
        return (st[0] < BISECT_MAX_ITERS) & (st[1] > 0)

    def bis_body(st):
        it, _, lo_, hi_, c_lo_, c_hi_, fin_, tie_ = st
        go_n = (jnp.max(jnp.where(open_rows(fin_, c_lo_, c_hi_), 1.0, 0.0)) > 0.0).astype(I32)
        mid = 0.5 * lo_ + 0.5 * hi_
        collapsed = (mid <= lo_) | (mid >= hi_)
        cm = count_gt(mid)
        act = fin_ == 0.0
        up = act & (~collapsed) & (cm >= k_eff)
        dn = act & (~collapsed) & (cm < k_eff)
        tie_n = jnp.where(act & collapsed, 1.0, tie_)
        fin_n = jnp.where((act & collapsed) | (up & (cm == k_eff)), 1.0, fin_)
        return (it + 1, go_n, jnp.where(up, mid, lo_), jnp.where(dn, mid, hi_),
                jnp.where(up, cm, c_lo_), jnp.where(dn, cm, c_hi_), fin_n, tie_n)

    _, _, lo, hi, c_lo, c_hi, fin_f, tie_f = lax.while_loop(
        bis_cond, bis_body,
        (jnp.int32(0), jnp.int32(1), lo, hi, c_lo, c_hi, jnp.where(fin0, 1.0, 0.0), jnp.where(tie0, 1.0, 0.0)))

    def pick_two():
        def body(kp, carry):
            e_min, e_max = carry
            r0 = pl.multiple_of(kp * (2 * TK), 2 * TK)
            sc = sc_ref[pl.ds(r0, 2 * TK), :]
            above = jnp.where(sc > lo, sc, -NEG_BIG).reshape(2 * TK // 32, 32, TQ)
            below_hi = jnp.where(sc <= hi, sc, NEG_BIG).reshape(2 * TK // 32, 32, TQ)
            return jnp.minimum(e_min, jnp.min(above, axis=0)), jnp.maximum(e_max, jnp.max(below_hi, axis=0))
        init = (jnp.full((32, TQ), -NEG_BIG, F32), jnp.full((32, TQ), NEG_BIG, F32))
        e_min, e_max = lax.fori_loop(0, n_pairs, body, init)
        return jnp.min(e_min, axis=0, keepdims=True), jnp.max(e_max, axis=0, keepdims=True)

    e1, e2 = pick_two()
    left = fin_f == 0.0
    lo = jnp.where(left & (e1 < e2), e1, lo)
    hi = jnp.where(left & (e1 >= e2), e1, hi)
    tie_f = jnp.where(left & (e1 >= e2), 1.0, tie_f)
    tie = tie_f > 0.0
    theta_ref = m_ref
    theta_ref[0:1, :] = lo

    @pl.when(jnp.max(tie_f) > 0.0)
    def _():
        v = jnp.where(tie, hi, jnp.inf)
        need = (k_eff - c_hi).astype(F32)
        tri = jnp.where(lax.broadcasted_iota(I32, (TK, TK), 1) < lax.broadcasted_iota(I32, (TK, TK), 0),
                        1.0, 0.0).astype(BF16)

        def tie_pair(kp, seen):
            r0 = pl.multiple_of(kp * (2 * TK), 2 * TK)
            sc2 = sc_ref[pl.ds(r0, 2 * TK), :]
            out = []
            for half in range(2):
                sc = sc2[half * TK:(half + 1) * TK, :]
                eqf = jnp.where(sc == v, 1.0, 0.0)
                rank = seen + _dot(tri, eqf.astype(BF16))
                out.append(jnp.where(eqf * (need - rank) > 0.0, -NEG_BIG, sc))
                seen = seen + jnp.sum(eqf, axis=0, keepdims=True)
            sc_ref[pl.ds(r0, 2 * TK), :] = jnp.concatenate(out, axis=0)
            return seen

        lax.fori_loop(0, n_pairs, tie_pair, jnp.zeros((1, TQ), F32))
        theta_ref[0:1, :] = jnp.where(tie, hi, lo)

    theta = theta_ref[0:1, :]

    m_ref[...] = jnp.full(m_ref.shape, LOGIT_NEG, F32)
    l_ref[...] = jnp.zeros(l_ref.shape, F32)
    acc_ref[...] = jnp.zeros(acc_ref.shape, F32)

    def tile_mask(kb):
        r0 = pl.multiple_of(kb * TK, TK)
        return jnp.where(sc_ref[pl.ds(r0, TK), :] > theta, 0.0, LOGIT_NEG)

    def logits_head(kb, slot, near, mb, h):
        r0 = pl.multiple_of(kb * TK, TK)
        p0 = (h // 2) * 2 * DH
        lg = _dot_nt(k_ref[pl.ds(r0, TK), p0:p0 + 2 * DH], qm_ref[h]) + mb
        if near is not None:
            lg = lg + bias_ref[near, h]
        lg_ref[slot, h] = lg
        m_old = m_ref[h:h + 1, :]
        m_new = jnp.maximum(m_old, jnp.max(lg, axis=0, keepdims=True))
        alpha_ref[slot, h:h + 1, :] = jnp.exp2(m_old - m_new)
        mtile_ref[slot, h:h + 1, :] = m_new
        m_ref[h:h + 1, :] = m_new

    def exp_head(slot, h):
        p_ref[slot, h] = jnp.exp2(lg_ref[slot, h] - mtile_ref[slot, h:h + 1, :]).astype(BF16)

    def pv_head(kb, slot, h):
        r0 = pl.multiple_of(kb * TK, TK)
        ones = jnp.ones((ONES_ROWS, TK), BF16)
        lhs = jnp.concatenate([vt_ref[h * DH:(h + 1) * DH, pl.ds(r0, TK)], ones], axis=0)
        pv = _dot(lhs, p_ref[slot, h])
        alpha = alpha_ref[slot, h:h + 1, :]
        acc_ref[h * DH:(h + 1) * DH, :] = alpha * acc_ref[h * DH:(h + 1) * DH, :] + pv[0:DH, :]
        l_ref[h:h + 1, :] = alpha * l_ref[h:h + 1, :] + pv[DH:DH + 1, :]

    def fill(kb, slot, near):
        mb = tile_mask(kb)
        for h in range(H):
            logits_head(kb, slot, near, mb, h)

    def drain(kb, slot):
        for h in range(H):
            exp_head(slot, h)
        for h in range(H):
            pv_head(kb, slot, h)

    def step(kb_prev, slot_prev, kb_next, slot_next, near_next):
        mb = tile_mask(kb_next)
        for h in range(H):
            exp_head(slot_prev, h)
            logits_head(kb_next, slot_next, near_next, mb, h)
            pv_head(kb_prev, slot_prev, h)

    n_far = j - 1

    @pl.when(n_far >= 1)
    def _():
        fill(0, 0, None)

        def far_loop(t, carry):
            step(2 * t, 0, 2 * t + 1, 1, None)
            step(2 * t + 1, 1, 2 * t + 2, 0, None)
            return carry

        lax.fori_loop(0, (n_far - 1) >> 1, far_loop, 0)

        @pl.when(((n_far - 1) & 1) == 1)
        def _():
            step(j - 3, 0, j - 2, 1, None)
            step(j - 2, 1, j - 1, 0, 0)
            step(j - 1, 0, j, 1, 1)
            drain(j, 1)

        @pl.when(((n_far - 1) & 1) == 0)
        def _():
            step(j - 2, 0, j - 1, 1, 0)
            step(j - 1, 1, j, 0, 1)
            drain(j, 0)

    @pl.when(j == 1)
    def _():
        fill(0, 0, 0)
        step(0, 0, 1, 1, 1)
        drain(1, 1)

    @pl.when(j == 0)
    def _():
        fill(0, 0, 1)
        drain(0, 0)

    for h in range(H):
        acc_ref[h * DH:(h + 1) * DH, :] = acc_ref[h * DH:(h + 1) * DH, :] / l_ref[h:h + 1, :]
    out_ref[...] = (acc_ref[...].T * az_ref[...]).astype(BF16)


def _dsa(rel_bias, iq, sm, aq, az, ik2, ak, avt, batch, seq):
    TQ = Q_TILE
    nq = seq // TQ
    width = A_HEADS * A_DH
    maps = jnp.asarray(_bucket_maps())
    qrow = lambda b, j: (b * nq + j, 0)
    return pl.pallas_call(
        _dsa_body,
        grid=(batch, nq),
        in_specs=[
            pl.BlockSpec(memory_space=pltpu.SMEM),
            pl.BlockSpec((TQ, width), qrow),
            pl.BlockSpec((TQ, 128), qrow),
            pl.BlockSpec((TQ, width), qrow),
            pl.BlockSpec((TQ, width), qrow),
            pl.BlockSpec((seq, 128), lambda b, j: (b, 0), pipeline_mode=pl.Buffered(1)),
            pl.BlockSpec((seq, width), lambda b, j: (b, 0), pipeline_mode=pl.Buffered(1)),
            pl.BlockSpec((None, width, seq), lambda b, j: (b, 0, 0), pipeline_mode=pl.Buffered(1)),
            _resident(maps.shape),
        ],
        out_specs=pl.BlockSpec((TQ, width), qrow),
        out_shape=jax.ShapeDtypeStruct((batch * seq, width), BF16),
        scratch_shapes=[
            pltpu.VMEM((seq, TQ), F32),
            pltpu.VMEM((4, 8, TQ), F32),
            pltpu.VMEM((A_HEADS, TQ, 2 * A_DH), BF16),
            pltpu.VMEM((A_HEADS, TQ, 2 * A_DH), BF16),
            pltpu.VMEM((2, A_HEADS, K_TILE, TQ), F32),
            pltpu.VMEM((A_HEADS * A_DH, TQ), F32),
            pltpu.VMEM((8, TQ), F32),
            pltpu.VMEM((8, TQ), F32),
            pltpu.VMEM((2, 8, TQ), F32),
            pltpu.VMEM((2, 8, TQ), F32),
            pltpu.VMEM((2, A_HEADS, K_TILE, TQ), F32),
            pltpu.VMEM((2, A_HEADS, K_TILE, TQ), BF16),
        ],
        compiler_params=pltpu.CompilerParams(
            dimension_semantics=("arbitrary", "arbitrary"), vmem_limit_bytes=VMEM_LIMIT),
        name="dsa",
    )(rel_bias, iq, sm, aq, az, ik2, ak, avt, maps)


def _outproj_body(x_ref, hm_ref, ha_ref, p_ref, wom_ref, woa_ref, wple_ref, wg_ref, g_ref, o_ref):
    h1 = x_ref[...] + _dot(hm_ref[...], wom_ref[...]) + _dot(ha_ref[...], woa_ref[...])
    gate = jax.nn.sigmoid(_dot(h1.astype(BF16), wg_ref[...]))
    h2 = h1 + _dot(p_ref[...].astype(BF16), wple_ref[...]) * gate
    ms = jnp.mean(h2 * h2, axis=-1, keepdims=True)
    o_ref[...] = h2 * lax.rsqrt(ms + EPS) * g_ref[...]


def _outproj(x2, hm, ha, p2, wom, woa, wple, wg, g):
    rows, d = x2.shape
    tm = ROW_TILE
    tile = lambda n: pl.BlockSpec((tm, n), lambda i: (i, 0))
    return pl.pallas_call(
        _outproj_body,
        grid=(rows // tm,),
        in_specs=[tile(d), tile(hm.shape[1]), tile(ha.shape[1]), tile(p2.shape[1]),
                  _resident(wom.shape), _resident(woa.shape), _resident(wple.shape),
                  _resident(wg.shape), _resident(g.shape)],
        out_specs=tile(d),
        out_shape=jax.ShapeDtypeStruct((rows, d), F32),
        compiler_params=pltpu.CompilerParams(
            dimension_semantics=("arbitrary",), vmem_limit_bytes=VMEM_LIMIT),
        name="outproj",
    )(x2, hm, ha, p2, wom, woa, wple, wg, g)


def kernel(x, p, w_in, b_gate, conv_w, w_out, norm_in, m_norm, rel_bias, w_ple, w_ple_gate, norm_final):
    batch, seq, d = x.shape
    assert w_in.shape[0] == 1 and p.shape[0] == 1, "single-layer block"
    assert seq % Q_TILE == 0 and seq % M_CHUNK == 0 and (batch * seq) % ROW_TILE == 0
    rows = batch * seq
    x2 = x.reshape(rows, d)

    sizes = [M_HEADS * M_QK, M_HEADS * M_QK, M_HEADS * M_V, M_HEADS * M_V, M_HEADS * M_V, 2 * M_HEADS,
             A_HEADS * A_DH, A_HEADS * A_DH, A_HEADS * A_DH, A_HEADS * A_DH,
             IDX_HEADS * IDX_DH, IDX_DH, IDX_HEADS]
    offs = np.concatenate([[0], np.cumsum(sizes)])
    wi = w_in[0]
    col = lambda i: wi[:, int(offs[i]):int(offs[i + 1])]
    w_if, w_ixk, w_ixw = col(5), col(11), col(12)

    def small_tile(parts, n_rows):
        out = jnp.zeros((n_rows, LANES), F32)
        for lane0, block in parts.items():
            out = out.at[:, lane0:lane0 + block.shape[1]].set(block)
        return out

    w_sm = small_tile({GATE_I_LANE: w_if[:, :M_HEADS], GATE_F_LANE: w_if[:, M_HEADS:], IDX_W_LANE: w_ixw}, d)
    w_ik2 = jnp.concatenate([w_ixk, w_ixk], axis=1)
    ws = [jnp.concatenate([col(0), col(1)], axis=1), col(2), col(3), col(4), w_sm, w_ik2,
          col(6), col(7), col(8), col(9), col(10)]
    ws = [w.astype(BF16) for w in ws]
    bg = b_gate[0][None, :]
    bpad = small_tile({GATE_I_LANE: bg[:, :M_HEADS], GATE_F_LANE: bg[:, M_HEADS:]}, 1)

    mqk, mv, gate, sm, ik2, aq, ak, avt, az, iq = _inproj(x2, norm_in[0][None, :], ws, batch, seq)

    hm = _mlstm(mqk, mv, gate, sm, conv_w[0], bpad, m_norm[0][None, :], batch, seq)

    ha = _dsa(rel_bias, iq, sm, aq, az, ik2, ak, avt, batch, seq)

    wo = w_out[0].astype(BF16)
    out = _outproj(x2, hm, ha, p[0].reshape(rows, -1), wo[:M_HEADS * M_V], wo[M_HEADS * M_V:],
                   w_ple[0].astype(BF16), w_ple_gate[0].astype(BF16), norm_final[None, :])
    return out.reshape(batch, seq, d)
```

```python
import math

import numpy as np
import jax
import jax.numpy as jnp
from jax import lax
from jax.experimental import pallas as pl
from jax.experimental.pallas import tpu as pltpu

F32 = jnp.float32
BF16 = jnp.bfloat16
I32 = jnp.int32

M_HEADS = 4
M_QK = 128
M_V = 256
CONV_W = 4
A_HEADS = 8
A_DH = 64
IDX_HEADS = 8
IDX_DH = 64
TOPK_MAX = 256
REL_BUCKETS = 32
REL_MAX_DIST = 128
EPS = 1e-6

LOG2E = 1.4426950408889634
NEG_BIG = -3.0e38
LOGIT_NEG = -1.0e30

ROW_TILE = 512
M_CHUNK = 512
Q_TILE = 256
K_TILE = 256
ONES_ROWS = 16
CONV_HALO = 8
LANES = 128
GATE_I_LANE = 0
GATE_F_LANE = 8
IDX_W_LANE = 16
assert GATE_I_LANE == 0
BISECT_MAX_ITERS = 256
VMEM_LIMIT = 56 * 1024 * 1024


def _silu(v):
    return v * jax.nn.sigmoid(v)


def _dot(a, b):
    return jnp.dot(a, b, preferred_element_type=F32)


def _dot_nt(a, b):
    return lax.dot_general(a, b, (((1,), (1,)), ((), ())), preferred_element_type=F32)


def _dot_tn(a, b):
    return lax.dot_general(a, b, (((0,), (0,)), ((), ())), preferred_element_type=F32)


def _resident(shape):
    nd = len(shape)
    return pl.BlockSpec(shape, lambda *_: (0,) * nd, pipeline_mode=pl.Buffered(1))


def _inproj_body(x_ref, g_ref, w_mqk, w_mv, w_mo, w_mz, w_sm, w_ik, w_aq, w_ak, w_av, w_az, w_iq,
                 o_mqk, o_mv, o_gate, o_sm, o_ik, o_aq, o_ak, o_avt, o_az, o_iq):
    x = x_ref[...]
    ms = jnp.mean(x * x, axis=-1, keepdims=True)
    xn = (x * lax.rsqrt(ms + EPS) * g_ref[...]).astype(BF16)
    o_mqk[...] = _dot(xn, w_mqk[...])
    o_mv[...] = _dot(xn, w_mv[...]).astype(BF16)
    o_gate[...] = jax.nn.sigmoid(_dot(xn, w_mo[...])) * _silu(_dot(xn, w_mz[...]))
    sm = _dot(xn, w_sm[...])
    o_sm[...] = sm
    o_ik[...] = _dot(xn, w_ik[...]).astype(BF16)
    o_aq[...] = (_dot(xn, w_aq[...]) * (A_DH ** -0.5 * LOG2E)).astype(BF16)
    o_ak[...] = _dot(xn, w_ak[...]).astype(BF16)
    o_avt[...] = _dot(xn, w_av[...]).T.astype(BF16)
    o_az[...] = _silu(_dot(xn, w_az[...]))
    iq = _dot(xn, w_iq[...])
    wsc = sm[:, IDX_W_LANE:IDX_W_LANE + IDX_HEADS] * (IDX_HEADS ** -0.5 * IDX_DH ** -0.5)
    lane = lax.broadcasted_iota(I32, (x.shape[0], 2 * IDX_DH), 1)
    for pair in range(IDX_HEADS // 2):
        wpair = jnp.where(lane < IDX_DH, wsc[:, 2 * pair:2 * pair + 1], wsc[:, 2 * pair + 1:2 * pair + 2])
        sl = slice(pair * 2 * IDX_DH, (pair + 1) * 2 * IDX_DH)
        o_iq[:, sl] = (iq[:, sl] * wpair).astype(BF16)


def _inproj(x2, g, ws, batch, seq):
    rows, d = x2.shape
    tm = ROW_TILE
    per_seq = seq // tm
    out_dtypes = [F32, BF16, F32, F32, BF16, BF16, BF16, BF16, F32, BF16]
    widths = [w.shape[1] for w in ws]
    out_widths = [widths[0], widths[1], widths[2]] + widths[4:]
    in_specs = [pl.BlockSpec((tm, d), lambda i: (i, 0)), _resident((1, d))]
    in_specs += [_resident(w.shape) for w in ws]
    out_specs = [pl.BlockSpec((tm, n), lambda i: (i, 0)) for n in out_widths]
    out_shape = [jax.ShapeDtypeStruct((rows, n), dt) for n, dt in zip(out_widths, out_dtypes)]
    av_pos = 7
    out_specs[av_pos] = pl.BlockSpec((None, out_widths[av_pos], tm), lambda i: (i // per_seq, 0, i % per_seq))
    out_shape[av_pos] = jax.ShapeDtypeStruct((batch, out_widths[av_pos], seq), BF16)
    return pl.pallas_call(
        _inproj_body,
        grid=(rows // tm,),
        in_specs=in_specs,
        out_specs=out_specs,
        out_shape=out_shape,
        compiler_params=pltpu.CompilerParams(
            dimension_semantics=("arbitrary",), vmem_limit_bytes=VMEM_LIMIT),
        name="inproj",
    )(x2, g, *ws)


def _row_scan(v, op, fill):
    n = v.shape[0]
    row = lax.broadcasted_iota(I32, v.shape, 0)
    sh = 1
    while sh < n:
        v = op(v, jnp.where(row >= sh, pltpu.roll(v, sh, 0), fill))
        sh *= 2
    return v


def _mlstm_body(mqk_ref, mv_ref, gate_ref, sm_ref, convw_ref, bpad_ref, mnorm_ref, out_ref,
                halo_ref, c_ref, n_ref, m_ref):
    L = M_CHUNK
    H, DK, DV = M_HEADS, M_QK, M_V

    @pl.when(pl.program_id(1) == 0)
    def _():
        halo_ref[...] = jnp.zeros(halo_ref.shape, F32)
        c_ref[...] = jnp.zeros(c_ref.shape, F32)
        n_ref[...] = jnp.zeros(n_ref.shape, F32)
        m_ref[...] = jnp.zeros(m_ref.shape, F32)

    raw = mqk_ref[...]
    halo = halo_ref[...]
    row8 = lax.broadcasted_iota(I32, halo.shape, 0)
    y = raw * convw_ref[CONV_W - 1:CONV_W, :]
    for back in range(1, CONV_W):
        rot = pltpu.roll(raw, back, 0)
        head = jnp.where(row8 < back, pltpu.roll(halo, back, 0), rot[0:CONV_HALO, :])
        tap = jnp.concatenate([head, rot[CONV_HALO:, :]], axis=0)
        y = y + tap * convw_ref[CONV_W - 1 - back:CONV_W - back, :]
    halo_ref[...] = raw[L - CONV_HALO:L, :]
    qk = _silu(y)

    g = sm_ref[...] + bpad_ref[...]
    lf = jax.nn.log_sigmoid(pltpu.roll(g, LANES - (GATE_F_LANE - GATE_I_LANE), 1))
    b = _row_scan(lf, jnp.add, 0.0)
    a = g - b
    m_prev = m_ref[0:1, :]
    mcol = jnp.maximum(m_prev, _row_scan(a, jnp.maximum, -jnp.inf))
    w_inter = jnp.exp(m_prev - mcol)
    e_neg_mt = jnp.exp(-(b + mcol))
    m_last = mcol[L - 1:L, :]
    wk = jnp.exp(a - m_last)
    decay = jnp.exp(m_prev - m_last)
    m_new = b[L - 1:L, :] + m_last
    a_rows = a.T

    ri = lax.broadcasted_iota(I32, (L, L), 0)
    ci = lax.broadcasted_iota(I32, (L, L), 1)
    causal = ci <= ri

    heads = range(H)
    q = [qk[:, h * DK:(h + 1) * DK] for h in heads]
    k = [qk[:, (H + h) * DK:(H + h + 1) * DK] * (DK ** -0.5) for h in heads]
    qb = [q[h].astype(BF16) for h in heads]
    v = [mv_ref[:, h * DV:(h + 1) * DV] for h in heads]
    c_old = [c_ref[h] for h in heads]
    qkt = [_dot_nt(qb[h], k[h].astype(BF16)) for h in heads]
    q_c = [_dot(qb[h], c_old[h].astype(BF16)) for h in heads]
    s = [qkt[h] * jnp.where(causal, jnp.exp(a_rows[h:h + 1, :] - mcol[:, h:h + 1]), 0.0) for h in heads]
    num = [_dot(s[h].astype(BF16), v[h]) + w_inter[:, h:h + 1] * q_c[h] for h in heads]
    kw = [k[h] * wk[:, h:h + 1] for h in heads]
    upd = [_dot_tn(kw[h].astype(BF16), v[h]) for h in heads]
    for h in heads:
        den = (jnp.sum(s[h], axis=1, keepdims=True)
               + w_inter[:, h:h + 1] * jnp.sum(q[h] * n_ref[h:h + 1, :], axis=1, keepdims=True))
        den = jnp.maximum(jnp.abs(den), e_neg_mt[:, h:h + 1])
        hh = num[h] / den
        hn = hh * lax.rsqrt(jnp.mean(hh * hh, axis=-1, keepdims=True) + EPS)
        hn = hn * mnorm_ref[:, h * DV:(h + 1) * DV]
        out_ref[:, h * DV:(h + 1) * DV] = (gate_ref[:, h * DV:(h + 1) * DV] * hn).astype(BF16)
    for h in heads:
        dec = decay[:, h:h + 1]
        c_ref[h] = dec * c_old[h] + upd[h]
        n_ref[h:h + 1, :] = dec * n_ref[h:h + 1, :] + jnp.sum(kw[h], axis=0, keepdims=True)
    m_ref[...] = jnp.broadcast_to(m_new, m_ref.shape)


def _mlstm(mqk, mv, gate, sm, conv_w, bpad, mnorm, batch, seq):
    L = M_CHUNK
    nc = seq // L
    wqk = mqk.shape[1]
    wv = mv.shape[1]
    row = lambda b, c: (b * nc + c, 0)
    return pl.pallas_call(
        _mlstm_body,
        grid=(batch, nc),
        in_specs=[
            pl.BlockSpec((L, wqk), row),
            pl.BlockSpec((L, wv), row),
            pl.BlockSpec((L, wv), row),
            pl.BlockSpec((L, 128), row),
            _resident(conv_w.shape),
            _resident(bpad.shape),
            _resident(mnorm.shape),
        ],
        out_specs=pl.BlockSpec((L, wv), row),
        out_shape=jax.ShapeDtypeStruct((batch * seq, wv), BF16),
        scratch_shapes=[
            pltpu.VMEM((CONV_HALO, wqk), F32),
            pltpu.VMEM((M_HEADS, M_QK, M_V), F32),
            pltpu.VMEM((8, M_QK), F32),
            pltpu.VMEM((8, 128), F32),
        ],
        compiler_params=pltpu.CompilerParams(
            dimension_semantics=("arbitrary", "arbitrary"), vmem_limit_bytes=VMEM_LIMIT),
        name="mlstm",
    )(mqk, mv, gate, sm, conv_w, bpad, mnorm)


def _t5_bucket_table(max_dist):
    d = np.arange(max_dist)
    max_exact = REL_BUCKETS // 2
    tabs = []
    for dt in (np.float32, np.float64):
        ratio = np.maximum(d, 1).astype(dt) / dt(max_exact)
        large = max_exact + (np.log(ratio) / dt(math.log(REL_MAX_DIST / max_exact))
                             * dt(REL_BUCKETS - max_exact)).astype(np.int32)
        large = np.minimum(large, REL_BUCKETS - 1)
        tabs.append(np.where(d < max_exact, d, large).astype(np.int32))
    assert np.array_equal(tabs[0], tabs[1])
    return tabs[0]


def _bucket_maps():
    tab = _t5_bucket_table(Q_TILE + K_TILE)
    s = np.arange(K_TILE)[:, None]
    t = np.arange(Q_TILE)[None, :]
    diag = tab[np.maximum(t - s, 0)]
    prev = tab[K_TILE + t - s]
    assert np.all(tab[K_TILE:] == REL_BUCKETS - 1)
    return np.stack([prev, diag]).astype(np.int32)


def _dsa_body(rb_ref, iq_ref, sm_ref, aq_ref, az_ref, ik_ref, k_ref, vt_ref, map_ref, out_ref,
              sc_ref, stat_ref, iqm_ref, qm_ref, bias_ref, acc_ref, m_ref, l_ref, alpha_ref, mtile_ref, lg_ref,
              p_ref):
    TQ, TK, H, DH = Q_TILE, K_TILE, A_HEADS, A_DH
    j = pl.program_id(1)
    n_tiles = j + 1

    @pl.when((pl.program_id(0) == 0) & (j == 0))
    def _():
        bias_ref[...] = jnp.zeros(bias_ref.shape, F32)

        def fill(bkt, carry):
            for which in range(2):
                hit = map_ref[which] == bkt
                for h in range(H):
                    val = (rb_ref[bkt, h] - rb_ref[REL_BUCKETS - 1, h]) * LOG2E
                    bias_ref[which, h] = jnp.where(hit, val, bias_ref[which, h])
            return carry

        lax.fori_loop(0, REL_BUCKETS - 1, fill, 0)

    lane = lax.broadcasted_iota(I32, (TQ, 2 * DH), 1)
    for h in range(H):
        p0 = (h // 2) * 2 * DH
        keep = (lane < DH) if h % 2 == 0 else (lane >= DH)
        iqm_ref[h] = jnp.where(keep, iq_ref[:, p0:p0 + 2 * DH], jnp.zeros((), BF16))
        qm_ref[h] = jnp.where(keep, aq_ref[:, p0:p0 + 2 * DH], jnp.zeros((), BF16))
    w_rows = sm_ref[...].T[IDX_W_LANE:IDX_W_LANE + IDX_HEADS, :]
    clamp_lo = jnp.where(w_rows >= 0.0, 0.0, -jnp.inf)
    clamp_hi = jnp.where(w_rows >= 0.0, jnp.inf, 0.0)

    s_loc = lax.broadcasted_iota(I32, (TK, TQ), 0)
    t_loc = lax.broadcasted_iota(I32, (TK, TQ), 1)

    fold = lambda v: v.reshape(v.shape[0] // 8, 8, TQ)

    def index_tile(kb, diag, stats):
        r0 = pl.multiple_of(kb * TK, TK)
        lhs = ik_ref[pl.ds(r0, TK), :]
        sc = None
        for h in range(H):
            y = _dot_nt(lhs, iqm_ref[h])
            term = jnp.minimum(jnp.maximum(y, clamp_lo[h:h + 1, :]), clamp_hi[h:h + 1, :])
            sc = term if sc is None else sc + term
        sc_min = sc
        if diag:
            admissible = s_loc <= t_loc
            sc_min = jnp.where(admissible, sc, -NEG_BIG)
            sc = jnp.where(admissible, sc, NEG_BIG)
        sc_ref[pl.ds(r0, TK), :] = sc
        c_pos_, c_nn_, mx_, mn_ = stats
        c_pos_ = c_pos_ + jnp.sum(fold(jnp.where(sc > 0.0, 1.0, 0.0)), axis=0)
        c_nn_ = c_nn_ + jnp.sum(fold(jnp.where(sc >= 0.0, 1.0, 0.0)), axis=0)
        mx_ = jnp.maximum(mx_, jnp.max(fold(sc), axis=0))
        mn_ = jnp.minimum(mn_, jnp.min(fold(sc_min), axis=0))
        return c_pos_, c_nn_, mx_, mn_

    stat_ref[0] = jnp.zeros((8, TQ), F32)
    stat_ref[1] = jnp.zeros((8, TQ), F32)
    stat_ref[2] = jnp.full((8, TQ), NEG_BIG, F32)
    stat_ref[3] = jnp.full((8, TQ), -NEG_BIG, F32)

    def index_tiles(tiles):
        st = tuple(stat_ref[i] for i in range(4))
        for kb, diag in tiles:
            st = index_tile(kb, diag, st)
        for i in range(4):
            stat_ref[i] = st[i]

    def index_quads(i, carry):
        index_tiles([(4 * i + o, False) for o in range(4)])
        return carry

    n_quads = j >> 2
    lax.fori_loop(0, n_quads, index_quads, 0)
    left = j & 3

    for n_left in range(4):
        @pl.when(left == n_left)
        def _(n_left=n_left):
            index_tiles([(j - o, False) for o in range(n_left, 0, -1)] + [(j, True)])

    c_pos = jnp.sum(stat_ref[0], axis=0, keepdims=True).astype(I32)
    c_nn = jnp.sum(stat_ref[1], axis=0, keepdims=True).astype(I32)
    hi0 = jnp.max(stat_ref[2], axis=0, keepdims=True)
    mn = jnp.min(stat_ref[3], axis=0, keepdims=True)

    @pl.when((n_tiles & 1) == 1)
    def _():
        sc_ref[pl.ds(pl.multiple_of(n_tiles * TK, TK), TK), :] = jnp.full((TK, TQ), NEG_BIG, F32)

    n_pairs = (n_tiles + 1) >> 1

    def count_gt(thr):
        def body(kp, cnt):
            r0 = pl.multiple_of(kp * (2 * TK), 2 * TK)
            hit = jnp.where(sc_ref[pl.ds(r0, 2 * TK), :] > thr, 1, 0).astype(I32)
            return cnt + jnp.sum(hit.reshape(2 * TK // 32, 32, TQ), axis=0)
        cnt32 = lax.fori_loop(1, n_pairs, body, body(0, jnp.zeros((32, TQ), I32)))
        return jnp.sum(cnt32, axis=0, keepdims=True)

    t_glob = j * TQ + lax.broadcasted_iota(I32, (1, TQ), 1)
    k_eff = jnp.minimum(TOPK_MAX, t_glob + 1)

    zero = jnp.zeros((1, TQ), F32)
    all_in = (t_glob + 1) <= TOPK_MAX
    pos_ok = c_pos >= k_eff
    tie0 = (~all_in) & (~pos_ok) & (c_nn >= k_eff)
    lo_neg = jnp.maximum(mn + mn - 1.0, NEG_BIG)
    lo = jnp.where(all_in, NEG_BIG, jnp.where(pos_ok, zero, lo_neg))
    hi = jnp.where((~all_in) & (~pos_ok), zero, hi0)
    c_lo = jnp.where(pos_ok, c_pos, t_glob + 1)
    c_hi = jnp.where(pos_ok, 0, c_pos)
    fin0 = all_in | (pos_ok & (c_pos == k_eff)) | tie0

    def open_rows(fin_, c_lo_, c_hi_):
        return (fin_ == 0.0) & ((c_lo_ - c_hi_) > 2)

    def bis_cond(st):
        return (st[0] < BISECT_MAX_ITERS) & (st[1] > 0)

    def bis_body(st):
        it, _, lo_, hi_, c_lo_, c_hi_, fin_, tie_ = st
        go_n = (jnp.max(jnp.where(open_rows(fin_, c_lo_, c_hi_), 1.0, 0.0)) > 0.0).astype(I32)
        mid = 0.5 * lo_ + 0.5 * hi_
        collapsed = (mid <= lo_) | (mid >= hi_)
        cm = count_gt(mid)
        act = fin_ == 0.0
        up = act & (~collapsed) & (cm >= k_eff)
        dn = act & (~collapsed) & (cm < k_eff)
        tie_n = jnp.where(act & collapsed, 1.0, tie_)
        fin_n = jnp.where((act & collapsed) | (up & (cm == k_eff)), 1.0, fin_)
        return (it + 1, go_n, jnp.where(up, mid, lo_), jnp.where(dn, mid, hi_),
                jnp.where(up, cm, c_lo_), jnp.where(dn, cm, c_hi_), fin_n, tie_n)

    _, _, lo, hi, c_lo, c_hi, fin_f, tie_f = lax.while_loop(
        bis_cond, bis_body,
        (jnp.int32(0), jnp.int32(1), lo, hi, c_lo, c_hi, jnp.where(fin0, 1.0, 0.0), jnp.where(tie0, 1.0, 0.0)))

    def pick_two():
        def body(kp, carry):
            e_min, e_max = carry
            r0 = pl.multiple_of(kp * (2 * TK), 2 * TK)
            sc = sc_ref[pl.ds(r0, 2 * TK), :]
            above = jnp.where(sc > lo, sc, -NEG_BIG).reshape(2 * TK // 32, 32, TQ)
            below_hi = jnp.where(sc <= hi, sc, NEG_BIG).reshape(2 * TK // 32, 32, TQ)
            return jnp.minimum(e_min, jnp.min(above, axis=0)), jnp.maximum(e_max, jnp.max(below_hi, axis=0))
        init = (jnp.full((32, TQ), -NEG_BIG, F32), jnp.full((32, TQ), NEG_BIG, F32))
        e_min, e_max = lax.fori_loop(0, n_pairs, body, init)
        return jnp.min(e_min, axis=0, keepdims=True), jnp.max(e_max, axis=0, keepdims=True)

    e1, e2 = pick_two()
    left = fin_f == 0.0
    lo = jnp.where(left & (e1 < e2), e1, lo)
    hi = jnp.where(left & (e1 >= e2), e1, hi)
    tie_f = jnp.where(left & (e1 >= e2), 1.0, tie_f)
    tie = tie_f > 0.0
    theta_ref = m_ref
    theta_ref[0:1, :] = lo

    @pl.when(jnp.max(tie_f) > 0.0)
    def _():
        v = jnp.where(tie, hi, jnp.inf)
        need = (k_eff - c_hi).astype(F32)
        tri = jnp.where(lax.broadcasted_iota(I32, (TK, TK), 1) < lax.broadcasted_iota(I32, (TK, TK), 0),
                        1.0, 0.0).astype(BF16)

        def tie_pair(kp, seen):
            r0 = pl.multiple_of(kp * (2 * TK), 2 * TK)
            sc2 = sc_ref[pl.ds(r0, 2 * TK), :]
            out = []
            for half in range(2):
                sc = sc2[half * TK:(half + 1) * TK, :]
                eqf = jnp.where(sc == v, 1.0, 0.0)
                rank = seen + _dot(tri, eqf.astype(BF16))
                out.append(jnp.where(eqf * (need - rank) > 0.0, -NEG_BIG, sc))
                seen = seen + jnp.sum(eqf, axis=0, keepdims=True)
            sc_ref[pl.ds(r0, 2 * TK), :] = jnp.concatenate(out, axis=0)
            return seen

        lax.fori_loop(0, n_pairs, tie_pair, jnp.zeros((1, TQ), F32))
        theta_ref[0:1, :] = jnp.where(tie, hi, lo)

    theta = theta_ref[0:1, :]

    m_ref[...] = jnp.full(m_ref.shape, LOGIT_NEG, F32)
    l_ref[...] = jnp.zeros(l_ref.shape, F32)
    acc_ref[...] = jnp.zeros(acc_ref.shape, F32)

    def tile_mask(kb):
        r0 = pl.multiple_of(kb * TK, TK)
        return jnp.where(sc_ref[pl.ds(r0, TK), :] > theta, 0.0, LOGIT_NEG)

    def logits_head(kb, slot, near, mb, h):
        r0 = pl.multiple_of(kb * TK, TK)
        p0 = (h // 2) * 2 * DH
        lg = _dot_nt(k_ref[pl.ds(r0, TK), p0:p0 + 2 * DH], qm_ref[h]) + mb
        if near is not None:
            lg = lg + bias_ref[near, h]
        lg_ref[slot, h] = lg
        m_old = m_ref[h:h + 1, :]
        m_new = jnp.maximum(m_old, jnp.max(lg, axis=0, keepdims=True))
        alpha_ref[slot, h:h + 1, :] = jnp.exp2(m_old - m_new)
        mtile_ref[slot, h:h + 1, :] = m_new
        m_ref[h:h + 1, :] = m_new

    def exp_head(slot, h):
        p_ref[slot, h] = jnp.exp2(lg_ref[slot, h] - mtile_ref[slot, h:h + 1, :]).astype(BF16)

    def pv_head(kb, slot, h):
        r0 = pl.multiple_of(kb * TK, TK)
        ones = jnp.ones((ONES_ROWS, TK), BF16)
        lhs = jnp.concatenate([vt_ref[h * DH:(h + 1) * DH, pl.ds(r0, TK)], ones], axis=0)
        pv = _dot(lhs, p_ref[slot, h])
        alpha = alpha_ref[slot, h:h + 1, :]
        acc_ref[h * DH:(h + 1) * DH, :] = alpha * acc_ref[h * DH:(h + 1) * DH, :] + pv[0:DH, :]
        l_ref[h:h + 1, :] = alpha * l_ref[h:h + 1, :] + pv[DH:DH + 1, :]

    def fill(kb, slot, near):
        mb = tile_mask(kb)
        for h in range(H):
            logits_head(kb, slot, near, mb, h)

    def drain(kb, slot):
        for h in range(H):
            exp_head(slot, h)
        for h in range(H):
            pv_head(kb, slot, h)

    def step(kb_prev, slot_prev, kb_next, slot_next, near_next):
        mb = tile_mask(kb_next)
        for h in range(H):
            exp_head(slot_prev, h)
            logits_head(kb_next, slot_next, near_next, mb, h)
            pv_head(kb_prev, slot_prev, h)

    n_far = j - 1

    @pl.when(n_far >= 1)
    def _():
        fill(0, 0, None)

        def far_loop(t, carry):
            step(2 * t, 0, 2 * t + 1, 1, None)
            step(2 * t + 1, 1, 2 * t + 2, 0, None)
            return carry

        lax.fori_loop(0, (n_far - 1) >> 1, far_loop, 0)

        @pl.when(((n_far - 1) & 1) == 1)
        def _():
            step(j - 3, 0, j - 2, 1, None)
            step(j - 2, 1, j - 1, 0, 0)
            step(j - 1, 0, j, 1, 1)
            drain(j, 1)

        @pl.when(((n_far - 1) & 1) == 0)
        def _():
            step(j - 2, 0, j - 1, 1, 0)
            step(j - 1, 1, j, 0, 1)
            drain(j, 0)

    @pl.when(j == 1)
    def _():
        fill(0, 0, 0)
        step(0, 0, 1, 1, 1)
        drain(1, 1)

    @pl.when(j == 0)
    def _():
        fill(0, 0, 1)
        drain(0, 0)

    for h in range(H):
        acc_ref[h * DH:(h + 1) * DH, :] = acc_ref[h * DH:(h + 1) * DH, :] / l_ref[h:h + 1, :]
    out_ref[...] = (acc_ref[...].T * az_ref[...]).astype(BF16)


def _dsa(rel_bias, iq, sm, aq, az, ik2, ak, avt, batch, seq):
    TQ = Q_TILE
    nq = seq // TQ
    width = A_HEADS * A_DH
    maps = jnp.asarray(_bucket_maps())
    qrow = lambda b, j: (b * nq + j, 0)
    return pl.pallas_call(
        _dsa_body,
        grid=(batch, nq),
        in_specs=[
            pl.BlockSpec(memory_space=pltpu.SMEM),
            pl.BlockSpec((TQ, width), qrow),
            pl.BlockSpec((TQ, 128), qrow),
            pl.BlockSpec((TQ, width), qrow),
            pl.BlockSpec((TQ, width), qrow),
            pl.BlockSpec((seq, 128), lambda b, j: (b, 0), pipeline_mode=pl.Buffered(1)),
            pl.BlockSpec((seq, width), lambda b, j: (b, 0), pipeline_mode=pl.Buffered(1)),
            pl.BlockSpec((None, width, seq), lambda b, j: (b, 0, 0), pipeline_mode=pl.Buffered(1)),
            _resident(maps.shape),
        ],
        out_specs=pl.BlockSpec((TQ, width), qrow),
        out_shape=jax.ShapeDtypeStruct((batch * seq, width), BF16),
        scratch_shapes=[
            pltpu.VMEM((seq, TQ), F32),
            pltpu.VMEM((4, 8, TQ), F32),
            pltpu.VMEM((A_HEADS, TQ, 2 * A_DH), BF16),
            pltpu.VMEM((A_HEADS, TQ, 2 * A_DH), BF16),
            pltpu.VMEM((2, A_HEADS, K_TILE, TQ), F32),
            pltpu.VMEM((A_HEADS * A_DH, TQ), F32),
            pltpu.VMEM((8, TQ), F32),
            pltpu.VMEM((8, TQ), F32),
            pltpu.VMEM((2, 8, TQ), F32),
            pltpu.VMEM((2, 8, TQ), F32),
            pltpu.VMEM((2, A_HEADS, K_TILE, TQ), F32),
            pltpu.VMEM((2, A_HEADS, K_TILE, TQ), BF16),
        ],
        compiler_params=pltpu.CompilerParams(
            dimension_semantics=("arbitrary", "arbitrary"), vmem_limit_bytes=VMEM_LIMIT),
        name="dsa",
    )(rel_bias, iq, sm, aq, az, ik2, ak, avt, maps)


def _outproj_body(x_ref, hm_ref, ha_ref, p_ref, wom_ref, woa_ref, wple_ref, wg_ref, g_ref, o_ref):
    h1 = x_ref[...] + _dot(hm_ref[...], wom_ref[...]) + _dot(ha_ref[...], woa_ref[...])
    gate = jax.nn.sigmoid(_dot(h1.astype(BF16), wg_ref[...]))
    h2 = h1 + _dot(p_ref[...].astype(BF16), wple_ref[...]) * gate
    ms = jnp.mean(h2 * h2, axis=-1, keepdims=True)
    o_ref[...] = h2 * lax.rsqrt(ms + EPS) * g_ref[...]


def _outproj(x2, hm, ha, p2, wom, woa, wple, wg, g):
    rows, d = x2.shape
    tm = ROW_TILE
    tile = lambda n: pl.BlockSpec((tm, n), lambda i: (i, 0))
    return pl.pallas_call(
        _outproj_body,
        grid=(rows // tm,),
        in_specs=[tile(d), tile(hm.shape[1]), tile(ha.shape[1]), tile(p2.shape[1]),
                  _resident(wom.shape), _resident(woa.shape), _resident(wple.shape),
                  _resident(wg.shape), _resident(g.shape)],
        out_specs=tile(d),
        out_shape=jax.ShapeDtypeStruct((rows, d), F32),
        compiler_params=pltpu.CompilerParams(
            dimension_semantics=("arbitrary",), vmem_limit_bytes=VMEM_LIMIT),
        name="outproj",
    )(x2, hm, ha, p2, wom, woa, wple, wg, g)


def kernel(x, p, w_in, b_gate, conv_w, w_out, norm_in, m_norm, rel_bias, w_ple, w_ple_gate, norm_final):
    batch, seq, d = x.shape
    assert w_in.shape[0] == 1 and p.shape[0] == 1, "single-layer block"
    assert seq % Q_TILE == 0 and seq % M_CHUNK == 0 and (batch * seq) % ROW_TILE == 0
    rows = batch * seq
    x2 = x.reshape(rows, d)

    sizes = [M_HEADS * M_QK, M_HEADS * M_QK, M_HEADS * M_V, M_HEADS * M_V, M_HEADS * M_V, 2 * M_HEADS,
             A_HEADS * A_DH, A_HEADS * A_DH, A_HEADS * A_DH, A_HEADS * A_DH,
             IDX_HEADS * IDX_DH, IDX_DH, IDX_HEADS]
    offs = np.concatenate([[0], np.cumsum(sizes)])
    wi = w_in[0]
    col = lambda i: wi[:, int(offs[i]):int(offs[i + 1])]
    w_if, w_ixk, w_ixw = col(5), col(11), col(12)

    def small_tile(parts, n_rows):
        out = jnp.zeros((n_rows, LANES), F32)
        for lane0, block in parts.items():
            out = out.at[:, lane0:lane0 + block.shape[1]].set(block)
        return out

    w_sm = small_tile({GATE_I_LANE: w_if[:, :M_HEADS], GATE_F_LANE: w_if[:, M_HEADS:], IDX_W_LANE: w_ixw}, d)
    w_ik2 = jnp.concatenate([w_ixk, w_ixk], axis=1)
    ws = [jnp.concatenate([col(0), col(1)], axis=1), col(2), col(3), col(4), w_sm, w_ik2,
          col(6), col(7), col(8), col(9), col(10)]
    ws = [w.astype(BF16) for w in ws]
    bg = b_gate[0][None, :]
    bpad = small_tile({GATE_I_LANE: bg[:, :M_HEADS], GATE_F_LANE: bg[:, M_HEADS:]}, 1)

    mqk, mv, gate, sm, ik2, aq, ak, avt, az, iq = _inproj(x2, norm_in[0][None, :], ws, batch, seq)

    hm = _mlstm(mqk, mv, gate, sm, conv_w[0], bpad, m_norm[0][None, :], batch, seq)

    ha = _dsa(rel_bias, iq, sm, aq, az, ik2, ak, avt, batch, seq)

    wo = w_out[0].astype(BF16)
    out = _outproj(x2, hm, ha, p[0].reshape(rows, -1), wo[:M_HEADS * M_V], wo[M_HEADS * M_V:],
                   w_ple[0].astype(BF16), w_ple_gate[0].astype(BF16), norm_final[None, :])
    return out.reshape(batch, seq, d)
```

```python
import math

import numpy as np
import jax
import jax.numpy as jnp
from jax import lax
from jax.experimental import pallas as pl
from jax.experimental.pallas import tpu as pltpu

F32 = jnp.float32
BF16 = jnp.bfloat16
I32 = jnp.int32

M_HEADS = 4
M_QK = 128
M_V = 256
CONV_W = 4
A_HEADS = 8
A_DH = 64
IDX_HEADS = 8
IDX_DH = 64
TOPK_MAX = 256
REL_BUCKETS = 32
REL_MAX_DIST = 128
EPS = 1e-6

LOG2E = 1.4426950408889634
NEG_BIG = -3.0e38
LOGIT_NEG = -1.0e30

ROW_TILE = 512
M_CHUNK = 512
Q_TILE = 256
K_TILE = 256
ONES_ROWS = 16
CONV_HALO = 8
LANES = 128
GATE_I_LANE = 0
GATE_F_LANE = 8
IDX_W_LANE = 16
assert GATE_I_LANE == 0
BISECT_MAX_ITERS = 256
VMEM_LIMIT = 56 * 1024 * 1024


def _silu(v):
    return v * jax.nn.sigmoid(v)


def _dot(a, b):
    return jnp.dot(a, b, preferred_element_type=F32)


def _dot_nt(a, b):
    return lax.dot_general(a, b, (((1,), (1,)), ((), ())), preferred_element_type=F32)


def _dot_tn(a, b):
    return lax.dot_general(a, b, (((0,), (0,)), ((), ())), preferred_element_type=F32)


def _resident(shape):
    nd = len(shape)
    return pl.BlockSpec(shape, lambda *_: (0,) * nd, pipeline_mode=pl.Buffered(1))


def _inproj_body(x_ref, g_ref, w_mqk, w_mv, w_mo, w_mz, w_sm, w_ik, w_aq, w_ak, w_av, w_az, w_iq,
                 o_mqk, o_mv, o_gate, o_sm, o_ik, o_aq, o_ak, o_avt, o_az, o_iq):
    x = x_ref[...]
    ms = jnp.mean(x * x, axis=-1, keepdims=True)
    xn = (x * lax.rsqrt(ms + EPS) * g_ref[...]).astype(BF16)
    o_mqk[...] = _dot(xn, w_mqk[...])
    o_mv[...] = _dot(xn, w_mv[...]).astype(BF16)
    o_gate[...] = jax.nn.sigmoid(_dot(xn, w_mo[...])) * _silu(_dot(xn, w_mz[...]))
    sm = _dot(xn, w_sm[...])
    o_sm[...] = sm
    o_ik[...] = _dot(xn, w_ik[...]).astype(BF16)
    o_aq[...] = (_dot(xn, w_aq[...]) * (A_DH ** -0.5 * LOG2E)).astype(BF16)
    o_ak[...] = _dot(xn, w_ak[...]).astype(BF16)
    o_avt[...] = _dot(xn, w_av[...]).T.astype(BF16)
    o_az[...] = _silu(_dot(xn, w_az[...]))
    iq = _dot(xn, w_iq[...])
    wsc = sm[:, IDX_W_LANE:IDX_W_LANE + IDX_HEADS] * (IDX_HEADS ** -0.5 * IDX_DH ** -0.5)
    lane = lax.broadcasted_iota(I32, (x.shape[0], 2 * IDX_DH), 1)
    for pair in range(IDX_HEADS // 2):
        wpair = jnp.where(lane < IDX_DH, wsc[:, 2 * pair:2 * pair + 1], wsc[:, 2 * pair + 1:2 * pair + 2])
        sl = slice(pair * 2 * IDX_DH, (pair + 1) * 2 * IDX_DH)
        o_iq[:, sl] = (iq[:, sl] * wpair).astype(BF16)


def _inproj(x2, g, ws, batch, seq):
    rows, d = x2.shape
    tm = ROW_TILE
    per_seq = seq // tm
    out_dtypes = [F32, BF16, F32, F32, BF16, BF16, BF16, BF16, F32, BF16]
    widths = [w.shape[1] for w in ws]
    out_widths = [widths[0], widths[1], widths[2]] + widths[4:]
    in_specs = [pl.BlockSpec((tm, d), lambda i: (i, 0)), _resident((1, d))]
    in_specs += [_resident(w.shape) for w in ws]
    out_specs = [pl.BlockSpec((tm, n), lambda i: (i, 0)) for n in out_widths]
    out_shape = [jax.ShapeDtypeStruct((rows, n), dt) for n, dt in zip(out_widths, out_dtypes)]
    av_pos = 7
    out_specs[av_pos] = pl.BlockSpec((None, out_widths[av_pos], tm), lambda i: (i // per_seq, 0, i % per_seq))
    out_shape[av_pos] = jax.ShapeDtypeStruct((batch, out_widths[av_pos], seq), BF16)
    return pl.pallas_call(
        _inproj_body,
        grid=(rows // tm,),
        in_specs=in_specs,
        out_specs=out_specs,
        out_shape=out_shape,
        compiler_params=pltpu.CompilerParams(
            dimension_semantics=("arbitrary",), vmem_limit_bytes=VMEM_LIMIT),
        name="inproj",
    )(x2, g, *ws)


def _row_scan(v, op, fill):
    n = v.shape[0]
    row = lax.broadcasted_iota(I32, v.shape, 0)
    sh = 1
    while sh < n:
        v = op(v, jnp.where(row >= sh, pltpu.roll(v, sh, 0), fill))
        sh *= 2
    return v


def _mlstm_body(mqk_ref, mv_ref, gate_ref, sm_ref, convw_ref, bpad_ref, mnorm_ref, out_ref,
                halo_ref, c_ref, n_ref, m_ref):
    L = M_CHUNK
    H, DK, DV = M_HEADS, M_QK, M_V

    @pl.when(pl.program_id(1) == 0)
    def _():
        halo_ref[...] = jnp.zeros(halo_ref.shape, F32)
        c_ref[...] = jnp.zeros(c_ref.shape, F32)
        n_ref[...] = jnp.zeros(n_ref.shape, F32)
        m_ref[...] = jnp.zeros(m_ref.shape, F32)

    raw = mqk_ref[...]
    halo = halo_ref[...]
    row8 = lax.broadcasted_iota(I32, halo.shape, 0)
    y = raw * convw_ref[CONV_W - 1:CONV_W, :]
    for back in range(1, CONV_W):
        rot = pltpu.roll(raw, back, 0)
        head = jnp.where(row8 < back, pltpu.roll(halo, back, 0), rot[0:CONV_HALO, :])
        tap = jnp.concatenate([head, rot[CONV_HALO:, :]], axis=0)
        y = y + tap * convw_ref[CONV_W - 1 - back:CONV_W - back, :]
    halo_ref[...] = raw[L - CONV_HALO:L, :]
    qk = _silu(y)

    g = sm_ref[...] + bpad_ref[...]
    lf = jax.nn.log_sigmoid(pltpu.roll(g, LANES - (GATE_F_LANE - GATE_I_LANE), 1))
    b = _row_scan(lf, jnp.add, 0.0)
    a = g - b
    m_prev = m_ref[0:1, :]
    mcol = jnp.maximum(m_prev, _row_scan(a, jnp.maximum, -jnp.inf))
    w_inter = jnp.exp(m_prev - mcol)
    e_neg_mt = jnp.exp(-(b + mcol))
    m_last = mcol[L - 1:L, :]
    wk = jnp.exp(a - m_last)
    decay = jnp.exp(m_prev - m_last)
    m_new = b[L - 1:L, :] + m_last
    a_rows = a.T

    ri = lax.broadcasted_iota(I32, (L, L), 0)
    ci = lax.broadcasted_iota(I32, (L, L), 1)
    causal = ci <= ri

    heads = range(H)
    q = [qk[:, h * DK:(h + 1) * DK] for h in heads]
    k = [qk[:, (H + h) * DK:(H + h + 1) * DK] * (DK ** -0.5) for h in heads]
    qb = [q[h].astype(BF16) for h in heads]
    v = [mv_ref[:, h * DV:(h + 1) * DV] for h in heads]
    c_old = [c_ref[h] for h in heads]
    qkt = [_dot_nt(qb[h], k[h].astype(BF16)) for h in heads]
    q_c = [_dot(qb[h], c_old[h].astype(BF16)) for h in heads]
    s = [qkt[h] * jnp.where(causal, jnp.exp(a_rows[h:h + 1, :] - mcol[:, h:h + 1]), 0.0) for h in heads]
    num = [_dot(s[h].astype(BF16), v[h]) + w_inter[:, h:h + 1] * q_c[h] for h in heads]
    kw = [k[h] * wk[:, h:h + 1] for h in heads]
    upd = [_dot_tn(kw[h].astype(BF16), v[h]) for h in heads]
    for h in heads:
        den = (jnp.sum(s[h], axis=1, keepdims=True)
               + w_inter[:, h:h + 1] * jnp.sum(q[h] * n_ref[h:h + 1, :], axis=1, keepdims=True))
        den = jnp.maximum(jnp.abs(den), e_neg_mt[:, h:h + 1])
        hh = num[h] / den
        hn = hh * lax.rsqrt(jnp.mean(hh * hh, axis=-1, keepdims=True) + EPS)
        hn = hn * mnorm_ref[:, h * DV:(h + 1) * DV]
        out_ref[:, h * DV:(h + 1) * DV] = (gate_ref[:, h * DV:(h + 1) * DV] * hn).astype(BF16)
    for h in heads:
        dec = decay[:, h:h + 1]
        c_ref[h] = dec * c_old[h] + upd[h]
        n_ref[h:h + 1, :] = dec * n_ref[h:h + 1, :] + jnp.sum(kw[h], axis=0, keepdims=True)
    m_ref[...] = jnp.broadcast_to(m_new, m_ref.shape)


def _mlstm(mqk, mv, gate, sm, conv_w, bpad, mnorm, batch, seq):
    L = M_CHUNK
    nc = seq // L
    wqk = mqk.shape[1]
    wv = mv.shape[1]
    row = lambda b, c: (b * nc + c, 0)
    return pl.pallas_call(
        _mlstm_body,
        grid=(batch, nc),
        in_specs=[
            pl.BlockSpec((L, wqk), row),
            pl.BlockSpec((L, wv), row),
            pl.BlockSpec((L, wv), row),
            pl.BlockSpec((L, 128), row),
            _resident(conv_w.shape),
            _resident(bpad.shape),
            _resident(mnorm.shape),
        ],
        out_specs=pl.BlockSpec((L, wv), row),
        out_shape=jax.ShapeDtypeStruct((batch * seq, wv), BF16),
        scratch_shapes=[
            pltpu.VMEM((CONV_HALO, wqk), F32),
            pltpu.VMEM((M_HEADS, M_QK, M_V), F32),
            pltpu.VMEM((8, M_QK), F32),
            pltpu.VMEM((8, 128), F32),
        ],
        compiler_params=pltpu.CompilerParams(
            dimension_semantics=("arbitrary", "arbitrary"), vmem_limit_bytes=VMEM_LIMIT),
        name="mlstm",
    )(mqk, mv, gate, sm, conv_w, bpad, mnorm)


def _t5_bucket_table(max_dist):
    d = np.arange(max_dist)
    max_exact = REL_BUCKETS // 2
    tabs = []
    for dt in (np.float32, np.float64):
        ratio = np.maximum(d, 1).astype(dt) / dt(max_exact)
        large = max_exact + (np.log(ratio) / dt(math.log(REL_MAX_DIST / max_exact))
                             * dt(REL_BUCKETS - max_exact)).astype(np.int32)
        large = np.minimum(large, REL_BUCKETS - 1)
        tabs.append(np.where(d < max_exact, d, large).astype(np.int32))
    assert np.array_equal(tabs[0], tabs[1])
    return tabs[0]


def _bucket_maps():
    tab = _t5_bucket_table(Q_TILE + K_TILE)
    s = np.arange(K_TILE)[:, None]
    t = np.arange(Q_TILE)[None, :]
    diag = tab[np.maximum(t - s, 0)]
    prev = tab[K_TILE + t - s]
    assert np.all(tab[K_TILE:] == REL_BUCKETS - 1)
    return np.stack([prev, diag]).astype(np.int32)


def _dsa_body(rb_ref, iq_ref, sm_ref, aq_ref, az_ref, ik_ref, k_ref, vt_ref, map_ref, out_ref,
              sc_ref, stat_ref, iqm_ref, qm_ref, bias_ref, acc_ref, m_ref, l_ref, alpha_ref, mtile_ref, lg_ref,
              p_ref):
    TQ, TK, H, DH = Q_TILE, K_TILE, A_HEADS, A_DH
    j = pl.program_id(1)
    n_tiles = j + 1

    @pl.when((pl.program_id(0) == 0) & (j == 0))
    def _():
        bias_ref[...] = jnp.zeros(bias_ref.shape, F32)

        def fill(bkt, carry):
            for which in range(2):
                hit = map_ref[which] == bkt
                for h in range(H):
                    val = (rb_ref[bkt, h] - rb_ref[REL_BUCKETS - 1, h]) * LOG2E
                    bias_ref[which, h] = jnp.where(hit, val, bias_ref[which, h])
            return carry

        lax.fori_loop(0, REL_BUCKETS - 1, fill, 0)

    lane = lax.broadcasted_iota(I32, (TQ, 2 * DH), 1)
    for h in range(H):
        p0 = (h // 2) * 2 * DH
        keep = (lane < DH) if h % 2 == 0 else (lane >= DH)
        iqm_ref[h] = jnp.where(keep, iq_ref[:, p0:p0 + 2 * DH], jnp.zeros((), BF16))
        qm_ref[h] = jnp.where(keep, aq_ref[:, p0:p0 + 2 * DH], jnp.zeros((), BF16))
    w_rows = sm_ref[...].T[IDX_W_LANE:IDX_W_LANE + IDX_HEADS, :]
    clamp_lo = jnp.where(w_rows >= 0.0, 0.0, -jnp.inf)
    clamp_hi = jnp.where(w_rows >= 0.0, jnp.inf, 0.0)

    s_loc = lax.broadcasted_iota(I32, (TK, TQ), 0)
    t_loc = lax.broadcasted_iota(I32, (TK, TQ), 1)

    fold = lambda v: v.reshape(v.shape[0] // 8, 8, TQ)

    def index_tile(kb, diag, stats):
        r0 = pl.multiple_of(kb * TK, TK)
        lhs = ik_ref[pl.ds(r0, TK), :]
        sc = None
        for h in range(H):
            y = _dot_nt(lhs, iqm_ref[h])
            term = jnp.minimum(jnp.maximum(y, clamp_lo[h:h + 1, :]), clamp_hi[h:h + 1, :])
            sc = term if sc is None else sc + term
        sc_min = sc
        if diag:
            admissible = s_loc <= t_loc
            sc_min = jnp.where(admissible, sc, -NEG_BIG)
            sc = jnp.where(admissible, sc, NEG_BIG)
        sc_ref[pl.ds(r0, TK), :] = sc
        c_pos_, c_nn_, mx_, mn_ = stats
        c_pos_ = c_pos_ + jnp.sum(fold(jnp.where(sc > 0.0, 1.0, 0.0)), axis=0)
        c_nn_ = c_nn_ + jnp.sum(fold(jnp.where(sc >= 0.0, 1.0, 0.0)), axis=0)
        mx_ = jnp.maximum(mx_, jnp.max(fold(sc), axis=0))
        mn_ = jnp.minimum(mn_, jnp.min(fold(sc_min), axis=0))
        return c_pos_, c_nn_, mx_, mn_

    stat_ref[0] = jnp.zeros((8, TQ), F32)
    stat_ref[1] = jnp.zeros((8, TQ), F32)
    stat_ref[2] = jnp.full((8, TQ), NEG_BIG, F32)
    stat_ref[3] = jnp.full((8, TQ), -NEG_BIG, F32)

    def index_tiles(tiles):
        st = tuple(stat_ref[i] for i in range(4))
        for kb, diag in tiles:
            st = index_tile(kb, diag, st)
        for i in range(4):
            stat_ref[i] = st[i]

    def index_quads(i, carry):
        index_tiles([(4 * i + o, False) for o in range(4)])
        return carry

    n_quads = j >> 2
    lax.fori_loop(0, n_quads, index_quads, 0)
    left = j & 3

    for n_left in range(4):
        @pl.when(left == n_left)
        def _(n_left=n_left):
            index_tiles([(j - o, False) for o in range(n_left, 0, -1)] + [(j, True)])

    c_pos = jnp.sum(stat_ref[0], axis=0, keepdims=True).astype(I32)
    c_nn = jnp.sum(stat_ref[1], axis=0, keepdims=True).astype(I32)
    hi0 = jnp.max(stat_ref[2], axis=0, keepdims=True)
    mn = jnp.min(stat_ref[3], axis=0, keepdims=True)

    @pl.when((n_tiles & 1) == 1)
    def _():
        sc_ref[pl.ds(pl.multiple_of(n_tiles * TK, TK), TK), :] = jnp.full((TK, TQ), NEG_BIG, F32)

    n_pairs = (n_tiles + 1) >> 1

    def count_gt(thr):
        def body(kp, cnt):
            r0 = pl.multiple_of(kp * (2 * TK), 2 * TK)
            hit = jnp.where(sc_ref[pl.ds(r0, 2 * TK), :] > thr, 1, 0).astype(I32)
            return cnt + jnp.sum(hit.reshape(2 * TK // 32, 32, TQ), axis=0)
        cnt32 = lax.fori_loop(1, n_pairs, body, body(0, jnp.zeros((32, TQ), I32)))
        return jnp.sum(cnt32, axis=0, keepdims=True)

    t_glob = j * TQ + lax.broadcasted_iota(I32, (1, TQ), 1)
    k_eff = jnp.minimum(TOPK_MAX, t_glob + 1)

    zero = jnp.zeros((1, TQ), F32)
    all_in = (t_glob + 1) <= TOPK_MAX
    pos_ok = c_pos >= k_eff
    tie0 = (~all_in) & (~pos_ok) & (c_nn >= k_eff)
    lo_neg = jnp.maximum(mn + mn - 1.0, NEG_BIG)
    lo = jnp.where(all_in, NEG_BIG, jnp.where(pos_ok, zero, lo_neg))
    hi = jnp.where((~all_in) & (~pos_ok), zero, hi0)
    c_lo = jnp.where(pos_ok, c_pos, t_glob + 1)
    c_hi = jnp.where(pos_ok, 0, c_pos)
    fin0 = all_in | (pos_ok & (c_pos == k_eff)) | tie0

    def open_rows(fin_, c_lo_, c_hi_):
        return (fin_ == 0.0) & ((c_lo_ - c_hi_) > 2)

    def bis_cond(st):
        return (st[0] < BISECT_MAX_ITERS) & (st[1] > 0)

    def halve(lo_, hi_, c_lo_, c_hi_, fin_, tie_):
        mid = 0.5 * lo_ + 0.5 * hi_
        collapsed = (mid <= lo_) | (mid >= hi_)
        cm = count_gt(mid)
        act = fin_ == 0.0
        up = act & (~collapsed) & (cm >= k_eff)
        dn = act & (~collapsed) & (cm < k_eff)
        tie_n = jnp.where(act & collapsed, 1.0, tie_)
        fin_n = jnp.where((act & collapsed) | (up & (cm == k_eff)), 1.0, fin_)
        return (jnp.where(up, mid, lo_), jnp.where(dn, mid, hi_),
                jnp.where(up, cm, c_lo_), jnp.where(dn, cm, c_hi_), fin_n, tie_n)

    def bis_body(st):
        it, _, lo_, hi_, c_lo_, c_hi_, fin_, tie_ = st
        go_n = (jnp.max(jnp.where(open_rows(fin_, c_lo_, c_hi_), 1.0, 0.0)) > 0.0).astype(I32)
        state = halve(*halve(lo_, hi_, c_lo_, c_hi_, fin_, tie_))
        return (it + 2, go_n) + state

    _, _, lo, hi, c_lo, c_hi, fin_f, tie_f = lax.while_loop(
        bis_cond, bis_body,
        (jnp.int32(0), jnp.int32(1), lo, hi, c_lo, c_hi, jnp.where(fin0, 1.0, 0.0), jnp.where(tie0, 1.0, 0.0)))

    def pick_two():
        def body(kp, carry):
            e_min, e_max = carry
            r0 = pl.multiple_of(kp * (2 * TK), 2 * TK)
            sc = sc_ref[pl.ds(r0, 2 * TK), :]
            above = jnp.where(sc > lo, sc, -NEG_BIG).reshape(2 * TK // 32, 32, TQ)
            below_hi = jnp.where(sc <= hi, sc, NEG_BIG).reshape(2 * TK // 32, 32, TQ)
            return jnp.minimum(e_min, jnp.min(above, axis=0)), jnp.maximum(e_max, jnp.max(below_hi, axis=0))
        init = (jnp.full((32, TQ), -NEG_BIG, F32), jnp.full((32, TQ), NEG_BIG, F32))
        e_min, e_max = lax.fori_loop(0, n_pairs, body, init)
        return jnp.min(e_min, axis=0, keepdims=True), jnp.max(e_max, axis=0, keepdims=True)

    e1, e2 = pick_two()
    left = fin_f == 0.0
    lo = jnp.where(left & (e1 < e2), e1, lo)
    hi = jnp.where(left & (e1 >= e2), e1, hi)
    tie_f = jnp.where(left & (e1 >= e2), 1.0, tie_f)
    tie = tie_f > 0.0
    theta_ref = m_ref
    theta_ref[0:1, :] = lo

    @pl.when(jnp.max(tie_f) > 0.0)
    def _():
        v = jnp.where(tie, hi, jnp.inf)
        need = (k_eff - c_hi).astype(F32)
        tri = jnp.where(lax.broadcasted_iota(I32, (TK, TK), 1) < lax.broadcasted_iota(I32, (TK, TK), 0),
                        1.0, 0.0).astype(BF16)

        def tie_pair(kp, seen):
            r0 = pl.multiple_of(kp * (2 * TK), 2 * TK)
            sc2 = sc_ref[pl.ds(r0, 2 * TK), :]
            out = []
            for half in range(2):
                sc = sc2[half * TK:(half + 1) * TK, :]
                eqf = jnp.where(sc == v, 1.0, 0.0)
                rank = seen + _dot(tri, eqf.astype(BF16))
                out.append(jnp.where(eqf * (need - rank) > 0.0, -NEG_BIG, sc))
                seen = seen + jnp.sum(eqf, axis=0, keepdims=True)
            sc_ref[pl.ds(r0, 2 * TK), :] = jnp.concatenate(out, axis=0)
            return seen

        lax.fori_loop(0, n_pairs, tie_pair, jnp.zeros((1, TQ), F32))
        theta_ref[0:1, :] = jnp.where(tie, hi, lo)

    theta = theta_ref[0:1, :]

    m_ref[...] = jnp.full(m_ref.shape, LOGIT_NEG, F32)
    l_ref[...] = jnp.zeros(l_ref.shape, F32)
    acc_ref[...] = jnp.zeros(acc_ref.shape, F32)

    def tile_mask(kb):
        r0 = pl.multiple_of(kb * TK, TK)
        return jnp.where(sc_ref[pl.ds(r0, TK), :] > theta, 0.0, LOGIT_NEG)

    def logits_head(kb, slot, near, mb, h):
        r0 = pl.multiple_of(kb * TK, TK)
        p0 = (h // 2) * 2 * DH
        lg = _dot_nt(k_ref[pl.ds(r0, TK), p0:p0 + 2 * DH], qm_ref[h]) + mb
        if near is not None:
            lg = lg + bias_ref[near, h]
        lg_ref[slot, h] = lg
        m_old = m_ref[h:h + 1, :]
        m_new = jnp.maximum(m_old, jnp.max(lg, axis=0, keepdims=True))
        alpha_ref[slot, h:h + 1, :] = jnp.exp2(m_old - m_new)
        mtile_ref[slot, h:h + 1, :] = m_new
        m_ref[h:h + 1, :] = m_new

    def exp_head(slot, h):
        p_ref[slot, h] = jnp.exp2(lg_ref[slot, h] - mtile_ref[slot, h:h + 1, :]).astype(BF16)

    def pv_head(kb, slot, h):
        r0 = pl.multiple_of(kb * TK, TK)
        ones = jnp.ones((ONES_ROWS, TK), BF16)
        lhs = jnp.concatenate([vt_ref[h * DH:(h + 1) * DH, pl.ds(r0, TK)], ones], axis=0)
        pv = _dot(lhs, p_ref[slot, h])
        alpha = alpha_ref[slot, h:h + 1, :]
        acc_ref[h * DH:(h + 1) * DH, :] = alpha * acc_ref[h * DH:(h + 1) * DH, :] + pv[0:DH, :]
        l_ref[h:h + 1, :] = alpha * l_ref[h:h + 1, :] + pv[DH:DH + 1, :]

    def fill(kb, slot, near):
        mb = tile_mask(kb)
        for h in range(H):
            logits_head(kb, slot, near, mb, h)

    def drain(kb, slot):
        for h in range(H):
            exp_head(slot, h)
        for h in range(H):
            pv_head(kb, slot, h)

    def step(kb_prev, slot_prev, kb_next, slot_next, near_next):
        mb = tile_mask(kb_next)
        for h in range(H):
            exp_head(slot_prev, h)
            logits_head(kb_next, slot_next, near_next, mb, h)
            pv_head(kb_prev, slot_prev, h)

    n_far = j - 1

    @pl.when(n_far >= 1)
    def _():
        fill(0, 0, None)

        def far_loop(t, carry):
            step(2 * t, 0, 2 * t + 1, 1, None)
            step(2 * t + 1, 1, 2 * t + 2, 0, None)
            return carry

        lax.fori_loop(0, (n_far - 1) >> 1, far_loop, 0)

        @pl.when(((n_far - 1) & 1) == 1)
        def _():
            step(j - 3, 0, j - 2, 1, None)
            step(j - 2, 1, j - 1, 0, 0)
            step(j - 1, 0, j, 1, 1)
            drain(j, 1)

        @pl.when(((n_far - 1) & 1) == 0)
        def _():
            step(j - 2, 0, j - 1, 1, 0)
            step(j - 1, 1, j, 0, 1)
            drain(j, 0)

    @pl.when(j == 1)
    def _():
        fill(0, 0, 0)
        step(0, 0, 1, 1, 1)
        drain(1, 1)

    @pl.when(j == 0)
    def _():
        fill(0, 0, 1)
        drain(0, 0)

    for h in range(H):
        acc_ref[h * DH:(h + 1) * DH, :] = acc_ref[h * DH:(h + 1) * DH, :] / l_ref[h:h + 1, :]
    out_ref[...] = (acc_ref[...].T * az_ref[...]).astype(BF16)


def _dsa(rel_bias, iq, sm, aq, az, ik2, ak, avt, batch, seq):
    TQ = Q_TILE
    nq = seq // TQ
    width = A_HEADS * A_DH
    maps = jnp.asarray(_bucket_maps())
    qrow = lambda b, j: (b * nq + j, 0)
    return pl.pallas_call(
        _dsa_body,
        grid=(batch, nq),
        in_specs=[
            pl.BlockSpec(memory_space=pltpu.SMEM),
            pl.BlockSpec((TQ, width), qrow),
            pl.BlockSpec((TQ, 128), qrow),
            pl.BlockSpec((TQ, width), qrow),
            pl.BlockSpec((TQ, width), qrow),
            pl.BlockSpec((seq, 128), lambda b, j: (b, 0), pipeline_mode=pl.Buffered(1)),
            pl.BlockSpec((seq, width), lambda b, j: (b, 0), pipeline_mode=pl.Buffered(1)),
            pl.BlockSpec((None, width, seq), lambda b, j: (b, 0, 0), pipeline_mode=pl.Buffered(1)),
            _resident(maps.shape),
        ],
        out_specs=pl.BlockSpec((TQ, width), qrow),
        out_shape=jax.ShapeDtypeStruct((batch * seq, width), BF16),
        scratch_shapes=[
            pltpu.VMEM((seq, TQ), F32),
            pltpu.VMEM((4, 8, TQ), F32),
            pltpu.VMEM((A_HEADS, TQ, 2 * A_DH), BF16),
            pltpu.VMEM((A_HEADS, TQ, 2 * A_DH), BF16),
            pltpu.VMEM((2, A_HEADS, K_TILE, TQ), F32),
            pltpu.VMEM((A_HEADS * A_DH, TQ), F32),
            pltpu.VMEM((8, TQ), F32),
            pltpu.VMEM((8, TQ), F32),
            pltpu.VMEM((2, 8, TQ), F32),
            pltpu.VMEM((2, 8, TQ), F32),
            pltpu.VMEM((2, A_HEADS, K_TILE, TQ), F32),
            pltpu.VMEM((2, A_HEADS, K_TILE, TQ), BF16),
        ],
        compiler_params=pltpu.CompilerParams(
            dimension_semantics=("arbitrary", "arbitrary"), vmem_limit_bytes=VMEM_LIMIT),
        name="dsa",
    )(rel_bias, iq, sm, aq, az, ik2, ak, avt, maps)


def _outproj_body(x_ref, hm_ref, ha_ref, p_ref, wom_ref, woa_ref, wple_ref, wg_ref, g_ref, o_ref):
    h1 = x_ref[...] + _dot(hm_ref[...], wom_ref[...]) + _dot(ha_ref[...], woa_ref[...])
    gate = jax.nn.sigmoid(_dot(h1.astype(BF16), wg_ref[...]))
    h2 = h1 + _dot(p_ref[...].astype(BF16), wple_ref[...]) * gate
    ms = jnp.mean(h2 * h2, axis=-1, keepdims=True)
    o_ref[...] = h2 * lax.rsqrt(ms + EPS) * g_ref[...]


def _outproj(x2, hm, ha, p2, wom, woa, wple, wg, g):
    rows, d = x2.shape
    tm = ROW_TILE
    tile = lambda n: pl.BlockSpec((tm, n), lambda i: (i, 0))
    return pl.pallas_call(
        _outproj_body,
        grid=(rows // tm,),
        in_specs=[tile(d), tile(hm.shape[1]), tile(ha.shape[1]), tile(p2.shape[1]),
                  _resident(wom.shape), _resident(woa.shape), _resident(wple.shape),
                  _resident(wg.shape), _resident(g.shape)],
        out_specs=tile(d),
        out_shape=jax.ShapeDtypeStruct((rows, d), F32),
        compiler_params=pltpu.CompilerParams(
            dimension_semantics=("arbitrary",), vmem_limit_bytes=VMEM_LIMIT),
        name="outproj",
    )(x2, hm, ha, p2, wom, woa, wple, wg, g)


def kernel(x, p, w_in, b_gate, conv_w, w_out, norm_in, m_norm, rel_bias, w_ple, w_ple_gate, norm_final):
    batch, seq, d = x.shape
    assert w_in.shape[0] == 1 and p.shape[0] == 1, "single-layer block"
    assert seq % Q_TILE == 0 and seq % M_CHUNK == 0 and (batch * seq) % ROW_TILE == 0
    rows = batch * seq
    x2 = x.reshape(rows, d)

    sizes = [M_HEADS * M_QK, M_HEADS * M_QK, M_HEADS * M_V, M_HEADS * M_V, M_HEADS * M_V, 2 * M_HEADS,
             A_HEADS * A_DH, A_HEADS * A_DH, A_HEADS * A_DH, A_HEADS * A_DH,
             IDX_HEADS * IDX_DH, IDX_DH, IDX_HEADS]
    offs = np.concatenate([[0], np.cumsum(sizes)])
    wi = w_in[0]
    col = lambda i: wi[:, int(offs[i]):int(offs[i + 1])]
    w_if, w_ixk, w_ixw = col(5), col(11), col(12)

    def small_tile(parts, n_rows):
        out = jnp.zeros((n_rows, LANES), F32)
        for lane0, block in parts.items():
            out = out.at[:, lane0:lane0 + block.shape[1]].set(block)
        return out

    w_sm = small_tile({GATE_I_LANE: w_if[:, :M_HEADS], GATE_F_LANE: w_if[:, M_HEADS:], IDX_W_LANE: w_ixw}, d)
    w_ik2 = jnp.concatenate([w_ixk, w_ixk], axis=1)
    ws = [jnp.concatenate([col(0), col(1)], axis=1), col(2), col(3), col(4), w_sm, w_ik2,
          col(6), col(7), col(8), col(9), col(10)]
    ws = [w.astype(BF16) for w in ws]
    bg = b_gate[0][None, :]
    bpad = small_tile({GATE_I_LANE: bg[:, :M_HEADS], GATE_F_LANE: bg[:, M_HEADS:]}, 1)

    mqk, mv, gate, sm, ik2, aq, ak, avt, az, iq = _inproj(x2, norm_in[0][None, :], ws, batch, seq)

    hm = _mlstm(mqk, mv, gate, sm, conv_w[0], bpad, m_norm[0][None, :], batch, seq)

    ha = _dsa(rel_bias, iq, sm, aq, az, ik2, ak, avt, batch, seq)

    wo = w_out[0].astype(BF16)
    out = _outproj(x2, hm, ha, p[0].reshape(rows, -1), wo[:M_HEADS * M_V], wo[M_HEADS * M_V:],
                   w_ple[0].astype(BF16), w_ple_gate[0].astype(BF16), norm_final[None, :])
    return out.reshape(batch, seq, d)
```
